```python
import jax, jax.numpy as jnp
from jax import lax
import numpy as np

D_MODEL = 2048
BATCH = 2
SEQ = 8192
DEPTH = 4

CHUNK = 64
N_BRANCH = 4
W_BRANCH = 1024
NORM_EPS = 1e-6
CONV_WIDTH = 3
RWKV_HEAD = 64
RWKV_HEADS = W_BRANCH // RWKV_HEAD
DECAY_LORA = 64
AAA_LORA = 64
RWKV_LN_EPS = 64e-5
FOX_HEAD = 64
FOX_HEADS = W_BRANCH // FOX_HEAD
Q_BLOCK = 128
POOL_WINDOWS = (2, 4, 8, 16)
POOL_GROUPS = len(POOL_WINDOWS)
POOL_GROUP_W = W_BRANCH // POOL_GROUPS

SHIFT_W = 3 * W_BRANCH + DECAY_LORA + AAA_LORA
IN_SIZES = (
    4 * W_BRANCH,
    SHIFT_W, W_BRANCH,
    3 * W_BRANCH, FOX_HEADS, W_BRANCH,
    W_BRANCH, W_BRANCH,
    N_BRANCH * D_MODEL,
)
N_IN = sum(IN_SIZES)

kernel_name = 'hybrid_gated_stream_encoder'


def _rms_norm(x, g):
    xf = x.astype(jnp.float32)
    y = xf * lax.rsqrt(jnp.mean(xf * xf, axis=-1, keepdims=True) + NORM_EPS)
    return (y * g.astype(jnp.float32)).astype(x.dtype)


def _split(u, sizes):
    idx = [int(i) for i in np.cumsum(sizes)[:-1]]
    return jnp.split(u, idx, axis=-1)


def _short_conv_branch(u, conv_w):
    b_gate, c_gate, xv, g = _split(u, (W_BRANCH,) * 4)
    z = lax.conv_general_dilated(
        c_gate * xv, conv_w[:, None, :].astype(u.dtype), window_strides=(1,),
        padding=[(CONV_WIDTH - 1, 0)], dimension_numbers=('NWC', 'WIO', 'NWC'),
        feature_group_count=W_BRANCH)
    return b_gate * z * jax.nn.silu(g)


def _rwkv7_scan(r, decay, k, v, a, b):
    bsz, _, nh, n = r.shape
    xs = tuple(jnp.moveaxis(t, 1, 0) for t in (r, decay, k, v, a, b))

    def step(state, inp):
        r_t, w_t, k_t, v_t, a_t, b_t = inp
        sa = jnp.einsum('bhvk,bhk->bhv', state, a_t)
        state = (state * w_t[:, :, None, :] + sa[..., None] * b_t[:, :, None, :]
                 + v_t[..., None] * k_t[:, :, None, :])
        return state, jnp.einsum('bhvk,bhk->bhv', state, r_t)

    s0 = jnp.zeros((bsz, nh, n, n), jnp.float32)
    _, y = lax.scan(step, s0, xs)
    return jnp.moveaxis(y, 0, 1)


def _rwkv7_branch(u, g, mu, w0, w2, a0, a2, k_k, k_a, r_k, ln_g, ln_b):
    bsz, s, _ = u.shape
    uf = u.astype(jnp.float32)
    prev = jnp.pad(uf, ((0, 0), (1, 0), (0, 0)))[:, :-1]
    xm = uf + (prev - uf) * mu.astype(jnp.float32)
    r, k, v, wl, al = _split(xm, (W_BRANCH, W_BRANCH, W_BRANCH, DECAY_LORA, AAA_LORA))
    log_w = -jax.nn.softplus(-(w0 + jnp.einsum('bsr,rc->bsc', jnp.tanh(wl), w2))) - 0.5
    decay = jnp.exp(-jnp.exp(log_w))
    a = jax.nn.sigmoid(a0 + jnp.einsum('bsr,rc->bsc', al, a2))
    heads = lambda t: t.reshape(bsz, s, RWKV_HEADS, RWKV_HEAD)
    kk = heads(k * k_k)
    kk = kk / jnp.maximum(jnp.linalg.norm(kk, axis=-1, keepdims=True), 1e-12)
    k = k * (1.0 + (a - 1.0) * k_a)
    r, decay, k, v, a = map(heads, (r, decay, k, v, a))
    y = _rwkv7_scan(r, decay, k, v, -kk, kk * a)
    mean = jnp.mean(y, axis=-1, keepdims=True)
    var = jnp.mean(jnp.square(y - mean), axis=-1, keepdims=True)
    y = ((y - mean) * lax.rsqrt(var + RWKV_LN_EPS)).reshape(bsz, s, W_BRANCH) * ln_g + ln_b
    bonus = jnp.sum(r * k * r_k, axis=-1, keepdims=True) * v
    y = y + bonus.reshape(bsz, s, W_BRANCH)
    return y.astype(u.dtype) * jax.nn.silu(g)


def _fox_branch(qkv, f_logit, g, b_f):
    bsz, s, _ = qkv.shape
    q, k, v = [t.reshape(bsz, s, FOX_HEADS, FOX_HEAD).transpose(0, 2, 1, 3)
               for t in _split(qkv, (W_BRANCH,) * 3)]
    log_f = jax.nn.log_sigmoid(f_logit.astype(jnp.float32) + b_f.astype(jnp.float32))
    c = jnp.cumsum(log_f, axis=1).transpose(0, 2, 1)
    scale = FOX_HEAD ** -0.5
    outs = []
    for i in range(s // Q_BLOCK):
        q0, q1 = i * Q_BLOCK, (i + 1) * Q_BLOCK
        logits = jnp.einsum('bhqd,bhkd->bhqk', q[:, :, q0:q1], k[:, :, :q1]).astype(jnp.float32) * scale
        logits = logits + c[:, :, q0:q1, None] - c[:, :, None, :q1]
        causal = (q0 + jnp.arange(Q_BLOCK))[:, None] >= jnp.arange(q1)[None, :]
        p = jax.nn.softmax(jnp.where(causal, logits, -jnp.inf), axis=-1)
        outs.append(jnp.einsum('bhqk,bhkd->bhqd', p.astype(v.dtype), v[:, :, :q1]))
    o = jnp.concatenate(outs, axis=2).transpose(0, 2, 1, 3).reshape(bsz, s, W_BRANCH)
    return o * jax.nn.silu(g)


def _pool_branch(u, g, pool_w, pool_scale):
    bsz, s, _ = u.shape
    xg = u.astype(jnp.float32).reshape(bsz, s, POOL_GROUPS, POOL_GROUP_W)
    pos = jnp.arange(s)
    pooled = []
    for gi, win in enumerate(POOL_WINDOWS):
        xi = xg[:, :, gi]
        cs = jnp.cumsum(xi, axis=1)
        cs_prev = jnp.pad(cs, ((0, 0), (win, 0), (0, 0)))[:, :s]
        count = jnp.minimum(pos + 1, win).astype(jnp.float32)[None, :, None]
        pooled.append((cs - cs_prev) / count - xi)
    p = jnp.stack(pooled, axis=2)
    y = jnp.einsum('bsgc,gce->bsge', p, pool_w.astype(jnp.float32)).reshape(bsz, s, W_BRANCH) * pool_scale
    return y.astype(u.dtype) * jax.nn.silu(g)


def _hybrid_layer(x, norm_g, w_in, b_merge, conv_w, rwkv_mu, rwkv_w0, rwkv_w2, rwkv_a0, rwkv_a2,
                  rwkv_kk, rwkv_ka, rwkv_rk, rwkv_ln_g, rwkv_ln_b, fox_bf, pool_w, pool_scale,
                  w_branch, w_out):
    bsz, s, d = x.shape
    h = _rms_norm(x, norm_g)
    u = jnp.einsum('bsd,dn->bsn', h, w_in)
    a_in, b_in, b_gate, c_qkv, c_f, c_gate, d_in, d_gate, m_logit = _split(u, IN_SIZES)
    y_a = _short_conv_branch(a_in, conv_w)
    y_b = _rwkv7_branch(b_in, b_gate, rwkv_mu, rwkv_w0, rwkv_w2, rwkv_a0, rwkv_a2,
                        rwkv_kk, rwkv_ka, rwkv_rk, rwkv_ln_g, rwkv_ln_b)
    y_c = _fox_branch(c_qkv, c_f, c_gate, fox_bf)
    y_d = _pool_branch(d_in, d_gate, pool_w, pool_scale)
    ys = jnp.stack([y_a, y_b, y_c, y_d], axis=2)
    proj = jnp.einsum('bskc,kcd->bskd', ys, w_branch)
    gates = jax.nn.sigmoid(m_logit.reshape(bsz, s, N_BRANCH, d) + b_merge)
    merged = jnp.sum(gates * proj, axis=2)
    return x + jnp.einsum('bsd,de->bse', merged, w_out)


def setup_inputs(seed: int = 0) -> dict:
    key = jax.random.key(seed)
    ks = jax.random.split(key, 21)
    f32 = jnp.float32
    nrm = lambda k, shape, sc: sc * jax.random.normal(k, shape, f32)
    L, D, W = DEPTH, D_MODEL, W_BRANCH
    return {
        'x': nrm(ks[0], (BATCH, SEQ, D), 1.0),
        'norm_g': 1.0 + nrm(ks[1], (L, D), 0.02),
        'w_in': nrm(ks[2], (L, D, N_IN), D ** -0.5),
        'b_merge': nrm(ks[3], (L, N_BRANCH, D), 0.02),
        'conv_w': nrm(ks[4], (L, CONV_WIDTH, W), CONV_WIDTH ** -0.5),
        'rwkv_mu': jax.random.uniform(ks[5], (L, SHIFT_W), f32),
        'rwkv_w0': jax.random.uniform(ks[6], (L, W), f32, -3.0, 0.5),
        'rwkv_w2': nrm(ks[7], (L, DECAY_LORA, W), 0.1),
        'rwkv_a0': nrm(ks[8], (L, W), 0.1),
        'rwkv_a2': nrm(ks[9], (L, AAA_LORA, W), 0.1),
        'rwkv_kk': 0.85 + nrm(ks[10], (L, W), 0.05),
        'rwkv_ka': 1.0 + nrm(ks[11], (L, W), 0.05),
        'rwkv_rk': nrm(ks[12], (L, RWKV_HEADS, RWKV_HEAD), 0.1),
        'rwkv_ln_g': 1.0 + nrm(ks[13], (L, W), 0.02),
        'rwkv_ln_b': nrm(ks[14], (L, W), 0.02),
        'fox_bf': jax.random.uniform(ks[15], (L, FOX_HEADS), f32, 1.0, 5.0),
        'pool_w': nrm(ks[16], (L, POOL_GROUPS, POOL_GROUP_W, POOL_GROUP_W), POOL_GROUP_W ** -0.5),
        'pool_scale': 1.0 + nrm(ks[17], (L, W), 0.1),
        'w_branch': nrm(ks[18], (L, N_BRANCH, W, D), W ** -0.5),
        'w_out': nrm(ks[19], (L, D, D), D ** -0.5),
        'final_g': 1.0 + nrm(ks[20], (D,), 0.02),
    }


def reference(x, norm_g, w_in, b_merge, conv_w, rwkv_mu, rwkv_w0, rwkv_w2, rwkv_a0, rwkv_a2,
              rwkv_kk, rwkv_ka, rwkv_rk, rwkv_ln_g, rwkv_ln_b, fox_bf, pool_w, pool_scale,
              w_branch, w_out, final_g):
    for l in range(DEPTH):
        x = _hybrid_layer(x, norm_g[l], w_in[l], b_merge[l], conv_w[l], rwkv_mu[l], rwkv_w0[l],
                          rwkv_w2[l], rwkv_a0[l], rwkv_a2[l], rwkv_kk[l], rwkv_ka[l], rwkv_rk[l],
                          rwkv_ln_g[l], rwkv_ln_b[l], fox_bf[l], pool_w[l], pool_scale[l],
                          w_branch[l], w_out[l])
    return _rms_norm(x, final_g)
```

```python
import functools

import jax
import jax.numpy as jnp
import numpy as np
from jax import lax
from jax.experimental import pallas as pl
from jax.experimental.pallas import tpu as pltpu

F32 = jnp.float32
BF16 = jnp.bfloat16
HIGHEST = lax.Precision.HIGHEST

NORM_EPS = 1e-6
RWKV_LN_EPS = 64e-5
HEAD = 64
LORA = 64
POOL_WINDOWS = (2, 4, 8, 16)
RWKV_CHUNK = 64
LANES = 128
NEG_BIG = -1e30
VMEM_LIMIT = 48 * 1024 * 1024


def _params(sem):
    return pltpu.CompilerParams(dimension_semantics=sem, vmem_limit_bytes=VMEM_LIMIT)


def _tile(n, want):
    t = min(n, want)
    while n % t:
        t //= 2
    return t


def _sigmoid(x):
    return 1.0 / (1.0 + jnp.exp(-x))


def _silu(x):
    return x * _sigmoid(x)


def _rms_rows(xf, g):
    ms = jnp.mean(xf * xf, axis=-1, keepdims=True)
    return xf * lax.rsqrt(ms + NORM_EPS) * g


def _dot(a, b):
    return jnp.dot(a, b, preferred_element_type=F32)


def _dot_nt(a, b):
    return lax.dot_general(a, b, (((1,), (1,)), ((), ())), preferred_element_type=F32)


def _dot_f32(a, b):
    return jnp.dot(a, b, precision=HIGHEST, preferred_element_type=F32)


def _inproj_kernel(x_ref, g_ref, w_ref, o_ref, h_ref):
    @pl.when(pl.program_id(1) == 0)
    def _():
        h_ref[...] = _rms_rows(x_ref[...], g_ref[...]).astype(BF16)

    o_ref[...] = _dot(h_ref[...], w_ref[...]).astype(o_ref.dtype)


def _inproj(x2, g, w):
    t, d = x2.shape
    n = w.shape[1]
    tm, tn = _tile(t, 1024), _tile(n, 512)
    return pl.pallas_call(
        _inproj_kernel,
        grid=(t // tm, n // tn),
        in_specs=[
            pl.BlockSpec((tm, d), lambda i, j: (i, 0)),
            pl.BlockSpec((1, d), lambda i, j: (0, 0)),
            pl.BlockSpec((d, tn), lambda i, j: (0, j)),
        ],
        out_specs=pl.BlockSpec((tm, tn), lambda i, j: (i, j)),
        out_shape=jax.ShapeDtypeStruct((t, n), BF16),
        scratch_shapes=[pltpu.VMEM((tm, d), BF16)],
        compiler_params=_params(("parallel", "arbitrary")),
        name="inproj",
    )(x2, g, w)


def _side_kernel(x_ref, g_ref, ws_ref, wft_ref, bf_ref, lora_ref, ct_ref, carry_ref):
    @pl.when(pl.program_id(1) == 0)
    def _():
        carry_ref[...] = jnp.zeros_like(carry_ref)

    h = _rms_rows(x_ref[...], g_ref[...]).astype(BF16)
    lora_ref[...] = _dot(h, ws_ref[...])
    z = _dot_nt(wft_ref[...], h) + bf_ref[...]
    logf = jnp.minimum(z, 0.0) - jnp.log1p(jnp.exp(-jnp.abs(z)))
    ts = logf.shape[1]
    upper = (lax.broadcasted_iota(jnp.int32, (ts, ts), 0)
             <= lax.broadcasted_iota(jnp.int32, (ts, ts), 1)).astype(F32)
    c = _dot_f32(logf, upper) + carry_ref[:, 0:1]
    ct_ref[...] = c
    carry_ref[...] = jnp.broadcast_to(c[:, ts - 1:ts], carry_ref.shape)


def _side(x3, g, w_small, w_ft, b_f):
    b, s, d = x3.shape
    nh = w_ft.shape[0]
    ns = w_small.shape[1]
    ts = _tile(s, 512)
    return pl.pallas_call(
        _side_kernel,
        grid=(b, s // ts),
        in_specs=[
            pl.BlockSpec((None, ts, d), lambda i, j: (i, j, 0)),
            pl.BlockSpec((1, d), lambda i, j: (0, 0)),
            pl.BlockSpec((d, ns), lambda i, j: (0, 0)),
            pl.BlockSpec((nh, d), lambda i, j: (0, 0)),
            pl.BlockSpec((nh, 1), lambda i, j: (0, 0)),
        ],
        out_specs=[
            pl.BlockSpec((None, ts, ns), lambda i, j: (i, j, 0)),
            pl.BlockSpec((None, nh, ts), lambda i, j: (i, 0, j)),
        ],
        out_shape=[
            jax.ShapeDtypeStruct((b, s, ns), F32),
            jax.ShapeDtypeStruct((b, nh, s), F32),
        ],
        scratch_shapes=[pltpu.VMEM((nh, LANES), F32)],
        compiler_params=_params(("parallel", "arbitrary")),
        name="side",
    )(x3, g, w_small, w_ft, b_f)


HALO = 8


def _conv_kernel(bg_ref, cg_ref, xv_ref, g_ref, cw_ref, o_ref, pbuf):
    ts = o_ref.shape[0]

    @pl.when(pl.program_id(2) == 0)
    def _():
        pbuf[0:HALO, :] = jnp.zeros((HALO, pbuf.shape[1]), F32)

    p = cg_ref[...].astype(F32) * xv_ref[...].astype(F32)
    pbuf[HALO:HALO + ts, :] = p
    cw = cw_ref[...]
    z = (cw[0:1] * pbuf[HALO - 2:HALO - 2 + ts, :] + cw[1:2] * pbuf[HALO - 1:HALO - 1 + ts, :]
         + cw[2:3] * p)
    o_ref[...] = (bg_ref[...].astype(F32) * z * _silu(g_ref[...].astype(F32))).astype(o_ref.dtype)
    pbuf[0:HALO, :] = pbuf[ts:ts + HALO, :]


def _conv_branch(u3, conv_w, w, col0):
    b, s, _ = u3.shape
    ts, tw = _tile(s, 512), _tile(w, 512)
    nw = w // tw
    c0 = col0 // tw

    def spec(k):
        return pl.BlockSpec((None, ts, tw), lambda i, j, t, k=k: (i, t, c0 + k * nw + j))

    return pl.pallas_call(
        _conv_kernel,
        grid=(b, nw, s // ts),
        in_specs=[spec(0), spec(1), spec(2), spec(3),
                  pl.BlockSpec((conv_w.shape[0], tw), lambda i, j, t: (0, j))],
        out_specs=pl.BlockSpec((None, ts, tw), lambda i, j, t: (i, t, j)),
        out_shape=jax.ShapeDtypeStruct((b, s, w), BF16),
        scratch_shapes=[pltpu.VMEM((HALO + ts, tw), F32)],
        compiler_params=_params(("parallel", "parallel", "arbitrary")),
        name="conv_branch",
    )(u3, u3, u3, u3, conv_w)


POOL_HALO = 16


def _pool_kernel(x_ref, g_ref, pw_ref, sc_ref, o_ref, xbuf):
    ts, w = o_ref.shape
    gw = w // len(POOL_WINDOWS)
    s = pl.program_id(1)

    @pl.when(s == 0)
    def _():
        xbuf[0:POOL_HALO, :] = jnp.zeros((POOL_HALO, w), F32)

    x = x_ref[...].astype(F32)
    xbuf[POOL_HALO:POOL_HALO + ts, :] = x
    pos = s * ts + lax.broadcasted_iota(jnp.int32, (ts, 1), 0)
    for gi, win in enumerate(POOL_WINDOWS):
        lo, hi = gi * gw, (gi + 1) * gw
        xi = x[:, lo:hi]
        acc = xi
        for k in range(1, win):
            acc = acc + xbuf[POOL_HALO - k:POOL_HALO - k + ts, lo:hi]
        count = jnp.minimum(pos + 1, win).astype(F32)
        pooled = acc / count - xi
        y = _dot(pooled.astype(BF16), pw_ref[gi])
        o_ref[:, lo:hi] = (y * sc_ref[:, lo:hi] * _silu(g_ref[:, lo:hi].astype(F32))).astype(o_ref.dtype)
    xbuf[0:POOL_HALO, :] = xbuf[ts:ts + POOL_HALO, :]


def _pool_branch(u3, pool_w, pool_scale, w, col0):
    b, s, _ = u3.shape
    ts = _tile(s, 512)
    c0 = col0 // w
    ng, gw, _ = pool_w.shape
    return pl.pallas_call(
        _pool_kernel,
        grid=(b, s // ts),
        in_specs=[
            pl.BlockSpec((None, ts, w), lambda i, t: (i, t, c0)),
            pl.BlockSpec((None, ts, w), lambda i, t: (i, t, c0 + 1)),
            pl.BlockSpec((ng, gw, gw), lambda i, t: (0, 0, 0)),
            pl.BlockSpec((1, w), lambda i, t: (0, 0)),
        ],
        out_specs=pl.BlockSpec((None, ts, w), lambda i, t: (i, t, 0)),
        out_shape=jax.ShapeDtypeStruct((b, s, w), BF16),
        scratch_shapes=[pltpu.VMEM((POOL_HALO + ts, w), F32)],
        compiler_params=_params(("parallel", "arbitrary")),
        name="pool_branch",
    )(u3, u3, pool_w, pool_scale)


def _rwkv_prep_kernel(r_ref, k_ref, v_ref, lo_ref, mu_ref, mul_ref, w0_ref, w2_ref, a0_ref, a2_ref,
                      kkw_ref, kaw_ref, hsum_ref, hbc_ref,
                      ro_ref, ko_ref, vo_ref, na_ref, bv_ref, ld_ref, buf, lbuf):
    ts, w = ro_ref.shape

    @pl.when(pl.program_id(1) == 0)
    def _():
        buf[0:HALO, :] = jnp.zeros((HALO, buf.shape[1]), F32)
        lbuf[0:HALO, :] = jnp.zeros((HALO, lbuf.shape[1]), F32)

    for idx, ref in enumerate((r_ref, k_ref, v_ref)):
        buf[HALO:HALO + ts, idx * w:(idx + 1) * w] = ref[...].astype(F32)
    lbuf[HALO:HALO + ts, :] = lo_ref[...]

    def mixed(cur, prev, mu):
        return cur + (prev - cur) * mu

    mu = mu_ref[...]
    r = mixed(buf[HALO:HALO + ts, 0:w], buf[HALO - 1:HALO - 1 + ts, 0:w], mu[0:1])
    k = mixed(buf[HALO:HALO + ts, w:2 * w], buf[HALO - 1:HALO - 1 + ts, w:2 * w], mu[1:2])
    v = mixed(buf[HALO:HALO + ts, 2 * w:3 * w], buf[HALO - 1:HALO - 1 + ts, 2 * w:3 * w], mu[2:3])
    lo = mixed(lbuf[HALO:HALO + ts, :], lbuf[HALO - 1:HALO - 1 + ts, :], mul_ref[...])
    wl, al = lo[:, 0:LORA], lo[:, LORA:2 * LORA]

    z = w0_ref[...] + _dot(jnp.tanh(wl).astype(BF16), w2_ref[...])
    ld_ref[...] = -float(np.exp(-0.5)) * _sigmoid(z)
    a = _sigmoid(a0_ref[...] + _dot(al.astype(BF16), a2_ref[...]))

    kk = k * kkw_ref[...]
    n2 = _dot_f32(_dot_f32(kk * kk, hsum_ref[...]), hbc_ref[...])
    kk = kk * lax.rsqrt(jnp.maximum(n2, 1e-24))
    ro_ref[...] = r.astype(ro_ref.dtype)
    ko_ref[...] = (k * (1.0 + (a - 1.0) * kaw_ref[...])).astype(ko_ref.dtype)
    vo_ref[...] = v.astype(vo_ref.dtype)
    na_ref[...] = (-kk).astype(na_ref.dtype)
    bv_ref[...] = (kk * a).astype(bv_ref.dtype)

    buf[0:HALO, :] = buf[ts:ts + HALO, :]
    lbuf[0:HALO, :] = lbuf[ts:ts + HALO, :]


def _rwkv_prep(u3, lora, mu3, mul, w0, w2, a0, a2, kkw, kaw, w, col0):
    b, s, _ = u3.shape
    ts = _tile(s, 256)
    c0 = col0 // w
    nh = w // HEAD
    head_of = np.arange(w) // HEAD
    hsum = jnp.asarray((head_of[:, None] == np.arange(nh)[None, :]).astype(np.float32))
    hbc = hsum.T
    row = lambda k: pl.BlockSpec((None, ts, w), lambda i, t, k=k: (i, t, c0 + k))
    full = lambda shape: pl.BlockSpec(shape, lambda i, t: (0,) * len(shape))
    outw = pl.BlockSpec((None, ts, w), lambda i, t: (i, t, 0))
    sds = lambda dt: jax.ShapeDtypeStruct((b, s, w), dt)
    return pl.pallas_call(
        _rwkv_prep_kernel,
        grid=(b, s // ts),
        in_specs=[row(0), row(1), row(2),
                  pl.BlockSpec((None, ts, 2 * LORA), lambda i, t: (i, t, 0)),
                  full((3, w)), full((1, 2 * LORA)), full((1, w)), full((LORA, w)),
                  full((1, w)), full((LORA, w)), full((1, w)), full((1, w)),
                  full((w, nh)), full((nh, w))],
        out_specs=[outw] * 6,
        out_shape=[sds(BF16)] * 5 + [sds(F32)],
        scratch_shapes=[pltpu.VMEM((HALO + ts, 3 * w), F32), pltpu.VMEM((HALO + ts, 2 * LORA), F32)],
        compiler_params=_params(("parallel", "arbitrary")),
        name="rwkv_prep",
    )(u3, u3, u3, lora, mu3, mul, w0, w2, a0, a2, kkw, kaw, hsum, hbc)


def _head_stack(x, lane_head):
    zero = jnp.zeros_like(x)
    return jnp.concatenate([jnp.where(lane_head == 0, x, zero), jnp.where(lane_head == 1, x, zero)], axis=0)


def _rwkv_scan_kernel(r_ref, k_ref, v_ref, na_ref, bv_ref, ld_ref, g_ref, lng_ref, lnb_ref, rk_ref,
                      o_ref, state, ybuf):
    ts = o_ref.shape[0]
    c = RWKV_CHUNK
    n_chunks = ts // c

    @pl.when(pl.program_id(2) == 0)
    def _():
        state[...] = jnp.zeros_like(state)

    lane_head = lax.broadcasted_iota(jnp.int32, (1, LANES), 1) // HEAD
    t_idx = lax.broadcasted_iota(jnp.int32, (c, LANES), 0)
    s_idx = lax.broadcasted_iota(jnp.int32, (c, LANES), 1) % c
    strict = s_idx < t_idx
    incl = s_idx <= t_idx
    tril = (lax.broadcasted_iota(jnp.int32, (c, c), 1)
            <= lax.broadcasted_iota(jnp.int32, (c, c), 0)).astype(F32)
    blk = (lax.broadcasted_iota(jnp.int32, (LANES, LANES), 0) // HEAD
           == lax.broadcasted_iota(jnp.int32, (LANES, LANES), 1) // HEAD)
    eye2 = (lax.broadcasted_iota(jnp.int32, (LANES, LANES), 0)
            == lax.broadcasted_iota(jnp.int32, (LANES, LANES), 1)).astype(F32)

    def chunk(ci, carry):
        rows = pl.ds(pl.multiple_of(ci * c, c), c)
        r = r_ref[rows, :].astype(F32)
        k = k_ref[rows, :].astype(F32)
        v = v_ref[rows, :]
        na = na_ref[rows, :].astype(F32)
        bv = bv_ref[rows, :].astype(F32)
        ld = ld_ref[rows, :]
        lc = _dot_f32(tril, ld)
        lend = lc[c - 1:c, :]
        e_in = jnp.exp(lc)
        e_ex = jnp.exp(lc - ld)
        e_neg = jnp.exp(-lc)
        e_end = jnp.exp(lend - lc)
        rh = (r * e_in).astype(BF16)
        ah = (na * e_ex).astype(BF16)
        bt = (bv * e_neg).astype(BF16)
        kt = (k * e_neg).astype(BF16)
        bb = (bv * e_end).astype(BF16)
        kb = (k * e_end).astype(BF16)

        lhs = jnp.concatenate([ah, rh], axis=0)
        rhs = jnp.concatenate([_head_stack(bt, lane_head), _head_stack(kt, lane_head)], axis=0)
        sc = _dot_nt(lhs, rhs)
        a_ab = jnp.where(strict, sc[0:c, 0:LANES], 0.0)
        a_ak = jnp.where(strict, sc[0:c, LANES:2 * LANES], 0.0)
        a_rb = jnp.where(incl, sc[c:2 * c, 0:LANES], 0.0)
        a_rk = jnp.where(incl, sc[c:2 * c, LANES:2 * LANES], 0.0)

        x = _head_stack(a_ab, lane_head)
        tinv = eye2 + x
        p2 = 2
        while p2 < c:
            xb = x.astype(BF16)
            x = _dot(xb, xb)
            tinv = tinv + _dot(tinv.astype(BF16), x.astype(BF16))
            p2 *= 2
        t_cat = (tinv[0:c, :] + tinv[c:2 * c, :]).astype(BF16)

        g2 = state[...]
        g2b = g2.astype(BF16)
        vs = _head_stack(v, lane_head)
        p = _dot_nt(ah, g2b) + _dot(a_ak.astype(BF16), vs)
        u = _dot(t_cat, _head_stack(p.astype(BF16), lane_head))
        ub = u.astype(BF16)
        y = _dot_nt(rh, g2b) + _dot(a_rb.astype(BF16), _head_stack(ub, lane_head)) + _dot(a_rk.astype(BF16), vs)
        ybuf[rows, :] = y
        uv_t = jnp.concatenate([ub, v], axis=0).T
        upd = _dot(uv_t, jnp.concatenate([bb, kb], axis=0))
        state[...] = g2 * jnp.exp(lend) + jnp.where(blk, upd, 0.0)
        return carry

    lax.fori_loop(0, n_chunks, chunk, 0)

    y = ybuf[...]
    h0 = lane_head == 0

    def head_sum(t):
        s0 = jnp.sum(jnp.where(h0, t, 0.0), axis=-1, keepdims=True)
        s1 = jnp.sum(jnp.where(h0, 0.0, t), axis=-1, keepdims=True)
        return jnp.where(h0, s0, s1)

    mean = head_sum(y) * (1.0 / HEAD)
    d = y - mean
    var = head_sum(d * d) * (1.0 / HEAD)
    yn = d * lax.rsqrt(var + RWKV_LN_EPS) * lng_ref[...] + lnb_ref[...]
    r = r_ref[...].astype(F32)
    k = k_ref[...].astype(F32)
    bonus = head_sum(r * k * rk_ref[...]) * v_ref[...].astype(F32)
    o_ref[...] = ((yn + bonus) * _silu(g_ref[...].astype(F32))).astype(o_ref.dtype)


def _rwkv_scan(prep, u3, ln_g, ln_b, r_k, w, gate_col0):
    r, k, v, na, bv, ld = prep
    b, s, _ = r.shape
    ts = _tile(s, 512)
    n_pairs = w // LANES
    gc0 = gate_col0 // LANES
    row = pl.BlockSpec((None, ts, LANES), lambda i, p, t: (i, t, p))
    vec = pl.BlockSpec((1, LANES), lambda i, p, t: (0, p))
    return pl.pallas_call(
        _rwkv_scan_kernel,
        grid=(b, n_pairs, s // ts),
        in_specs=[row] * 6 + [pl.BlockSpec((None, ts, LANES), lambda i, p, t: (i, t, gc0 + p)),
                              vec, vec, vec],
        out_specs=row,
        out_shape=jax.ShapeDtypeStruct((b, s, w), BF16),
        scratch_shapes=[pltpu.VMEM((LANES, LANES), F32), pltpu.VMEM((ts, LANES), F32)],
        compiler_params=_params(("parallel", "parallel", "arbitrary")),
        name="rwkv_scan",
    )(r, k, v, na, bv, ld, u3, ln_g, ln_b, r_k)


def _fox_kernel(qi_ref, kj_ref, q_ref, k_ref, v_ref, c_ref, g_ref, o_ref, m_ref, l_ref, acc_ref):
    step = pl.program_id(2)
    qi, kj = qi_ref[step], kj_ref[step]
    tq, tk = q_ref.shape[0], k_ref.shape[0]
    lane_head = lax.broadcasted_iota(jnp.int32, (1, LANES), 1) // HEAD

    @pl.when(kj == 0)
    def _():
        m_ref[...] = jnp.full_like(m_ref, NEG_BIG)
        l_ref[...] = jnp.zeros_like(l_ref)
        acc_ref[...] = jnp.zeros_like(acc_ref)

    q2 = q_ref[...] * jnp.asarray(HEAD ** -0.5, q_ref.dtype)
    k2 = k_ref[...]
    v2 = v_ref[...]
    causal = (qi * tq + lax.broadcasted_iota(jnp.int32, (tq, tk), 0)
              >= kj * tk + lax.broadcasted_iota(jnp.int32, (tq, tk), 1))
    alphas, pvs = [], []
    for h in range(2):
        qh = jnp.where(lane_head == h, q2, jnp.zeros_like(q2))
        vh = jnp.where(lane_head == h, v2, jnp.zeros_like(v2))
        s = _dot_nt(qh, k2) - c_ref[h:h + 1, :]
        s = jnp.where(causal, s, NEG_BIG)
        m_prev = m_ref[h]
        m_new = jnp.maximum(m_prev, jnp.max(s, axis=-1, keepdims=True))
        alpha = jnp.exp(m_prev - m_new)
        p = jnp.exp(s - m_new)
        l_ref[h] = alpha * l_ref[h] + jnp.sum(p, axis=-1, keepdims=True)
        m_ref[h] = m_new
        alphas.append(alpha)
        pvs.append(_dot(p.astype(BF16), vh))
    alpha2 = jnp.where(lane_head == 0, alphas[0], alphas[1])
    acc_ref[...] = acc_ref[...] * alpha2 + pvs[0] + pvs[1]

    @pl.when(kj == qi)
    def _():
        l2 = jnp.where(lane_head == 0, l_ref[0], l_ref[1])
        o_ref[...] = (acc_ref[...] / l2 * _silu(g_ref[...].astype(F32))).astype(o_ref.dtype)


def _fox_branch(u3, ct, w, q_col0, gate_col0):
    b, s, _ = u3.shape
    tq = _tile(s, 512)
    nq = s // tq
    n_pairs = w // LANES
    pairs = [(i, j) for i in range(nq) for j in range(i + 1)]
    qi = jnp.asarray([p[0] for p in pairs], jnp.int32)
    kj = jnp.asarray([p[1] for p in pairs], jnp.int32)
    qc0, gc0 = q_col0 // LANES, gate_col0 // LANES
    ct4 = ct.reshape(b, n_pairs, 2, s)
    grid_spec = pltpu.PrefetchScalarGridSpec(
        num_scalar_prefetch=2,
        grid=(b, n_pairs, len(pairs)),
        in_specs=[
            pl.BlockSpec((None, tq, LANES), lambda i, p, t, qi, kj: (i, qi[t], qc0 + p)),
            pl.BlockSpec((None, tq, LANES), lambda i, p, t, qi, kj: (i, kj[t], qc0 + n_pairs + p)),
            pl.BlockSpec((None, tq, LANES), lambda i, p, t, qi, kj: (i, kj[t], qc0 + 2 * n_pairs + p)),
            pl.BlockSpec((None, None, 2, tq), lambda i, p, t, qi, kj: (i, p, 0, kj[t])),
            pl.BlockSpec((None, tq, LANES), lambda i, p, t, qi, kj: (i, qi[t], gc0 + p)),
        ],
        out_specs=pl.BlockSpec((None, tq, LANES), lambda i, p, t, qi, kj: (i, qi[t], p)),
        scratch_shapes=[pltpu.VMEM((2, tq, 1), F32), pltpu.VMEM((2, tq, 1), F32),
                        pltpu.VMEM((tq, LANES), F32)],
    )
    return pl.pallas_call(
        _fox_kernel,
        grid_spec=grid_spec,
        out_shape=jax.ShapeDtypeStruct((b, s, w), BF16),
        compiler_params=_params(("parallel", "parallel", "arbitrary")),
        name="fox_attention",
    )(qi, kj, u3, u3, u3, ct4, u3)


def _merge_kernel(ya_ref, yb_ref, yc_ref, yd_ref, ma_ref, mb_ref, mc_ref, md_ref, bm_ref, wb_ref, o_ref):
    acc = None
    branches = ((ya_ref, ma_ref), (yb_ref, mb_ref), (yc_ref, mc_ref), (yd_ref, md_ref))
    for kbr, (y_ref, ml_ref) in enumerate(branches):
        proj = _dot(y_ref[...], wb_ref[kbr])
        gate = _sigmoid(ml_ref[...].astype(F32) + bm_ref[kbr:kbr + 1, :])
        acc = gate * proj if acc is None else acc + gate * proj
    o_ref[...] = acc.astype(o_ref.dtype)


def _merge(ys, u2, b_merge, w_branch, ml_col0):
    t, w = ys[0].shape
    nb, _, d = w_branch.shape
    tm, tn = _tile(t, 1024), _tile(d, 512)
    nd = d // tn
    mc0 = ml_col0 // tn
    yspec = pl.BlockSpec((tm, w), lambda i, j: (i, 0))
    mspec = lambda k: pl.BlockSpec((tm, tn), lambda i, j, k=k: (i, mc0 + k * nd + j))
    return pl.pallas_call(
        _merge_kernel,
        grid=(t // tm, nd),
        in_specs=[yspec] * 4 + [mspec(k) for k in range(4)] + [
            pl.BlockSpec((nb, tn), lambda i, j: (0, j)),
            pl.BlockSpec((nb, w, tn), lambda i, j: (0, 0, j)),
        ],
        out_specs=pl.BlockSpec((tm, tn), lambda i, j: (i, j)),
        out_shape=jax.ShapeDtypeStruct((t, d), BF16),
        compiler_params=_params(("parallel", "arbitrary")),
        name="merge",
    )(*ys, u2, u2, u2, u2, b_merge, w_branch)


def _outproj_kernel(m_ref, w_ref, x_ref, o_ref):
    o_ref[...] = x_ref[...] + _dot(m_ref[...], w_ref[...])


def _outproj(merged, w_out, x2):
    t, d = x2.shape
    tm, tn = _tile(t, 1024), _tile(d, 512)
    return pl.pallas_call(
        _outproj_kernel,
        grid=(t // tm, d // tn),
        in_specs=[
            pl.BlockSpec((tm, d), lambda i, j: (i, 0)),
            pl.BlockSpec((d, tn), lambda i, j: (0, j)),
            pl.BlockSpec((tm, tn), lambda i, j: (i, j)),
        ],
        out_specs=pl.BlockSpec((tm, tn), lambda i, j: (i, j)),
        out_shape=jax.ShapeDtypeStruct((t, d), F32),
        compiler_params=_params(("parallel", "arbitrary")),
        name="outproj",
    )(merged, w_out, x2)


def _final_norm_kernel(x_ref, g_ref, o_ref):
    o_ref[...] = _rms_rows(x_ref[...], g_ref[...])


def _final_norm(x2, g):
    t, d = x2.shape
    tm = _tile(t, 512)
    return pl.pallas_call(
        _final_norm_kernel,
        grid=(t // tm,),
        in_specs=[pl.BlockSpec((tm, d), lambda i: (i, 0)), pl.BlockSpec((1, d), lambda i: (0, 0))],
        out_specs=pl.BlockSpec((tm, d), lambda i: (i, 0)),
        out_shape=jax.ShapeDtypeStruct((t, d), F32),
        compiler_params=_params(("parallel",)),
        name="final_norm",
    )(x2, g)


def _layer(x2, bsz, seq, norm_g, w_in, b_merge, conv_w, rwkv_mu, rwkv_w0, rwkv_w2, rwkv_a0, rwkv_a2,
           rwkv_kk, rwkv_ka, rwkv_rk, rwkv_ln_g, rwkv_ln_b, fox_bf, pool_w, pool_scale, w_branch, w_out):
    t, d = x2.shape
    w = conv_w.shape[1]
    nh = fox_bf.shape[0]
    lo0, f0 = 7 * w, 11 * w + 2 * LORA
    w_main = jnp.concatenate([w_in[:, :lo0], w_in[:, lo0 + 2 * LORA:f0], w_in[:, f0 + nh:]], axis=1).astype(BF16)
    w_small = w_in[:, lo0:lo0 + 2 * LORA].astype(BF16)
    w_ft = w_in[:, f0:f0 + nh].T.astype(BF16)
    row = lambda a: a.reshape(1, -1)

    u2 = _inproj(x2, row(norm_g), w_main)
    u3 = u2.reshape(bsz, seq, -1)
    lora, ct = _side(x2.reshape(bsz, seq, d), row(norm_g), w_small, w_ft, fox_bf.reshape(nh, 1))
    y_a = _conv_branch(u3, conv_w, w, 0)
    prep = _rwkv_prep(u3, lora, rwkv_mu[:3 * w].reshape(3, w), row(rwkv_mu[3 * w:]), row(rwkv_w0),
                      rwkv_w2.astype(BF16), row(rwkv_a0), rwkv_a2.astype(BF16), row(rwkv_kk), row(rwkv_ka),
                      w, 4 * w)
    y_b = _rwkv_scan(prep, u3, row(rwkv_ln_g), row(rwkv_ln_b), row(rwkv_rk), w, 7 * w)
    y_c = _fox_branch(u3, ct, w, 8 * w, 11 * w)
    y_d = _pool_branch(u3, pool_w.astype(BF16), row(pool_scale), w, 12 * w)
    ys = [y.reshape(t, w) for y in (y_a, y_b, y_c, y_d)]
    merged = _merge(ys, u2, b_merge, w_branch.astype(BF16), 14 * w)
    return _outproj(merged, w_out.astype(BF16), x2)


def kernel(x, norm_g, w_in, b_merge, conv_w, rwkv_mu, rwkv_w0, rwkv_w2, rwkv_a0, rwkv_a2, rwkv_kk, rwkv_ka,
           rwkv_rk, rwkv_ln_g, rwkv_ln_b, fox_bf, pool_w, pool_scale, w_branch, w_out, final_g):
    bsz, seq, d = x.shape
    x2 = x.reshape(bsz * seq, d)
    for l in range(norm_g.shape[0]):
        x2 = _layer(x2, bsz, seq, norm_g[l], w_in[l], b_merge[l], conv_w[l], rwkv_mu[l], rwkv_w0[l],
                    rwkv_w2[l], rwkv_a0[l], rwkv_a2[l], rwkv_kk[l], rwkv_ka[l], rwkv_rk[l], rwkv_ln_g[l],
                    rwkv_ln_b[l], fox_bf[l], pool_w[l], pool_scale[l], w_branch[l], w_out[l])
    return _final_norm(x2, final_g.reshape(1, d)).reshape(bsz, seq, d)
```

```python
import functools

import jax
import jax.numpy as jnp
import numpy as np
from jax import lax
from jax.experimental import pallas as pl
from jax.experimental.pallas import tpu as pltpu

F32 = jnp.float32
BF16 = jnp.bfloat16
HIGHEST = lax.Precision.HIGHEST

NORM_EPS = 1e-6
RWKV_LN_EPS = 64e-5
HEAD = 64
LORA = 64
POOL_WINDOWS = (2, 4, 8, 16)
RWKV_CHUNK = 64
LANES = 128
MXU_DIM = 256
LOG2E = float(np.log2(np.e))
NEG_BIG = -1e30
VMEM_LIMIT = 48 * 1024 * 1024


def _params(sem):
    return pltpu.CompilerParams(dimension_semantics=sem, vmem_limit_bytes=VMEM_LIMIT)


def _tile(n, want):
    t = min(n, want)
    while n % t:
        t //= 2
    return t


def _sigmoid(x):
    return 1.0 / (1.0 + jnp.exp(-x))


def _silu(x):
    return x * _sigmoid(x)


def _rms_rows(xf, g):
    ms = jnp.mean(xf * xf, axis=-1, keepdims=True)
    return xf * lax.rsqrt(ms + NORM_EPS) * g


def _dot(a, b):
    return jnp.dot(a, b, preferred_element_type=F32)


def _dot_nt(a, b):
    return lax.dot_general(a, b, (((1,), (1,)), ((), ())), preferred_element_type=F32)


def _dot_f32(a, b):
    return jnp.dot(a, b, precision=HIGHEST, preferred_element_type=F32)


def _inproj_kernel(x_ref, g_ref, w_ref, o_ref, h_ref):
    @pl.when(pl.program_id(1) == 0)
    def _():
        h_ref[...] = _rms_rows(x_ref[...], g_ref[...]).astype(BF16)

    o_ref[...] = _dot(h_ref[...], w_ref[...]).astype(o_ref.dtype)


def _inproj(x2, g, w):
    t, d = x2.shape
    n = w.shape[1]
    tm, tn = _tile(t, 1024), _tile(n, 512)
    return pl.pallas_call(
        _inproj_kernel,
        grid=(t // tm, n // tn),
        in_specs=[
            pl.BlockSpec((tm, d), lambda i, j: (i, 0)),
            pl.BlockSpec((1, d), lambda i, j: (0, 0)),
            pl.BlockSpec((d, tn), lambda i, j: (0, j)),
        ],
        out_specs=pl.BlockSpec((tm, tn), lambda i, j: (i, j)),
        out_shape=jax.ShapeDtypeStruct((t, n), BF16),
        scratch_shapes=[pltpu.VMEM((tm, d), BF16)],
        compiler_params=_params(("parallel", "arbitrary")),
        name="inproj",
    )(x2, g, w)


C_PIECES = 3


def _bf16_pieces(x, n=C_PIECES):
    pieces = []
    for _ in range(n):
        p = x.astype(BF16)
        pieces.append(p)
        x = x - p.astype(F32)
    return pieces


def _side_kernel(x_ref, g_ref, ws_ref, bf_ref, sel_ref, lora_ref, kc_ref, carry_ref):
    @pl.when(pl.program_id(1) == 0)
    def _():
        carry_ref[...] = jnp.zeros_like(carry_ref)

    h = _rms_rows(x_ref[...], g_ref[...]).astype(BF16)
    sf = _dot(h, ws_ref[...])
    ns = lora_ref.shape[1]
    lora_ref[...] = sf[:, :ns]
    z = sf[:, ns:] + bf_ref[...]
    logf = jnp.minimum(z, 0.0) - jnp.log1p(jnp.exp(-jnp.abs(z)))
    ts = logf.shape[0]
    lower = (lax.broadcasted_iota(jnp.int32, (ts, ts), 0)
             >= lax.broadcasted_iota(jnp.int32, (ts, ts), 1)).astype(F32)
    c = _dot_f32(lower, logf) + carry_ref[...]
    carry_ref[...] = c[ts - 1:ts, :]
    kc = None
    for piece, sel in zip(_bf16_pieces(-LOG2E * c), (sel_ref[0], sel_ref[1], sel_ref[2])):
        term = _dot(piece, sel)
        kc = term if kc is None else kc + term
    kc_ref[...] = kc.astype(kc_ref.dtype)


def _forget_lane(head, piece):
    return C_PIECES * head + piece


def _side(x3, g, w_small, b_f, w):
    b, s, d = x3.shape
    ns = w_small.shape[1] - LANES
    ts = _tile(s, 512)
    nh = w // HEAD
    sel = np.zeros((C_PIECES, LANES, w), np.float32)
    for hd in range(nh):
        for piece in range(C_PIECES):
            sel[piece, hd, LANES * (hd // 2) + _forget_lane(hd % 2, piece)] = 1.0
    return pl.pallas_call(
        _side_kernel,
        grid=(b, s // ts),
        in_specs=[
            pl.BlockSpec((None, ts, d), lambda i, j: (i, j, 0)),
            pl.BlockSpec((1, d), lambda i, j: (0, 0)),
            pl.BlockSpec((d, ns + LANES), lambda i, j: (0, 0)),
            pl.BlockSpec((1, LANES), lambda i, j: (0, 0)),
            pl.BlockSpec((C_PIECES, LANES, w), lambda i, j: (0, 0, 0)),
        ],
        out_specs=[
            pl.BlockSpec((None, ts, ns), lambda i, j: (i, j, 0)),
            pl.BlockSpec((None, ts, w), lambda i, j: (i, j, 0)),
        ],
        out_shape=[
            jax.ShapeDtypeStruct((b, s, ns), F32),
            jax.ShapeDtypeStruct((b, s, w), BF16),
        ],
        scratch_shapes=[pltpu.VMEM((1, LANES), F32)],
        compiler_params=_params(("parallel", "arbitrary")),
        name="side",
    )(x3, g, w_small, b_f, jnp.asarray(sel, BF16))


HALO = 8


def _conv_kernel(bg_ref, cg_ref, xv_ref, g_ref, cw_ref, o_ref, pbuf):
    ts = o_ref.shape[0]

    @pl.when(pl.program_id(2) == 0)
    def _():
        pbuf[0:HALO, :] = jnp.zeros((HALO, pbuf.shape[1]), F32)

    p = cg_ref[...].astype(F32) * xv_ref[...].astype(F32)
    pbuf[HALO:HALO + ts, :] = p
    cw = cw_ref[...]
    z = (cw[0:1] * pbuf[HALO - 2:HALO - 2 + ts, :] + cw[1:2] * pbuf[HALO - 1:HALO - 1 + ts, :]
         + cw[2:3] * p)
    o_ref[...] = (bg_ref[...].astype(F32) * z * _silu(g_ref[...].astype(F32))).astype(o_ref.dtype)
    pbuf[0:HALO, :] = pbuf[ts:ts + HALO, :]


def _conv_branch(u3, conv_w, w, col0):
    b, s, _ = u3.shape
    ts, tw = _tile(s, 512), _tile(w, 512)
    nw = w // tw
    c0 = col0 // tw

    def spec(k):
        return pl.BlockSpec((None, ts, tw), lambda i, j, t, k=k: (i, t, c0 + k * nw + j))

    return pl.pallas_call(
        _conv_kernel,
        grid=(b, nw, s // ts),
        in_specs=[spec(0), spec(1), spec(2), spec(3),
                  pl.BlockSpec((conv_w.shape[0], tw), lambda i, j, t: (0, j))],
        out_specs=pl.BlockSpec((None, ts, tw), lambda i, j, t: (i, t, j)),
        out_shape=jax.ShapeDtypeStruct((b, s, w), BF16),
        scratch_shapes=[pltpu.VMEM((HALO + ts, tw), F32)],
        compiler_params=_params(("parallel", "parallel", "arbitrary")),
        name="conv_branch",
    )(u3, u3, u3, u3, conv_w)


POOL_HALO = 16


def _pool_kernel(x_ref, g_ref, pw_ref, sc_ref, o_ref, xbuf):
    ts, w = o_ref.shape
    gw = w // len(POOL_WINDOWS)
    s = pl.program_id(1)

    @pl.when(s == 0)
    def _():
        xbuf[0:POOL_HALO, :] = jnp.zeros((POOL_HALO, w), F32)

    x = x_ref[...].astype(F32)
    xbuf[POOL_HALO:POOL_HALO + ts, :] = x
    pos = s * ts + lax.broadcasted_iota(jnp.int32, (ts, 1), 0)
    for gi, win in enumerate(POOL_WINDOWS):
        lo, hi = gi * gw, (gi + 1) * gw
        xi = x[:, lo:hi]
        acc = xi
        for k in range(1, win):
            acc = acc + xbuf[POOL_HALO - k:POOL_HALO - k + ts, lo:hi]
        count = jnp.minimum(pos + 1, win).astype(F32)
        pooled = acc / count - xi
        y = _dot(pooled.astype(BF16), pw_ref[gi])
        o_ref[:, lo:hi] = (y * sc_ref[:, lo:hi] * _silu(g_ref[:, lo:hi].astype(F32))).astype(o_ref.dtype)
    xbuf[0:POOL_HALO, :] = xbuf[ts:ts + POOL_HALO, :]


def _pool_branch(u3, pool_w, pool_scale, w, col0):
    b, s, _ = u3.shape
    ts = _tile(s, 512)
    c0 = col0 // w
    ng, gw, _ = pool_w.shape
    return pl.pallas_call(
        _pool_kernel,
        grid=(b, s // ts),
        in_specs=[
            pl.BlockSpec((None, ts, w), lambda i, t: (i, t, c0)),
            pl.BlockSpec((None, ts, w), lambda i, t: (i, t, c0 + 1)),
            pl.BlockSpec((ng, gw, gw), lambda i, t: (0, 0, 0)),
            pl.BlockSpec((1, w), lambda i, t: (0, 0)),
        ],
        out_specs=pl.BlockSpec((None, ts, w), lambda i, t: (i, t, 0)),
        out_shape=jax.ShapeDtypeStruct((b, s, w), BF16),
        scratch_shapes=[pltpu.VMEM((POOL_HALO + ts, w), F32)],
        compiler_params=_params(("parallel", "arbitrary")),
        name="pool_branch",
    )(u3, u3, pool_w, pool_scale)


def _rwkv_prep_kernel(r_ref, k_ref, v_ref, lo_ref, mu_ref, mul_ref, w0_ref, w2_ref, a0_ref, a2_ref,
                      kkw_ref, kaw_ref, hsum_ref, hbc_ref,
                      ro_ref, ko_ref, vo_ref, na_ref, bv_ref, ld_ref, buf, lbuf):
    ts, w = ro_ref.shape

    @pl.when(pl.program_id(1) == 0)
    def _():
        buf[0:HALO, :] = jnp.zeros((HALO, buf.shape[1]), F32)
        lbuf[0:HALO, :] = jnp.zeros((HALO, lbuf.shape[1]), F32)

    for idx, ref in enumerate((r_ref, k_ref, v_ref)):
        buf[HALO:HALO + ts, idx * w:(idx + 1) * w] = ref[...].astype(F32)
    lbuf[HALO:HALO + ts, :] = lo_ref[...]

    def mixed(cur, prev, mu):
        return cur + (prev - cur) * mu

    mu = mu_ref[...]
    r = mixed(buf[HALO:HALO + ts, 0:w], buf[HALO - 1:HALO - 1 + ts, 0:w], mu[0:1])
    k = mixed(buf[HALO:HALO + ts, w:2 * w], buf[HALO - 1:HALO - 1 + ts, w:2 * w], mu[1:2])
    v = mixed(buf[HALO:HALO + ts, 2 * w:3 * w], buf[HALO - 1:HALO - 1 + ts, 2 * w:3 * w], mu[2:3])
    lo = mixed(lbuf[HALO:HALO + ts, :], lbuf[HALO - 1:HALO - 1 + ts, :], mul_ref[...])
    wl, al = lo[:, 0:LORA], lo[:, LORA:2 * LORA]

    z = w0_ref[...] + _dot(jnp.tanh(wl).astype(BF16), w2_ref[...])
    ld_ref[...] = -float(np.exp(-0.5)) * _sigmoid(z)
    a = _sigmoid(a0_ref[...] + _dot(al.astype(BF16), a2_ref[...]))

    kk = k * kkw_ref[...]
    n2 = _dot_f32(_dot_f32(kk * kk, hsum_ref[...]), hbc_ref[...])
    kk = kk * lax.rsqrt(jnp.maximum(n2, 1e-24))
    ro_ref[...] = r.astype(ro_ref.dtype)
    ko_ref[...] = (k * (1.0 + (a - 1.0) * kaw_ref[...])).astype(ko_ref.dtype)
    vo_ref[...] = v.astype(vo_ref.dtype)
    na_ref[...] = (-kk).astype(na_ref.dtype)
    bv_ref[...] = (kk * a).astype(bv_ref.dtype)

    buf[0:HALO, :] = buf[ts:ts + HALO, :]
    lbuf[0:HALO, :] = lbuf[ts:ts + HALO, :]


def _rwkv_prep(u3, lora, mu3, mul, w0, w2, a0, a2, kkw, kaw, w, col0):
    b, s, _ = u3.shape
    ts = _tile(s, 256)
    c0 = col0 // w
    nh = w // HEAD
    head_of = np.arange(w) // HEAD
    hsum = jnp.asarray((head_of[:, None] == np.arange(nh)[None, :]).astype(np.float32))
    hbc = hsum.T
    row = lambda k: pl.BlockSpec((None, ts, w), lambda i, t, k=k: (i, t, c0 + k))
    full = lambda shape: pl.BlockSpec(shape, lambda i, t: (0,) * len(shape))
    outw = pl.BlockSpec((None, ts, w), lambda i, t: (i, t, 0))
    sds = lambda dt: jax.ShapeDtypeStruct((b, s, w), dt)
    return pl.pallas_call(
        _rwkv_prep_kernel,
        grid=(b, s // ts),
        in_specs=[row(0), row(1), row(2),
                  pl.BlockSpec((None, ts, 2 * LORA), lambda i, t: (i, t, 0)),
                  full((3, w)), full((1, 2 * LORA)), full((1, w)), full((LORA, w)),
                  full((1, w)), full((LORA, w)), full((1, w)), full((1, w)),
                  full((w, nh)), full((nh, w))],
        out_specs=[outw] * 6,
        out_shape=[sds(BF16)] * 5 + [sds(F32)],
        scratch_shapes=[pltpu.VMEM((HALO + ts, 3 * w), F32), pltpu.VMEM((HALO + ts, 2 * LORA), F32)],
        compiler_params=_params(("parallel", "arbitrary")),
        name="rwkv_prep",
    )(u3, u3, u3, lora, mu3, mul, w0, w2, a0, a2, kkw, kaw, hsum, hbc)


def _head_stack(x, lane_head):
    zero = jnp.zeros_like(x)
    return jnp.concatenate([jnp.where(lane_head == 0, x, zero), jnp.where(lane_head == 1, x, zero)], axis=0)


SUM_PIECES = 2


def _rwkv_scan_kernel(r_ref, k_ref, v_ref, na_ref, bv_ref, ld_ref, g_ref, lng_ref, lnb_ref, rk_ref,
                      cl_ref, cu_ref, hb_ref, o_ref, state, ybuf, dec_s, rh_s, ah_s, bt_s, kt_s, bb_s, kb_s):
    ts, width = o_ref.shape
    n_pairs = width // LANES
    c = RWKV_CHUNK
    n_chunks = ts // c

    @pl.when(pl.program_id(2) == 0)
    def _():
        state[...] = jnp.zeros_like(state)

    def chunk_sums(m_ref):
        sb = m_ref.shape[0]
        outs = []
        for r0 in range(0, ts, sb):
            pieces = jnp.concatenate(_bf16_pieces(ld_ref[r0:r0 + sb, :], SUM_PIECES), axis=1)
            both = _dot(m_ref[...], pieces)
            outs.append(both[:, 0:width] + both[:, width:2 * width])
        return jnp.concatenate(outs, axis=0)

    ld = ld_ref[...]
    lc = chunk_sums(cl_ref)
    lrest = chunk_sums(cu_ref)
    e_in = jnp.exp(lc)
    e_neg = jnp.exp(-lc)
    e_end = jnp.exp(lrest)
    r_all = r_ref[...].astype(F32)
    k_all = k_ref[...].astype(F32)
    na_all = na_ref[...].astype(F32)
    bv_all = bv_ref[...].astype(F32)
    dec_s[...] = e_in
    rh_s[...] = (r_all * e_in).astype(BF16)
    ah_s[...] = (na_all * jnp.exp(lc - ld)).astype(BF16)
    bt_s[...] = (bv_all * e_neg).astype(BF16)
    kt_s[...] = (k_all * e_neg).astype(BF16)
    bb_s[...] = (bv_all * e_end).astype(BF16)
    kb_s[...] = (k_all * e_end).astype(BF16)

    lane_head = lax.broadcasted_iota(jnp.int32, (1, LANES), 1) // HEAD
    t_idx = lax.broadcasted_iota(jnp.int32, (c, LANES), 0)
    s_idx = lax.broadcasted_iota(jnp.int32, (c, LANES), 1) % c
    strict = s_idx < t_idx
    incl = s_idx <= t_idx
    blk = (lax.broadcasted_iota(jnp.int32, (LANES, LANES), 0) // HEAD
           == lax.broadcasted_iota(jnp.int32, (LANES, LANES), 1) // HEAD)
    eye_cat = (s_idx == t_idx).astype(F32)

    def chunk(ci, states):
        rows = pl.ds(pl.multiple_of(ci * c, c), c)
        tail_rows = pl.ds(pl.multiple_of(ci * c + c - 8, 8), 8)
        pairs = range(n_pairs)
        lanes = [slice(pi * LANES, (pi + 1) * LANES) for pi in pairs]
        each = lambda f: [f(pi) for pi in pairs]
        v = each(lambda pi: v_ref[rows, lanes[pi]])
        rh = each(lambda pi: rh_s[rows, lanes[pi]])
        ah = each(lambda pi: ah_s[rows, lanes[pi]])
        sc = each(lambda pi: _dot_nt(
            jnp.concatenate([ah[pi], rh[pi]], axis=0),
            jnp.concatenate([_head_stack(bt_s[rows, lanes[pi]], lane_head),
                             _head_stack(kt_s[rows, lanes[pi]], lane_head)], axis=0)))
        a_ak = each(lambda pi: jnp.where(strict, sc[pi][0:c, LANES:2 * LANES], 0.0).astype(BF16))
        a_rb = each(lambda pi: jnp.where(incl, sc[pi][c:2 * c, 0:LANES], 0.0).astype(BF16))
        a_rk = each(lambda pi: jnp.where(incl, sc[pi][c:2 * c, LANES:2 * LANES], 0.0).astype(BF16))

        x = each(lambda pi: jnp.where(strict, sc[pi][0:c, 0:LANES], 0.0))
        tinv = each(lambda pi: eye_cat + x[pi])
        p2 = 2
        while p2 < c:
            xb = each(lambda pi: x[pi].astype(BF16))
            x = each(lambda pi: _dot(xb[pi], _head_stack(xb[pi], lane_head)))
            tinv = each(lambda pi: tinv[pi] + _dot(tinv[pi].astype(BF16),
                                                   _head_stack(x[pi].astype(BF16), lane_head)))
            p2 *= 2
        t_cat = each(lambda pi: tinv[pi].astype(BF16))

        g2b = each(lambda pi: states[pi].astype(BF16))
        vs = each(lambda pi: _head_stack(v[pi], lane_head))
        p = each(lambda pi: _dot_nt(ah[pi], g2b[pi]) + _dot(a_ak[pi], vs[pi]))
        ub = each(lambda pi: _dot(t_cat[pi], _head_stack(p[pi].astype(BF16), lane_head)).astype(BF16))
        new_states = []
        for pi in pairs:
            y = (_dot_nt(rh[pi], g2b[pi]) + _dot(a_rb[pi], _head_stack(ub[pi], lane_head))
                 + _dot(a_rk[pi], vs[pi]))
            ybuf[rows, lanes[pi]] = y
            uv_t = jnp.concatenate([ub[pi], v[pi]], axis=0).T
            upd = _dot(uv_t, jnp.concatenate([bb_s[rows, lanes[pi]], kb_s[rows, lanes[pi]]], axis=0))
            decay = dec_s[tail_rows, lanes[pi]][7:8, :]
            new_states.append(states[pi] * decay + jnp.where(blk, upd, 0.0))
        return tuple(new_states)

    final = lax.fori_loop(0, n_chunks, chunk, tuple(state[pi] for pi in range(n_pairs)))
    for pi in range(n_pairs):
        state[pi] = final[pi]

    hb = hb_ref[...]
    gw = hb.shape[0]
    for l0 in range(0, width, gw):
        lanes = slice(l0, l0 + gw)
        y = ybuf[:, lanes]
        mean = _head_sums(y, hb) * (1.0 / HEAD)
        d = y - mean
        var = _head_sums(d * d, hb) * (1.0 / HEAD)
        yn = d * lax.rsqrt(var + RWKV_LN_EPS) * lng_ref[:, lanes] + lnb_ref[:, lanes]
        rk = r_ref[:, lanes].astype(F32) * k_ref[:, lanes].astype(F32) * rk_ref[:, lanes]
        bonus = _head_sums(rk, hb) * v_ref[:, lanes].astype(F32)
        o_ref[:, lanes] = ((yn + bonus) * _silu(g_ref[:, lanes].astype(F32))).astype(o_ref.dtype)


def _head_sums(x, m):
    return functools.reduce(jnp.add, [_dot(p, m) for p in _bf16_pieces(x, SUM_PIECES)])


RWKV_PAIRS = 8


def _rwkv_scan(prep, u3, ln_g, ln_b, r_k, w, gate_col0):
    r, k, v, na, bv, ld = prep
    b, s, _ = r.shape
    ts = _tile(s, 512)
    width = LANES * _tile(w // LANES, RWKV_PAIRS)
    gc0 = gate_col0 // width
    sb = _tile(ts, MXU_DIM)
    chunk_of = np.arange(sb) // RWKV_CHUNK
    same = chunk_of[:, None] == chunk_of[None, :]
    tri = np.arange(sb)[:, None] >= np.arange(sb)[None, :]
    cl = jnp.asarray(same & tri, BF16)
    cu = jnp.asarray(same & ~tri, BF16)
    gw = _tile(width, MXU_DIM)
    head_of = np.arange(gw) // HEAD
    hb = jnp.asarray(head_of[:, None] == head_of[None, :], BF16)
    row = pl.BlockSpec((None, ts, width), lambda i, p, t: (i, t, p))
    vec = pl.BlockSpec((1, width), lambda i, p, t: (0, p))
    const = lambda n: pl.BlockSpec((n, n), lambda i, p, t: (0, 0))
    tile_bf16 = pltpu.VMEM((ts, width), BF16)
    return pl.pallas_call(
        _rwkv_scan_kernel,
        grid=(b, w // width, s // ts),
        in_specs=[row] * 6 + [pl.BlockSpec((None, ts, width), lambda i, p, t: (i, t, gc0 + p)),
                              vec, vec, vec, const(sb), const(sb), const(gw)],
        out_specs=row,
        out_shape=jax.ShapeDtypeStruct((b, s, w), BF16),
        scratch_shapes=[pltpu.VMEM((width // LANES, LANES, LANES), F32), pltpu.VMEM((ts, width), F32),
                        pltpu.VMEM((ts, width), F32)] + [tile_bf16] * 6,
        compiler_params=_params(("parallel", "parallel", "arbitrary")),
        name="rwkv_scan",
    )(r, k, v, na, bv, ld, u3, ln_g, ln_b, r_k, cl, cu, hb)


FOX_SUB = 32
FOX_ACC_ROWS = HEAD + 16


def _fold8(x, op):
    parts = [x[i:i + 8, :] for i in range(0, x.shape[0], 8)]
    return functools.reduce(op, parts)


def _fox_kernel(qi_ref, kj_ref, q_ref, k_ref, v_ref, kc_ref, g_ref, o_ref,
                m_ref, acc_ref, qx_ref, s_ref, p_ref):
    step = pl.program_id(2)
    qi, kj = qi_ref[step], kj_ref[step]
    tq, tk = q_ref.shape[0], k_ref.shape[0]
    lane = lax.broadcasted_iota(jnp.int32, (1, LANES), 1)

    @pl.when(kj == 0)
    def _():
        m_ref[...] = jnp.full_like(m_ref, NEG_BIG)
        acc_ref[...] = jnp.zeros_like(acc_ref)
        q2 = q_ref[...]
        for h in range(2):
            first = _forget_lane(h, 0)
            ones = jnp.where((lane >= first) & (lane < first + C_PIECES), 1.0, 0.0).astype(BF16)
            qx_ref[h, :, 0:LANES] = jnp.where(lane // HEAD == h, q2, jnp.zeros_like(q2))
            qx_ref[h, :, LANES:2 * LANES] = jnp.broadcast_to(ones, q2.shape)

    kx = jnp.concatenate([k_ref[...], kc_ref[...]], axis=1)
    eye = (lax.broadcasted_iota(jnp.int32, (LANES, LANES), 0)
           == lax.broadcasted_iota(jnp.int32, (LANES, LANES), 1)).astype(BF16)
    v_t = _dot_nt(eye, v_ref[...]).astype(BF16)
    sum_rows = jnp.ones((FOX_ACC_ROWS - HEAD, tk), BF16)

    def step_body(masked):
        for h in range(2):
            s_ref[h] = _dot_nt(kx, qx_ref[h])

        def scores(h, kb):
            blk = s_ref[h, kb * FOX_SUB:(kb + 1) * FOX_SUB, :]
            if masked:
                keep = (kb * FOX_SUB + lax.broadcasted_iota(jnp.int32, (FOX_SUB, tq), 0)
                        <= lax.broadcasted_iota(jnp.int32, (FOX_SUB, tq), 1))
                blk = jnp.where(keep, blk, NEG_BIG)
            return blk

        for h in range(2):
            mx = None
            for kb in range(tk // FOX_SUB):
                part = _fold8(scores(h, kb), jnp.maximum)
                mx = part if mx is None else jnp.maximum(mx, part)
            m_prev = m_ref[h]
            m_new = jnp.maximum(m_prev, jnp.max(mx, axis=0, keepdims=True))
            alpha = jnp.exp2(m_prev - m_new)
            m_ref[h] = m_new
            for kb in range(tk // FOX_SUB):
                p_ref[h, kb * FOX_SUB:(kb + 1) * FOX_SUB, :] = jnp.exp2(scores(h, kb) - m_new).astype(BF16)
            lhs = jnp.concatenate([v_t[h * HEAD:(h + 1) * HEAD, :], sum_rows], axis=0)
            acc_ref[h] = alpha * acc_ref[h] + _dot(lhs, p_ref[h])

    @pl.when(kj < qi)
    def _():
        step_body(False)

    @pl.when(kj == qi)
    def _():
        step_body(True)
        outs = [acc_ref[h, 0:HEAD, :] * (1.0 / acc_ref[h, HEAD:HEAD + 1, :]) for h in range(2)]
        out = jnp.concatenate(outs, axis=0).T
        o_ref[...] = (out * _silu(g_ref[...].astype(F32))).astype(o_ref.dtype)


def _fox_branch(u3, kc, w, q_col0, gate_col0):
    b, s, _ = u3.shape
    tq = _tile(s, 512)
    nq = s // tq
    n_pairs = w // LANES
    pairs = [(i, j) for i in range(nq) for j in range(i + 1)]
    qi = jnp.asarray([p[0] for p in pairs], jnp.int32)
    kj = jnp.asarray([p[1] for p in pairs], jnp.int32)
    qc0, gc0 = q_col0 // LANES, gate_col0 // LANES
    grid_spec = pltpu.PrefetchScalarGridSpec(
        num_scalar_prefetch=2,
        grid=(b, n_pairs, len(pairs)),
        in_specs=[
            pl.BlockSpec((None, tq, LANES), lambda i, p, t, qi, kj: (i, qi[t], qc0 + p)),
            pl.BlockSpec((None, tq, LANES), lambda i, p, t, qi, kj: (i, kj[t], qc0 + n_pairs + p)),
            pl.BlockSpec((None, tq, LANES), lambda i, p, t, qi, kj: (i, kj[t], qc0 + 2 * n_pairs + p)),
            pl.BlockSpec((None, tq, LANES), lambda i, p, t, qi, kj: (i, kj[t], p)),
            pl.BlockSpec((None, tq, LANES), lambda i, p, t, qi, kj: (i, qi[t], gc0 + p)),
        ],
        out_specs=pl.BlockSpec((None, tq, LANES), lambda i, p, t, qi, kj: (i, qi[t], p)),
        scratch_shapes=[pltpu.VMEM((2, 1, tq), F32),
                        pltpu.VMEM((2, FOX_ACC_ROWS, tq), F32), pltpu.VMEM((2, tq, 2 * LANES), BF16),
                        pltpu.VMEM((2, tq, tq), F32), pltpu.VMEM((2, tq, tq), BF16)],
    )
    return pl.pallas_call(
        _fox_kernel,
        grid_spec=grid_spec,
        out_shape=jax.ShapeDtypeStruct((b, s, w), BF16),
        compiler_params=_params(("parallel", "parallel", "arbitrary")),
        name="fox_attention",
    )(qi, kj, u3, u3, u3, kc, u3)


def _merge_kernel(ya_ref, yb_ref, yc_ref, yd_ref, ma_ref, mb_ref, mc_ref, md_ref, bm_ref, wb_ref, o_ref):
    acc = None
    branches = ((ya_ref, ma_ref), (yb_ref, mb_ref), (yc_ref, mc_ref), (yd_ref, md_ref))
    for kbr, (y_ref, ml_ref) in enumerate(branches):
        proj = _dot(y_ref[...], wb_ref[kbr])
        gate = _sigmoid(ml_ref[...].astype(F32) + bm_ref[kbr:kbr + 1, :])
        acc = gate * proj if acc is None else acc + gate * proj
    o_ref[...] = acc.astype(o_ref.dtype)


def _merge(ys, u2, b_merge, w_branch, ml_col0):
    t, w = ys[0].shape
    nb, _, d = w_branch.shape
    tm, tn = _tile(t, 1024), _tile(d, 512)
    nd = d // tn
    mc0 = ml_col0 // tn
    yspec = pl.BlockSpec((tm, w), lambda i, j: (i, 0))
    mspec = lambda k: pl.BlockSpec((tm, tn), lambda i, j, k=k: (i, mc0 + k * nd + j))
    return pl.pallas_call(
        _merge_kernel,
        grid=(t // tm, nd),
        in_specs=[yspec] * 4 + [mspec(k) for k in range(4)] + [
            pl.BlockSpec((nb, tn), lambda i, j: (0, j)),
            pl.BlockSpec((nb, w, tn), lambda i, j: (0, 0, j)),
        ],
        out_specs=pl.BlockSpec((tm, tn), lambda i, j: (i, j)),
        out_shape=jax.ShapeDtypeStruct((t, d), BF16),
        compiler_params=_params(("parallel", "arbitrary")),
        name="merge",
    )(*ys, u2, u2, u2, u2, b_merge, w_branch)


def _outproj_kernel(m_ref, w_ref, x_ref, o_ref):
    o_ref[...] = x_ref[...] + _dot(m_ref[...], w_ref[...])


def _outproj(merged, w_out, x2):
    t, d = x2.shape
    tm, tn = _tile(t, 1024), _tile(d, 512)
    return pl.pallas_call(
        _outproj_kernel,
        grid=(t // tm, d // tn),
        in_specs=[
            pl.BlockSpec((tm, d), lambda i, j: (i, 0)),
            pl.BlockSpec((d, tn), lambda i, j: (0, j)),
            pl.BlockSpec((tm, tn), lambda i, j: (i, j)),
        ],
        out_specs=pl.BlockSpec((tm, tn), lambda i, j: (i, j)),
        out_shape=jax.ShapeDtypeStruct((t, d), F32),
        compiler_params=_params(("parallel", "arbitrary")),
        name="outproj",
    )(merged, w_out, x2)


def _final_norm_kernel(x_ref, g_ref, o_ref):
    o_ref[...] = _rms_rows(x_ref[...], g_ref[...])


def _final_norm(x2, g):
    t, d = x2.shape
    tm = _tile(t, 512)
    return pl.pallas_call(
        _final_norm_kernel,
        grid=(t // tm,),
        in_specs=[pl.BlockSpec((tm, d), lambda i: (i, 0)), pl.BlockSpec((1, d), lambda i: (0, 0))],
        out_specs=pl.BlockSpec((tm, d), lambda i: (i, 0)),
        out_shape=jax.ShapeDtypeStruct((t, d), F32),
        compiler_params=_params(("parallel",)),
        name="final_norm",
    )(x2, g)


def _layer(x2, bsz, seq, norm_g, w_in, b_merge, conv_w, rwkv_mu, rwkv_w0, rwkv_w2, rwkv_a0, rwkv_a2,
           rwkv_kk, rwkv_ka, rwkv_rk, rwkv_ln_g, rwkv_ln_b, fox_bf, pool_w, pool_scale, w_branch, w_out):
    t, d = x2.shape
    w = conv_w.shape[1]
    nh = fox_bf.shape[0]
    lo0, q0, f0 = 7 * w, 8 * w + 2 * LORA, 11 * w + 2 * LORA
    w_main = jnp.concatenate([w_in[:, :lo0], w_in[:, lo0 + 2 * LORA:q0],
                              w_in[:, q0:q0 + w] * (HEAD ** -0.5 * LOG2E),
                              w_in[:, q0 + w:f0], w_in[:, f0 + nh:]], axis=1).astype(BF16)
    pad = jnp.zeros((d, LANES - nh), w_in.dtype)
    w_small = jnp.concatenate([w_in[:, lo0:lo0 + 2 * LORA], w_in[:, f0:f0 + nh], pad], axis=1).astype(BF16)
    b_f = jnp.concatenate([fox_bf, jnp.zeros((LANES - nh,), fox_bf.dtype)]).reshape(1, LANES)
    row = lambda a: a.reshape(1, -1)

    u2 = _inproj(x2, row(norm_g), w_main)
    u3 = u2.reshape(bsz, seq, -1)
    lora, kc = _side(x2.reshape(bsz, seq, d), row(norm_g), w_small, b_f, w)
    y_a = _conv_branch(u3, conv_w, w, 0)
    prep = _rwkv_prep(u3, lora, rwkv_mu[:3 * w].reshape(3, w), row(rwkv_mu[3 * w:]), row(rwkv_w0),
                      rwkv_w2.astype(BF16), row(rwkv_a0), rwkv_a2.astype(BF16), row(rwkv_kk), row(rwkv_ka),
                      w, 4 * w)
    y_b = _rwkv_scan(prep, u3, row(rwkv_ln_g), row(rwkv_ln_b), row(rwkv_rk), w, 7 * w)
    y_c = _fox_branch(u3, kc, w, 8 * w, 11 * w)
    y_d = _pool_branch(u3, pool_w.astype(BF16), row(pool_scale), w, 12 * w)
    ys = [y.reshape(t, w) for y in (y_a, y_b, y_c, y_d)]
    merged = _merge(ys, u2, b_merge, w_branch.astype(BF16), 14 * w)
    return _outproj(merged, w_out.astype(BF16), x2)


def kernel(x, norm_g, w_in, b_merge, conv_w, rwkv_mu, rwkv_w0, rwkv_w2, rwkv_a0, rwkv_a2, rwkv_kk, rwkv_ka,
           rwkv_rk, rwkv_ln_g, rwkv_ln_b, fox_bf, pool_w, pool_scale, w_branch, w_out, final_g):
    bsz, seq, d = x.shape
    x2 = x.reshape(bsz * seq, d)
    for l in range(norm_g.shape[0]):
        x2 = _layer(x2, bsz, seq, norm_g[l], w_in[l], b_merge[l], conv_w[l], rwkv_mu[l], rwkv_w0[l],
                    rwkv_w2[l], rwkv_a0[l], rwkv_a2[l], rwkv_kk[l], rwkv_ka[l], rwkv_rk[l], rwkv_ln_g[l],
                    rwkv_ln_b[l], fox_bf[l], pool_w[l], pool_scale[l], w_branch[l], w_out[l])
    return _final_norm(x2, final_g.reshape(1, d)).reshape(bsz, seq, d)
```

```python
import functools

import jax
import jax.numpy as jnp
import numpy as np
from jax import lax
from jax.experimental import pallas as pl
from jax.experimental.pallas import tpu as pltpu

F32 = jnp.float32
BF16 = jnp.bfloat16
HIGHEST = lax.Precision.HIGHEST

NORM_EPS = 1e-6
RWKV_LN_EPS = 64e-5
HEAD = 64
LORA = 64
POOL_WINDOWS = (2, 4, 8, 16)
RWKV_CHUNK = 64
LANES = 128
MXU_DIM = 256
LOG2E = float(np.log2(np.e))
NEG_BIG = -1e30
VMEM_LIMIT = 48 * 1024 * 1024


def _params(sem):
    return pltpu.CompilerParams(dimension_semantics=sem, vmem_limit_bytes=VMEM_LIMIT)


def _tile(n, want):
    t = min(n, want)
    while n % t:
        t //= 2
    return t


def _sigmoid(x):
    return 1.0 / (1.0 + jnp.exp(-x))


def _silu(x):
    return x * _sigmoid(x)


def _rms_rows(xf, g):
    ms = jnp.mean(xf * xf, axis=-1, keepdims=True)
    return xf * lax.rsqrt(ms + NORM_EPS) * g


def _dot(a, b):
    return jnp.dot(a, b, preferred_element_type=F32)


def _dot_nt(a, b):
    return lax.dot_general(a, b, (((1,), (1,)), ((), ())), preferred_element_type=F32)


def _dot_f32(a, b):
    return jnp.dot(a, b, precision=HIGHEST, preferred_element_type=F32)


def _inproj_kernel(x_ref, g_ref, w_ref, o_ref, h_ref):
    @pl.when(pl.program_id(1) == 0)
    def _():
        h_ref[...] = _rms_rows(x_ref[...], g_ref[...]).astype(BF16)

    o_ref[...] = _dot(h_ref[...], w_ref[...]).astype(o_ref.dtype)


def _inproj(x2, g, w):
    t, d = x2.shape
    n = w.shape[1]
    tm, tn = _tile(t, 1024), _tile(n, 512)
    return pl.pallas_call(
        _inproj_kernel,
        grid=(t // tm, n // tn),
        in_specs=[
            pl.BlockSpec((tm, d), lambda i, j: (i, 0)),
            pl.BlockSpec((1, d), lambda i, j: (0, 0)),
            pl.BlockSpec((d, tn), lambda i, j: (0, j)),
        ],
        out_specs=pl.BlockSpec((tm, tn), lambda i, j: (i, j)),
        out_shape=jax.ShapeDtypeStruct((t, n), BF16),
        scratch_shapes=[pltpu.VMEM((tm, d), BF16)],
        compiler_params=_params(("parallel", "arbitrary")),
        name="inproj",
    )(x2, g, w)


C_PIECES = 3


def _bf16_pieces(x, n=C_PIECES):
    pieces = []
    for _ in range(n):
        p = x.astype(BF16)
        pieces.append(p)
        x = x - p.astype(F32)
    return pieces


def _side_kernel(x_ref, g_ref, ws_ref, bf_ref, sel_ref, lora_ref, kc_ref, carry_ref):
    @pl.when(pl.program_id(1) == 0)
    def _():
        carry_ref[...] = jnp.zeros_like(carry_ref)

    h = _rms_rows(x_ref[...], g_ref[...]).astype(BF16)
    sf = _dot(h, ws_ref[...])
    ns = lora_ref.shape[1]
    lora_ref[...] = sf[:, :ns]
    z = sf[:, ns:] + bf_ref[...]
    logf = jnp.minimum(z, 0.0) - jnp.log1p(jnp.exp(-jnp.abs(z)))
    ts = logf.shape[0]
    lower = (lax.broadcasted_iota(jnp.int32, (ts, ts), 0)
             >= lax.broadcasted_iota(jnp.int32, (ts, ts), 1)).astype(F32)
    c = _dot_f32(lower, logf) + carry_ref[...]
    carry_ref[...] = c[ts - 1:ts, :]
    kc = None
    for piece, sel in zip(_bf16_pieces(-LOG2E * c), (sel_ref[0], sel_ref[1], sel_ref[2])):
        term = _dot(piece, sel)
        kc = term if kc is None else kc + term
    kc_ref[...] = kc.astype(kc_ref.dtype)


def _forget_lane(head, piece):
    return C_PIECES * head + piece


def _side(x3, g, w_small, b_f, w):
    b, s, d = x3.shape
    ns = w_small.shape[1] - LANES
    ts = _tile(s, 512)
    nh = w // HEAD
    sel = np.zeros((C_PIECES, LANES, w), np.float32)
    for hd in range(nh):
        for piece in range(C_PIECES):
            sel[piece, hd, LANES * (hd // 2) + _forget_lane(hd % 2, piece)] = 1.0
    return pl.pallas_call(
        _side_kernel,
        grid=(b, s // ts),
        in_specs=[
            pl.BlockSpec((None, ts, d), lambda i, j: (i, j, 0)),
            pl.BlockSpec((1, d), lambda i, j: (0, 0)),
            pl.BlockSpec((d, ns + LANES), lambda i, j: (0, 0)),
            pl.BlockSpec((1, LANES), lambda i, j: (0, 0)),
            pl.BlockSpec((C_PIECES, LANES, w), lambda i, j: (0, 0, 0)),
        ],
        out_specs=[
            pl.BlockSpec((None, ts, ns), lambda i, j: (i, j, 0)),
            pl.BlockSpec((None, ts, w), lambda i, j: (i, j, 0)),
        ],
        out_shape=[
            jax.ShapeDtypeStruct((b, s, ns), F32),
            jax.ShapeDtypeStruct((b, s, w), BF16),
        ],
        scratch_shapes=[pltpu.VMEM((1, LANES), F32)],
        compiler_params=_params(("parallel", "arbitrary")),
        name="side",
    )(x3, g, w_small, b_f, jnp.asarray(sel, BF16))


HALO = 8


def _conv_kernel(bg_ref, cg_ref, xv_ref, g_ref, cw_ref, o_ref, pbuf):
    ts = o_ref.shape[0]

    @pl.when(pl.program_id(2) == 0)
    def _():
        pbuf[0:HALO, :] = jnp.zeros((HALO, pbuf.shape[1]), F32)

    p = cg_ref[...].astype(F32) * xv_ref[...].astype(F32)
    pbuf[HALO:HALO + ts, :] = p
    cw = cw_ref[...]
    z = (cw[0:1] * pbuf[HALO - 2:HALO - 2 + ts, :] + cw[1:2] * pbuf[HALO - 1:HALO - 1 + ts, :]
         + cw[2:3] * p)
    o_ref[...] = (bg_ref[...].astype(F32) * z * _silu(g_ref[...].astype(F32))).astype(o_ref.dtype)
    pbuf[0:HALO, :] = pbuf[ts:ts + HALO, :]


def _conv_branch(u3, conv_w, w, col0):
    b, s, _ = u3.shape
    ts, tw = _tile(s, 512), _tile(w, 512)
    nw = w // tw
    c0 = col0 // tw

    def spec(k):
        return pl.BlockSpec((None, ts, tw), lambda i, j, t, k=k: (i, t, c0 + k * nw + j))

    return pl.pallas_call(
        _conv_kernel,
        grid=(b, nw, s // ts),
        in_specs=[spec(0), spec(1), spec(2), spec(3),
                  pl.BlockSpec((conv_w.shape[0], tw), lambda i, j, t: (0, j))],
        out_specs=pl.BlockSpec((None, ts, tw), lambda i, j, t: (i, t, j)),
        out_shape=jax.ShapeDtypeStruct((b, s, w), BF16),
        scratch_shapes=[pltpu.VMEM((HALO + ts, tw), F32)],
        compiler_params=_params(("parallel", "parallel", "arbitrary")),
        name="conv_branch",
    )(u3, u3, u3, u3, conv_w)


POOL_HALO = 16


def _pool_kernel(x_ref, g_ref, pw_ref, sc_ref, o_ref, xbuf):
    ts, w = o_ref.shape
    gw = w // len(POOL_WINDOWS)
    s = pl.program_id(1)

    @pl.when(s == 0)
    def _():
        xbuf[0:POOL_HALO, :] = jnp.zeros((POOL_HALO, w), F32)

    x = x_ref[...].astype(F32)
    xbuf[POOL_HALO:POOL_HALO + ts, :] = x
    pos = s * ts + lax.broadcasted_iota(jnp.int32, (ts, 1), 0)
    for gi, win in enumerate(POOL_WINDOWS):
        lo, hi = gi * gw, (gi + 1) * gw
        xi = x[:, lo:hi]
        acc = xi
        for k in range(1, win):
            acc = acc + xbuf[POOL_HALO - k:POOL_HALO - k + ts, lo:hi]
        count = jnp.minimum(pos + 1, win).astype(F32)
        pooled = acc / count - xi
        y = _dot(pooled.astype(BF16), pw_ref[gi])
        o_ref[:, lo:hi] = (y * sc_ref[:, lo:hi] * _silu(g_ref[:, lo:hi].astype(F32))).astype(o_ref.dtype)
    xbuf[0:POOL_HALO, :] = xbuf[ts:ts + POOL_HALO, :]


def _pool_branch(u3, pool_w, pool_scale, w, col0):
    b, s, _ = u3.shape
    ts = _tile(s, 512)
    c0 = col0 // w
    ng, gw, _ = pool_w.shape
    return pl.pallas_call(
        _pool_kernel,
        grid=(b, s // ts),
        in_specs=[
            pl.BlockSpec((None, ts, w), lambda i, t: (i, t, c0)),
            pl.BlockSpec((None, ts, w), lambda i, t: (i, t, c0 + 1)),
            pl.BlockSpec((ng, gw, gw), lambda i, t: (0, 0, 0)),
            pl.BlockSpec((1, w), lambda i, t: (0, 0)),
        ],
        out_specs=pl.BlockSpec((None, ts, w), lambda i, t: (i, t, 0)),
        out_shape=jax.ShapeDtypeStruct((b, s, w), BF16),
        scratch_shapes=[pltpu.VMEM((POOL_HALO + ts, w), F32)],
        compiler_params=_params(("parallel", "arbitrary")),
        name="pool_branch",
    )(u3, u3, pool_w, pool_scale)


def _rwkv_prep_kernel(r_ref, k_ref, v_ref, lo_ref, mu_ref, mul_ref, w0_ref, w2_ref, a0_ref, a2_ref,
                      kkw_ref, kaw_ref, hsum_ref, hbc_ref,
                      ro_ref, ko_ref, vo_ref, na_ref, bv_ref, ld_ref, buf, lbuf):
    ts, w = ro_ref.shape

    @pl.when(pl.program_id(1) == 0)
    def _():
        buf[0:HALO, :] = jnp.zeros((HALO, buf.shape[1]), F32)
        lbuf[0:HALO, :] = jnp.zeros((HALO, lbuf.shape[1]), F32)

    for idx, ref in enumerate((r_ref, k_ref, v_ref)):
        buf[HALO:HALO + ts, idx * w:(idx + 1) * w] = ref[...].astype(F32)
    lbuf[HALO:HALO + ts, :] = lo_ref[...]

    def mixed(cur, prev, mu):
        return cur + (prev - cur) * mu

    mu = mu_ref[...]
    r = mixed(buf[HALO:HALO + ts, 0:w], buf[HALO - 1:HALO - 1 + ts, 0:w], mu[0:1])
    k = mixed(buf[HALO:HALO + ts, w:2 * w], buf[HALO - 1:HALO - 1 + ts, w:2 * w], mu[1:2])
    v = mixed(buf[HALO:HALO + ts, 2 * w:3 * w], buf[HALO - 1:HALO - 1 + ts, 2 * w:3 * w], mu[2:3])
    lo = mixed(lbuf[HALO:HALO + ts, :], lbuf[HALO - 1:HALO - 1 + ts, :], mul_ref[...])
    wl, al = lo[:, 0:LORA], lo[:, LORA:2 * LORA]

    z = w0_ref[...] + _dot(jnp.tanh(wl).astype(BF16), w2_ref[...])
    ld_ref[...] = -float(np.exp(-0.5)) * _sigmoid(z)
    a = _sigmoid(a0_ref[...] + _dot(al.astype(BF16), a2_ref[...]))

    kk = k * kkw_ref[...]
    n2 = _dot_f32(_dot_f32(kk * kk, hsum_ref[...]), hbc_ref[...])
    kk = kk * lax.rsqrt(jnp.maximum(n2, 1e-24))
    ro_ref[...] = r.astype(ro_ref.dtype)
    ko_ref[...] = (k * (1.0 + (a - 1.0) * kaw_ref[...])).astype(ko_ref.dtype)
    vo_ref[...] = v.astype(vo_ref.dtype)
    na_ref[...] = (-kk).astype(na_ref.dtype)
    bv_ref[...] = (kk * a).astype(bv_ref.dtype)

    buf[0:HALO, :] = buf[ts:ts + HALO, :]
    lbuf[0:HALO, :] = lbuf[ts:ts + HALO, :]


def _rwkv_prep(u3, lora, mu3, mul, w0, w2, a0, a2, kkw, kaw, w, col0):
    b, s, _ = u3.shape
    ts = _tile(s, 256)
    c0 = col0 // w
    nh = w // HEAD
    head_of = np.arange(w) // HEAD
    hsum = jnp.asarray((head_of[:, None] == np.arange(nh)[None, :]).astype(np.float32))
    hbc = hsum.T
    row = lambda k: pl.BlockSpec((None, ts, w), lambda i, t, k=k: (i, t, c0 + k))
    full = lambda shape: pl.BlockSpec(shape, lambda i, t: (0,) * len(shape))
    outw = pl.BlockSpec((None, ts, w), lambda i, t: (i, t, 0))
    sds = lambda dt: jax.ShapeDtypeStruct((b, s, w), dt)
    return pl.pallas_call(
        _rwkv_prep_kernel,
        grid=(b, s // ts),
        in_specs=[row(0), row(1), row(2),
                  pl.BlockSpec((None, ts, 2 * LORA), lambda i, t: (i, t, 0)),
                  full((3, w)), full((1, 2 * LORA)), full((1, w)), full((LORA, w)),
                  full((1, w)), full((LORA, w)), full((1, w)), full((1, w)),
                  full((w, nh)), full((nh, w))],
        out_specs=[outw] * 6,
        out_shape=[sds(BF16)] * 5 + [sds(F32)],
        scratch_shapes=[pltpu.VMEM((HALO + ts, 3 * w), F32), pltpu.VMEM((HALO + ts, 2 * LORA), F32)],
        compiler_params=_params(("parallel", "arbitrary")),
        name="rwkv_prep",
    )(u3, u3, u3, lora, mu3, mul, w0, w2, a0, a2, kkw, kaw, hsum, hbc)


def _head_stack(x, lane_head):
    zero = jnp.zeros_like(x)
    return jnp.concatenate([jnp.where(lane_head == 0, x, zero), jnp.where(lane_head == 1, x, zero)], axis=0)


SUM_PIECES = 2


def _rwkv_scan_kernel(r_ref, k_ref, v_ref, na_ref, bv_ref, ld_ref, g_ref, lng_ref, lnb_ref, rk_ref,
                      cl_ref, cu_ref, hb_ref, o_ref, state, ybuf, dec_s, rh_s, ah_s, bt_s, kt_s, bb_s, kb_s):
    ts, width = o_ref.shape
    n_pairs = width // LANES
    c = RWKV_CHUNK
    n_chunks = ts // c

    @pl.when(pl.program_id(2) == 0)
    def _():
        state[...] = jnp.zeros_like(state)

    def chunk_sums(m_ref):
        sb = m_ref.shape[0]
        outs = []
        for r0 in range(0, ts, sb):
            pieces = jnp.concatenate(_bf16_pieces(ld_ref[r0:r0 + sb, :], SUM_PIECES), axis=1)
            both = _dot(m_ref[...], pieces)
            outs.append(both[:, 0:width] + both[:, width:2 * width])
        return jnp.concatenate(outs, axis=0)

    ld = ld_ref[...]
    lc = chunk_sums(cl_ref)
    lrest = chunk_sums(cu_ref)
    e_in = jnp.exp(lc)
    e_neg = jnp.exp(-lc)
    e_end = jnp.exp(lrest)
    r_all = r_ref[...].astype(F32)
    k_all = k_ref[...].astype(F32)
    na_all = na_ref[...].astype(F32)
    bv_all = bv_ref[...].astype(F32)
    dec_s[...] = e_in
    rh_s[...] = (r_all * e_in).astype(BF16)
    ah_s[...] = (na_all * jnp.exp(lc - ld)).astype(BF16)
    bt_s[...] = (bv_all * e_neg).astype(BF16)
    kt_s[...] = (k_all * e_neg).astype(BF16)
    bb_s[...] = (bv_all * e_end).astype(BF16)
    kb_s[...] = (k_all * e_end).astype(BF16)

    lane_head = lax.broadcasted_iota(jnp.int32, (1, LANES), 1) // HEAD
    t_idx = lax.broadcasted_iota(jnp.int32, (c, LANES), 0)
    s_idx = lax.broadcasted_iota(jnp.int32, (c, LANES), 1) % c
    strict = s_idx < t_idx
    incl = s_idx <= t_idx
    blk = (lax.broadcasted_iota(jnp.int32, (LANES, LANES), 0) // HEAD
           == lax.broadcasted_iota(jnp.int32, (LANES, LANES), 1) // HEAD)
    eye_cat = (s_idx == t_idx).astype(F32)

    def chunk(ci, states):
        rows = pl.ds(pl.multiple_of(ci * c, c), c)
        tail_rows = pl.ds(pl.multiple_of(ci * c + c - 8, 8), 8)
        pairs = range(n_pairs)
        lanes = [slice(pi * LANES, (pi + 1) * LANES) for pi in pairs]
        each = lambda f: [f(pi) for pi in pairs]
        v = each(lambda pi: v_ref[rows, lanes[pi]])
        rh = each(lambda pi: rh_s[rows, lanes[pi]])
        ah = each(lambda pi: ah_s[rows, lanes[pi]])
        sc = each(lambda pi: _dot_nt(
            jnp.concatenate([ah[pi], rh[pi]], axis=0),
            jnp.concatenate([_head_stack(bt_s[rows, lanes[pi]], lane_head),
                             _head_stack(kt_s[rows, lanes[pi]], lane_head)], axis=0)))
        a_ak = each(lambda pi: jnp.where(strict, sc[pi][0:c, LANES:2 * LANES], 0.0).astype(BF16))
        a_rb = each(lambda pi: jnp.where(incl, sc[pi][c:2 * c, 0:LANES], 0.0).astype(BF16))
        a_rk = each(lambda pi: jnp.where(incl, sc[pi][c:2 * c, LANES:2 * LANES], 0.0).astype(BF16))

        x = each(lambda pi: jnp.where(strict, sc[pi][0:c, 0:LANES], 0.0))
        tinv = each(lambda pi: eye_cat + x[pi])
        p2 = 2
        while p2 < c:
            xb = each(lambda pi: x[pi].astype(BF16))
            x = each(lambda pi: _dot(xb[pi], _head_stack(xb[pi], lane_head)))
            tinv = each(lambda pi: tinv[pi] + _dot(tinv[pi].astype(BF16),
                                                   _head_stack(x[pi].astype(BF16), lane_head)))
            p2 *= 2
        t_cat = each(lambda pi: tinv[pi].astype(BF16))

        g2b = each(lambda pi: states[pi].astype(BF16))
        vs = each(lambda pi: _head_stack(v[pi], lane_head))
        p = each(lambda pi: _dot_nt(ah[pi], g2b[pi]) + _dot(a_ak[pi], vs[pi]))
        ub = each(lambda pi: _dot(t_cat[pi], _head_stack(p[pi].astype(BF16), lane_head)).astype(BF16))
        new_states = []
        for pi in pairs:
            y = (_dot_nt(rh[pi], g2b[pi]) + _dot(a_rb[pi], _head_stack(ub[pi], lane_head))
                 + _dot(a_rk[pi], vs[pi]))
            ybuf[rows, lanes[pi]] = y
            uv_t = jnp.concatenate([ub[pi], v[pi]], axis=0).T
            upd = _dot(uv_t, jnp.concatenate([bb_s[rows, lanes[pi]], kb_s[rows, lanes[pi]]], axis=0))
            decay = dec_s[tail_rows, lanes[pi]][7:8, :]
            new_states.append(states[pi] * decay + jnp.where(blk, upd, 0.0))
        return tuple(new_states)

    final = lax.fori_loop(0, n_chunks, chunk, tuple(state[pi] for pi in range(n_pairs)))
    for pi in range(n_pairs):
        state[pi] = final[pi]

    hb = hb_ref[...]
    gw = hb.shape[0]
    for l0 in range(0, width, gw):
        lanes = slice(l0, l0 + gw)
        y = ybuf[:, lanes]
        mean = _head_sums(y, hb) * (1.0 / HEAD)
        d = y - mean
        var = _head_sums(d * d, hb) * (1.0 / HEAD)
        yn = d * lax.rsqrt(var + RWKV_LN_EPS) * lng_ref[:, lanes] + lnb_ref[:, lanes]
        rk = r_ref[:, lanes].astype(F32) * k_ref[:, lanes].astype(F32) * rk_ref[:, lanes]
        bonus = _head_sums(rk, hb) * v_ref[:, lanes].astype(F32)
        o_ref[:, lanes] = ((yn + bonus) * _silu(g_ref[:, lanes].astype(F32))).astype(o_ref.dtype)


def _head_sums(x, m):
    return functools.reduce(jnp.add, [_dot(p, m) for p in _bf16_pieces(x, SUM_PIECES)])


RWKV_PAIRS = 8


def _rwkv_scan(prep, u3, ln_g, ln_b, r_k, w, gate_col0):
    r, k, v, na, bv, ld = prep
    b, s, _ = r.shape
    ts = _tile(s, 512)
    width = LANES * _tile(w // LANES, RWKV_PAIRS)
    gc0 = gate_col0 // width
    sb = _tile(ts, MXU_DIM)
    chunk_of = np.arange(sb) // RWKV_CHUNK
    same = chunk_of[:, None] == chunk_of[None, :]
    tri = np.arange(sb)[:, None] >= np.arange(sb)[None, :]
    cl = jnp.asarray(same & tri, BF16)
    cu = jnp.asarray(same & ~tri, BF16)
    gw = _tile(width, MXU_DIM)
    head_of = np.arange(gw) // HEAD
    hb = jnp.asarray(head_of[:, None] == head_of[None, :], BF16)
    row = pl.BlockSpec((None, ts, width), lambda i, p, t: (i, t, p))
    vec = pl.BlockSpec((1, width), lambda i, p, t: (0, p))
    const = lambda n: pl.BlockSpec((n, n), lambda i, p, t: (0, 0))
    tile_bf16 = pltpu.VMEM((ts, width), BF16)
    return pl.pallas_call(
        _rwkv_scan_kernel,
        grid=(b, w // width, s // ts),
        in_specs=[row] * 6 + [pl.BlockSpec((None, ts, width), lambda i, p, t: (i, t, gc0 + p)),
                              vec, vec, vec, const(sb), const(sb), const(gw)],
        out_specs=row,
        out_shape=jax.ShapeDtypeStruct((b, s, w), BF16),
        scratch_shapes=[pltpu.VMEM((width // LANES, LANES, LANES), F32), pltpu.VMEM((ts, width), F32),
                        pltpu.VMEM((ts, width), F32)] + [tile_bf16] * 6,
        compiler_params=_params(("parallel", "parallel", "arbitrary")),
        name="rwkv_scan",
    )(r, k, v, na, bv, ld, u3, ln_g, ln_b, r_k, cl, cu, hb)


FOX_TQ = 1024
FOX_TK = 1024
FOX_SUB = 16
FOX_ACC_ROWS = HEAD + 16


def _fold8(x, op):
    parts = [x[i:i + 8, :] for i in range(0, x.shape[0], 8)]
    return functools.reduce(op, parts)


def _fox_kernel(qi_ref, kj_ref, q_ref, k_ref, v_ref, kc_ref, g_ref, o_ref,
                m_ref, acc_ref, qx_ref, s0_ref, s1_ref, p0_ref, p1_ref):
    step = pl.program_id(2)
    qi, kj = qi_ref[step], kj_ref[step]
    tq, tk = q_ref.shape[0], k_ref.shape[0]
    lane = lax.broadcasted_iota(jnp.int32, (1, LANES), 1)

    @pl.when(kj == 0)
    def _():
        m_ref[...] = jnp.full_like(m_ref, NEG_BIG)
        acc_ref[...] = jnp.zeros_like(acc_ref)
        q2 = q_ref[...]
        for h in range(2):
            first = _forget_lane(h, 0)
            ones = jnp.where((lane >= first) & (lane < first + C_PIECES), 1.0, 0.0).astype(BF16)
            qx_ref[h, :, 0:LANES] = jnp.where(lane // HEAD == h, q2, jnp.zeros_like(q2))
            qx_ref[h, :, LANES:2 * LANES] = jnp.broadcast_to(ones, q2.shape)

    offset = qi * tq - kj * tk

    s_refs, p_refs = (s0_ref, s1_ref), (p0_ref, p1_ref)

    def step_body(masked):
        kx = jnp.concatenate([k_ref[...], kc_ref[...]], axis=1)
        if masked:
            key_minus_query = (lax.broadcasted_iota(jnp.int32, (FOX_SUB, tq), 0)
                               - lax.broadcasted_iota(jnp.int32, (FOX_SUB, tq), 1))

        def score_matmul(h):
            s_refs[h][...] = _dot_nt(kx, qx_ref[h])

        def scores(h, kb):
            blk = s_refs[h][kb * FOX_SUB:(kb + 1) * FOX_SUB, :]
            if masked:
                blk = jnp.where(key_minus_query <= offset - kb * FOX_SUB, blk, NEG_BIG)
            return blk

        def running_max(h):
            mx = None
            for kb in range(tk // FOX_SUB):
                part = _fold8(scores(h, kb), jnp.maximum)
                mx = part if mx is None else jnp.maximum(mx, part)
            m_prev = m_ref[h]
            m_new = jnp.maximum(m_prev, jnp.max(mx, axis=0, keepdims=True))
            m_ref[h] = m_new
            return m_new, jnp.exp2(m_prev - m_new)

        def probabilities(h, m_new):
            for kb in range(tk // FOX_SUB):
                p_refs[h][kb * FOX_SUB:(kb + 1) * FOX_SUB, :] = jnp.exp2(scores(h, kb) - m_new).astype(BF16)

        def accumulate(h, alpha):
            lhs = jnp.concatenate([v_t[h * HEAD:(h + 1) * HEAD, :], sum_rows], axis=0)
            acc_ref[h] = alpha * acc_ref[h] + _dot(lhs, p_refs[h][...])

        score_matmul(0)
        m0, alpha0 = running_max(0)
        score_matmul(1)
        probabilities(0, m0)
        eye = (lax.broadcasted_iota(jnp.int32, (LANES, LANES), 0)
               == lax.broadcasted_iota(jnp.int32, (LANES, LANES), 1)).astype(BF16)
        v_t = _dot_nt(eye, v_ref[...]).astype(BF16)
        sum_rows = jnp.ones((FOX_ACC_ROWS - HEAD, tk), BF16)
        m1, alpha1 = running_max(1)
        accumulate(0, alpha0)
        probabilities(1, m1)
        accumulate(1, alpha1)

    crosses_diagonal = (kj + 1) * tk - 1 > qi * tq

    @pl.when(jnp.logical_not(crosses_diagonal))
    def _():
        step_body(False)

    @pl.when(crosses_diagonal)
    def _():
        step_body(True)

    @pl.when((kj + 1) * tk == (qi + 1) * tq)
    def _():
        outs = [acc_ref[h, 0:HEAD, :] * (1.0 / acc_ref[h, HEAD:HEAD + 1, :]) for h in range(2)]
        out = jnp.concatenate(outs, axis=0).T
        o_ref[...] = (out * _silu(g_ref[...].astype(F32))).astype(o_ref.dtype)


def _fox_branch(u3, kc, w, q_col0, gate_col0):
    b, s, _ = u3.shape
    tq, tk = _tile(s, FOX_TQ), _tile(s, FOX_TK)
    n_pairs = w // LANES
    pairs = [(i, j) for i in range(s // tq) for j in range((i + 1) * tq // tk)]
    qi = jnp.asarray([p[0] for p in pairs], jnp.int32)
    kj = jnp.asarray([p[1] for p in pairs], jnp.int32)
    qc0, gc0 = q_col0 // LANES, gate_col0 // LANES
    grid_spec = pltpu.PrefetchScalarGridSpec(
        num_scalar_prefetch=2,
        grid=(b, n_pairs, len(pairs)),
        in_specs=[
            pl.BlockSpec((None, tq, LANES), lambda i, p, t, qi, kj: (i, qi[t], qc0 + p)),
            pl.BlockSpec((None, tk, LANES), lambda i, p, t, qi, kj: (i, kj[t], qc0 + n_pairs + p)),
            pl.BlockSpec((None, tk, LANES), lambda i, p, t, qi, kj: (i, kj[t], qc0 + 2 * n_pairs + p)),
            pl.BlockSpec((None, tk, LANES), lambda i, p, t, qi, kj: (i, kj[t], p)),
            pl.BlockSpec((None, tq, LANES), lambda i, p, t, qi, kj: (i, qi[t], gc0 + p)),
        ],
        out_specs=pl.BlockSpec((None, tq, LANES), lambda i, p, t, qi, kj: (i, qi[t], p)),
        scratch_shapes=[pltpu.VMEM((2, 1, tq), F32),
                        pltpu.VMEM((2, FOX_ACC_ROWS, tq), F32), pltpu.VMEM((2, tq, 2 * LANES), BF16),
                        pltpu.VMEM((tk, tq), F32), pltpu.VMEM((tk, tq), F32),
                        pltpu.VMEM((tk, tq), BF16), pltpu.VMEM((tk, tq), BF16)],
    )
    return pl.pallas_call(
        _fox_kernel,
        grid_spec=grid_spec,
        out_shape=jax.ShapeDtypeStruct((b, s, w), BF16),
        compiler_params=_params(("parallel", "parallel", "arbitrary")),
        name="fox_attention",
    )(qi, kj, u3, u3, u3, kc, u3)


def _merge_kernel(ya_ref, yb_ref, yc_ref, yd_ref, ma_ref, mb_ref, mc_ref, md_ref, bm_ref, wb_ref, o_ref):
    acc = None
    branches = ((ya_ref, ma_ref), (yb_ref, mb_ref), (yc_ref, mc_ref), (yd_ref, md_ref))
    for kbr, (y_ref, ml_ref) in enumerate(branches):
        proj = _dot(y_ref[...], wb_ref[kbr])
        gate = _sigmoid(ml_ref[...].astype(F32) + bm_ref[kbr:kbr + 1, :])
        acc = gate * proj if acc is None else acc + gate * proj
    o_ref[...] = acc.astype(o_ref.dtype)


def _merge(ys, u2, b_merge, w_branch, ml_col0):
    t, w = ys[0].shape
    nb, _, d = w_branch.shape
    tm, tn = _tile(t, 1024), _tile(d, 512)
    nd = d // tn
    mc0 = ml_col0 // tn
    yspec = pl.BlockSpec((tm, w), lambda i, j: (i, 0))
    mspec = lambda k: pl.BlockSpec((tm, tn), lambda i, j, k=k: (i, mc0 + k * nd + j))
    return pl.pallas_call(
        _merge_kernel,
        grid=(t // tm, nd),
        in_specs=[yspec] * 4 + [mspec(k) for k in range(4)] + [
            pl.BlockSpec((nb, tn), lambda i, j: (0, j)),
            pl.BlockSpec((nb, w, tn), lambda i, j: (0, 0, j)),
        ],
        out_specs=pl.BlockSpec((tm, tn), lambda i, j: (i, j)),
        out_shape=jax.ShapeDtypeStruct((t, d), BF16),
        compiler_params=_params(("parallel", "arbitrary")),
        name="merge",
    )(*ys, u2, u2, u2, u2, b_merge, w_branch)


def _outproj_kernel(m_ref, w_ref, x_ref, o_ref):
    o_ref[...] = x_ref[...] + _dot(m_ref[...], w_ref[...])


def _outproj(merged, w_out, x2):
    t, d = x2.shape
    tm, tn = _tile(t, 1024), _tile(d, 512)
    return pl.pallas_call(
        _outproj_kernel,
        grid=(t // tm, d // tn),
        in_specs=[
            pl.BlockSpec((tm, d), lambda i, j: (i, 0)),
            pl.BlockSpec((d, tn), lambda i, j: (0, j)),
            pl.BlockSpec((tm, tn), lambda i, j: (i, j)),
        ],
        out_specs=pl.BlockSpec((tm, tn), lambda i, j: (i, j)),
        out_shape=jax.ShapeDtypeStruct((t, d), F32),
        compiler_params=_params(("parallel", "arbitrary")),
        name="outproj",
    )(merged, w_out, x2)


def _final_norm_kernel(x_ref, g_ref, o_ref):
    o_ref[...] = _rms_rows(x_ref[...], g_ref[...])


def _final_norm(x2, g):
    t, d = x2.shape
    tm = _tile(t, 512)
    return pl.pallas_call(
        _final_norm_kernel,
        grid=(t // tm,),
        in_specs=[pl.BlockSpec((tm, d), lambda i: (i, 0)), pl.BlockSpec((1, d), lambda i: (0, 0))],
        out_specs=pl.BlockSpec((tm, d), lambda i: (i, 0)),
        out_shape=jax.ShapeDtypeStruct((t, d), F32),
        compiler_params=_params(("parallel",)),
        name="final_norm",
    )(x2, g)


def _layer(x2, bsz, seq, norm_g, w_in, b_merge, conv_w, rwkv_mu, rwkv_w0, rwkv_w2, rwkv_a0, rwkv_a2,
           rwkv_kk, rwkv_ka, rwkv_rk, rwkv_ln_g, rwkv_ln_b, fox_bf, pool_w, pool_scale, w_branch, w_out):
    t, d = x2.shape
    w = conv_w.shape[1]
    nh = fox_bf.shape[0]
    lo0, q0, f0 = 7 * w, 8 * w + 2 * LORA, 11 * w + 2 * LORA
    w_main = jnp.concatenate([w_in[:, :lo0], w_in[:, lo0 + 2 * LORA:q0],
                              w_in[:, q0:q0 + w] * (HEAD ** -0.5 * LOG2E),
                              w_in[:, q0 + w:f0], w_in[:, f0 + nh:]], axis=1).astype(BF16)
    pad = jnp.zeros((d, LANES - nh), w_in.dtype)
    w_small = jnp.concatenate([w_in[:, lo0:lo0 + 2 * LORA], w_in[:, f0:f0 + nh], pad], axis=1).astype(BF16)
    b_f = jnp.concatenate([fox_bf, jnp.zeros((LANES - nh,), fox_bf.dtype)]).reshape(1, LANES)
    row = lambda a: a.reshape(1, -1)

    u2 = _inproj(x2, row(norm_g), w_main)
    u3 = u2.reshape(bsz, seq, -1)
    lora, kc = _side(x2.reshape(bsz, seq, d), row(norm_g), w_small, b_f, w)
    y_a = _conv_branch(u3, conv_w, w, 0)
    prep = _rwkv_prep(u3, lora, rwkv_mu[:3 * w].reshape(3, w), row(rwkv_mu[3 * w:]), row(rwkv_w0),
                      rwkv_w2.astype(BF16), row(rwkv_a0), rwkv_a2.astype(BF16), row(rwkv_kk), row(rwkv_ka),
                      w, 4 * w)
    y_b = _rwkv_scan(prep, u3, row(rwkv_ln_g), row(rwkv_ln_b), row(rwkv_rk), w, 7 * w)
    y_c = _fox_branch(u3, kc, w, 8 * w, 11 * w)
    y_d = _pool_branch(u3, pool_w.astype(BF16), row(pool_scale), w, 12 * w)
    ys = [y.reshape(t, w) for y in (y_a, y_b, y_c, y_d)]
    merged = _merge(ys, u2, b_merge, w_branch.astype(BF16), 14 * w)
    return _outproj(merged, w_out.astype(BF16), x2)


def kernel(x, norm_g, w_in, b_merge, conv_w, rwkv_mu, rwkv_w0, rwkv_w2, rwkv_a0, rwkv_a2, rwkv_kk, rwkv_ka,
           rwkv_rk, rwkv_ln_g, rwkv_ln_b, fox_bf, pool_w, pool_scale, w_branch, w_out, final_g):
    bsz, seq, d = x.shape
    x2 = x.reshape(bsz * seq, d)
    for l in range(norm_g.shape[0]):
        x2 = _layer(x2, bsz, seq, norm_g[l], w_in[l], b_merge[l], conv_w[l], rwkv_mu[l], rwkv_w0[l],
                    rwkv_w2[l], rwkv_a0[l], rwkv_a2[l], rwkv_kk[l], rwkv_ka[l], rwkv_rk[l], rwkv_ln_g[l],
                    rwkv_ln_b[l], fox_bf[l], pool_w[l], pool_scale[l], w_branch[l], w_out[l])
    return _final_norm(x2, final_g.reshape(1, d)).reshape(bsz, seq, d)
```

```python
import functools

import jax
import jax.numpy as jnp
import numpy as np
from jax import lax
from jax.experimental import pallas as pl
from jax.experimental.pallas import tpu as pltpu

F32 = jnp.float32
BF16 = jnp.bfloat16
HIGHEST = lax.Precision.HIGHEST

NORM_EPS = 1e-6
RWKV_LN_EPS = 64e-5
HEAD = 64
LORA = 64
POOL_WINDOWS = (2, 4, 8, 16)
RWKV_CHUNK = 64
LANES = 128
MXU_DIM = 256
LOG2E = float(np.log2(np.e))
NEG_BIG = -1e30
VMEM_LIMIT = 48 * 1024 * 1024


def _params(sem):
    return pltpu.CompilerParams(dimension_semantics=sem, vmem_limit_bytes=VMEM_LIMIT)


def _tile(n, want):
    t = min(n, want)
    while n % t:
        t //= 2
    return t


def _sigmoid(x):
    return 1.0 / (1.0 + jnp.exp(-x))


def _silu(x):
    return x * _sigmoid(x)


def _rms_rows(xf, g):
    ms = jnp.mean(xf * xf, axis=-1, keepdims=True)
    return xf * lax.rsqrt(ms + NORM_EPS) * g


def _dot(a, b):
    return jnp.dot(a, b, preferred_element_type=F32)


def _dot_nt(a, b):
    return lax.dot_general(a, b, (((1,), (1,)), ((), ())), preferred_element_type=F32)


def _dot_f32(a, b):
    return jnp.dot(a, b, precision=HIGHEST, preferred_element_type=F32)


def _inproj_kernel(x_ref, g_ref, w_ref, o_ref, h_ref):
    @pl.when(pl.program_id(1) == 0)
    def _():
        h_ref[...] = _rms_rows(x_ref[...], g_ref[...]).astype(BF16)

    o_ref[...] = _dot(h_ref[...], w_ref[...]).astype(o_ref.dtype)


def _inproj(x2, g, w_all, layer):
    t, d = x2.shape
    n = w_all.shape[2]
    tm, tn = _tile(t, 1024), _tile(n, 1024)
    return pl.pallas_call(
        _inproj_kernel,
        grid=(t // tm, n // tn),
        in_specs=[
            pl.BlockSpec((tm, d), lambda i, j: (i, 0)),
            pl.BlockSpec((1, d), lambda i, j: (0, 0)),
            pl.BlockSpec((None, d, tn), lambda i, j: (layer, 0, j)),
        ],
        out_specs=pl.BlockSpec((tm, tn), lambda i, j: (i, j)),
        out_shape=jax.ShapeDtypeStruct((t, n), BF16),
        scratch_shapes=[pltpu.VMEM((tm, d), BF16)],
        compiler_params=_params(("parallel", "arbitrary")),
        name="inproj",
    )(x2, g, w_all)


C_PIECES = 3


def _bf16_pieces(x, n=C_PIECES):
    pieces = []
    for _ in range(n):
        p = x.astype(BF16)
        pieces.append(p)
        x = x - p.astype(F32)
    return pieces


def _side_kernel(x_ref, g_ref, ws_ref, bf_ref, sel_ref, lora_ref, kc_ref, carry_ref):
    @pl.when(pl.program_id(1) == 0)
    def _():
        carry_ref[...] = jnp.zeros_like(carry_ref)

    h = _rms_rows(x_ref[...], g_ref[...]).astype(BF16)
    sf = _dot(h, ws_ref[...])
    ns = lora_ref.shape[1]
    lora_ref[...] = sf[:, :ns]
    z = sf[:, ns:] + bf_ref[...]
    logf = jnp.minimum(z, 0.0) - jnp.log1p(jnp.exp(-jnp.abs(z)))
    ts = logf.shape[0]
    lower = (lax.broadcasted_iota(jnp.int32, (ts, ts), 0)
             >= lax.broadcasted_iota(jnp.int32, (ts, ts), 1)).astype(F32)
    c = _dot_f32(lower, logf) + carry_ref[...]
    carry_ref[...] = c[ts - 1:ts, :]
    kc = None
    for piece, sel in zip(_bf16_pieces(-LOG2E * c), (sel_ref[0], sel_ref[1], sel_ref[2])):
        term = _dot(piece, sel)
        kc = term if kc is None else kc + term
    kc_ref[...] = kc.astype(kc_ref.dtype)


def _forget_lane(head, piece):
    return C_PIECES * head + piece


def _side(x3, g, w_small, b_f, w):
    b, s, d = x3.shape
    ns = w_small.shape[1] - LANES
    ts = _tile(s, 512)
    nh = w // HEAD
    sel = np.zeros((C_PIECES, LANES, w), np.float32)
    for hd in range(nh):
        for piece in range(C_PIECES):
            sel[piece, hd, LANES * (hd // 2) + _forget_lane(hd % 2, piece)] = 1.0
    return pl.pallas_call(
        _side_kernel,
        grid=(b, s // ts),
        in_specs=[
            pl.BlockSpec((None, ts, d), lambda i, j: (i, j, 0)),
            pl.BlockSpec((1, d), lambda i, j: (0, 0)),
            pl.BlockSpec((d, ns + LANES), lambda i, j: (0, 0)),
            pl.BlockSpec((1, LANES), lambda i, j: (0, 0)),
            pl.BlockSpec((C_PIECES, LANES, w), lambda i, j: (0, 0, 0)),
        ],
        out_specs=[
            pl.BlockSpec((None, ts, ns), lambda i, j: (i, j, 0)),
            pl.BlockSpec((None, ts, w), lambda i, j: (i, j, 0)),
        ],
        out_shape=[
            jax.ShapeDtypeStruct((b, s, ns), F32),
            jax.ShapeDtypeStruct((b, s, w), BF16),
        ],
        scratch_shapes=[pltpu.VMEM((1, LANES), F32)],
        compiler_params=_params(("parallel", "arbitrary")),
        name="side",
    )(x3, g, w_small, b_f, jnp.asarray(sel, BF16))


HALO = 8


def _conv_kernel(bg_ref, cg_ref, xv_ref, g_ref, cw_ref, o_ref, pbuf):
    ts = o_ref.shape[0]

    @pl.when(pl.program_id(2) == 0)
    def _():
        pbuf[0:HALO, :] = jnp.zeros((HALO, pbuf.shape[1]), F32)

    p = cg_ref[...].astype(F32) * xv_ref[...].astype(F32)
    pbuf[HALO:HALO + ts, :] = p
    cw = cw_ref[...]
    z = (cw[0:1] * pbuf[HALO - 2:HALO - 2 + ts, :] + cw[1:2] * pbuf[HALO - 1:HALO - 1 + ts, :]
         + cw[2:3] * p)
    o_ref[...] = (bg_ref[...].astype(F32) * z * _silu(g_ref[...].astype(F32))).astype(o_ref.dtype)
    pbuf[0:HALO, :] = pbuf[ts:ts + HALO, :]


def _conv_branch(u3, conv_w, w, col0):
    b, s, _ = u3.shape
    ts, tw = _tile(s, 512), _tile(w, 512)
    nw = w // tw
    c0 = col0 // tw

    def spec(k):
        return pl.BlockSpec((None, ts, tw), lambda i, j, t, k=k: (i, t, c0 + k * nw + j))

    return pl.pallas_call(
        _conv_kernel,
        grid=(b, nw, s // ts),
        in_specs=[spec(0), spec(1), spec(2), spec(3),
                  pl.BlockSpec((conv_w.shape[0], tw), lambda i, j, t: (0, j))],
        out_specs=pl.BlockSpec((None, ts, tw), lambda i, j, t: (i, t, j)),
        out_shape=jax.ShapeDtypeStruct((b, s, w), BF16),
        scratch_shapes=[pltpu.VMEM((HALO + ts, tw), F32)],
        compiler_params=_params(("parallel", "parallel", "arbitrary")),
        name="conv_branch",
    )(u3, u3, u3, u3, conv_w)


POOL_HALO = 16


def _pool_kernel(x_ref, g_ref, pw_ref, sc_ref, o_ref, xbuf):
    ts, w = o_ref.shape
    gw = w // len(POOL_WINDOWS)
    s = pl.program_id(1)

    @pl.when(s == 0)
    def _():
        xbuf[0:POOL_HALO, :] = jnp.zeros((POOL_HALO, w), F32)

    x = x_ref[...].astype(F32)
    xbuf[POOL_HALO:POOL_HALO + ts, :] = x
    pos = s * ts + lax.broadcasted_iota(jnp.int32, (ts, 1), 0)
    for gi, win in enumerate(POOL_WINDOWS):
        lo, hi = gi * gw, (gi + 1) * gw
        xi = x[:, lo:hi]
        acc = xi
        for k in range(1, win):
            acc = acc + xbuf[POOL_HALO - k:POOL_HALO - k + ts, lo:hi]
        count = jnp.minimum(pos + 1, win).astype(F32)
        pooled = acc / count - xi
        y = _dot(pooled.astype(BF16), pw_ref[gi])
        o_ref[:, lo:hi] = (y * sc_ref[:, lo:hi] * _silu(g_ref[:, lo:hi].astype(F32))).astype(o_ref.dtype)
    xbuf[0:POOL_HALO, :] = xbuf[ts:ts + POOL_HALO, :]


def _pool_branch(u3, pool_w, pool_scale, w, col0):
    b, s, _ = u3.shape
    ts = _tile(s, 512)
    c0 = col0 // w
    ng, gw, _ = pool_w.shape
    return pl.pallas_call(
        _pool_kernel,
        grid=(b, s // ts),
        in_specs=[
            pl.BlockSpec((None, ts, w), lambda i, t: (i, t, c0)),
            pl.BlockSpec((None, ts, w), lambda i, t: (i, t, c0 + 1)),
            pl.BlockSpec((ng, gw, gw), lambda i, t: (0, 0, 0)),
            pl.BlockSpec((1, w), lambda i, t: (0, 0)),
        ],
        out_specs=pl.BlockSpec((None, ts, w), lambda i, t: (i, t, 0)),
        out_shape=jax.ShapeDtypeStruct((b, s, w), BF16),
        scratch_shapes=[pltpu.VMEM((POOL_HALO + ts, w), F32)],
        compiler_params=_params(("parallel", "arbitrary")),
        name="pool_branch",
    )(u3, u3, pool_w, pool_scale)


def _rwkv_prep_kernel(r_ref, k_ref, v_ref, lo_ref, mu_ref, mul_ref, w0_ref, w2_ref, a0_ref, a2_ref,
                      kkw_ref, kaw_ref, hb_ref,
                      ro_ref, ko_ref, vo_ref, na_ref, bv_ref, ld_ref, buf, lbuf):
    ts, w = ro_ref.shape

    @pl.when(pl.program_id(1) == 0)
    def _():
        buf[0:HALO, :] = jnp.zeros((HALO, buf.shape[1]), F32)
        lbuf[0:HALO, :] = jnp.zeros((HALO, lbuf.shape[1]), F32)

    for idx, ref in enumerate((r_ref, k_ref, v_ref)):
        buf[HALO:HALO + ts, idx * w:(idx + 1) * w] = ref[...].astype(F32)
    lbuf[HALO:HALO + ts, :] = lo_ref[...]

    def mixed(cur, prev, mu):
        return cur + (prev - cur) * mu

    mu = mu_ref[...]
    r = mixed(buf[HALO:HALO + ts, 0:w], buf[HALO - 1:HALO - 1 + ts, 0:w], mu[0:1])
    k = mixed(buf[HALO:HALO + ts, w:2 * w], buf[HALO - 1:HALO - 1 + ts, w:2 * w], mu[1:2])
    v = mixed(buf[HALO:HALO + ts, 2 * w:3 * w], buf[HALO - 1:HALO - 1 + ts, 2 * w:3 * w], mu[2:3])
    lo = mixed(lbuf[HALO:HALO + ts, :], lbuf[HALO - 1:HALO - 1 + ts, :], mul_ref[...])
    wl, al = lo[:, 0:LORA], lo[:, LORA:2 * LORA]

    z = w0_ref[...] + _dot(jnp.tanh(wl).astype(BF16), w2_ref[...])
    ld_ref[...] = -float(np.exp(-0.5)) * _sigmoid(z)
    a = _sigmoid(a0_ref[...] + _dot(al.astype(BF16), a2_ref[...]))

    kk = k * kkw_ref[...]
    hb = hb_ref[...]
    gw = hb.shape[0]
    kk2 = kk * kk
    n2 = jnp.concatenate([_head_sums(kk2[:, l0:l0 + gw], hb) for l0 in range(0, w, gw)], axis=1)
    kk = kk * lax.rsqrt(jnp.maximum(n2, 1e-24))
    ro_ref[...] = r.astype(ro_ref.dtype)
    ko_ref[...] = (k * (1.0 + (a - 1.0) * kaw_ref[...])).astype(ko_ref.dtype)
    vo_ref[...] = v.astype(vo_ref.dtype)
    na_ref[...] = (-kk).astype(na_ref.dtype)
    bv_ref[...] = (kk * a).astype(bv_ref.dtype)

    buf[0:HALO, :] = buf[ts:ts + HALO, :]
    lbuf[0:HALO, :] = lbuf[ts:ts + HALO, :]


def _rwkv_prep(u3, lora, mu3, mul, w0, w2, a0, a2, kkw, kaw, w, col0):
    b, s, _ = u3.shape
    ts = _tile(s, 256)
    c0 = col0 // w
    gw = _tile(w, MXU_DIM)
    head_of = np.arange(gw) // HEAD
    hb = jnp.asarray(head_of[:, None] == head_of[None, :], BF16)
    row = lambda k: pl.BlockSpec((None, ts, w), lambda i, t, k=k: (i, t, c0 + k))
    full = lambda shape: pl.BlockSpec(shape, lambda i, t: (0,) * len(shape))
    outw = pl.BlockSpec((None, ts, w), lambda i, t: (i, t, 0))
    sds = lambda dt: jax.ShapeDtypeStruct((b, s, w), dt)
    return pl.pallas_call(
        _rwkv_prep_kernel,
        grid=(b, s // ts),
        in_specs=[row(0), row(1), row(2),
                  pl.BlockSpec((None, ts, 2 * LORA), lambda i, t: (i, t, 0)),
                  full((3, w)), full((1, 2 * LORA)), full((1, w)), full((LORA, w)),
                  full((1, w)), full((LORA, w)), full((1, w)), full((1, w)),
                  full((gw, gw))],
        out_specs=[outw] * 6,
        out_shape=[sds(BF16)] * 5 + [sds(F32)],
        scratch_shapes=[pltpu.VMEM((HALO + ts, 3 * w), F32), pltpu.VMEM((HALO + ts, 2 * LORA), F32)],
        compiler_params=_params(("parallel", "arbitrary")),
        name="rwkv_prep",
    )(u3, u3, u3, lora, mu3, mul, w0, w2, a0, a2, kkw, kaw, hb)


def _head_stack(x, lane_head):
    zero = jnp.zeros_like(x)
    return jnp.concatenate([jnp.where(lane_head == 0, x, zero), jnp.where(lane_head == 1, x, zero)], axis=0)


SUM_PIECES = 2


def _rwkv_scan_kernel(r_ref, k_ref, v_ref, na_ref, bv_ref, ld_ref, g_ref, lng_ref, lnb_ref, rk_ref,
                      cl_ref, cu_ref, hb_ref, o_ref, state, ybuf, dec_s, rh_s, ah_s, bt_s, kt_s, bb_s, kb_s):
    ts, width = o_ref.shape
    n_pairs = width // LANES
    c = RWKV_CHUNK
    n_chunks = ts // c

    @pl.when(pl.program_id(2) == 0)
    def _():
        state[...] = jnp.zeros_like(state)

    sb = cl_ref.shape[0]
    lcs, lrests = [], []
    for r0 in range(0, ts, sb):
        pieces = jnp.concatenate(_bf16_pieces(ld_ref[r0:r0 + sb, :], SUM_PIECES), axis=1)
        for m_ref, outs in ((cl_ref, lcs), (cu_ref, lrests)):
            both = _dot(m_ref[...], pieces)
            outs.append(both[:, 0:width] + both[:, width:2 * width])
    ld = ld_ref[...]
    lc = jnp.concatenate(lcs, axis=0)
    lrest = jnp.concatenate(lrests, axis=0)
    e_in = jnp.exp(lc)
    e_neg = jnp.exp(-lc)
    e_end = jnp.exp(lrest)
    r_all = r_ref[...].astype(F32)
    k_all = k_ref[...].astype(F32)
    na_all = na_ref[...].astype(F32)
    bv_all = bv_ref[...].astype(F32)
    dec_s[...] = e_in
    rh_s[...] = (r_all * e_in).astype(BF16)
    ah_s[...] = (na_all * jnp.exp(lc - ld)).astype(BF16)
    bt_s[...] = (bv_all * e_neg).astype(BF16)
    kt_s[...] = (k_all * e_neg).astype(BF16)
    bb_s[...] = (bv_all * e_end).astype(BF16)
    kb_s[...] = (k_all * e_end).astype(BF16)

    lane_head = lax.broadcasted_iota(jnp.int32, (1, LANES), 1) // HEAD
    t_idx = lax.broadcasted_iota(jnp.int32, (c, LANES), 0)
    s_idx = lax.broadcasted_iota(jnp.int32, (c, LANES), 1) % c
    strict = s_idx < t_idx
    incl = s_idx <= t_idx
    blk = (lax.broadcasted_iota(jnp.int32, (LANES, LANES), 0) // HEAD
           == lax.broadcasted_iota(jnp.int32, (LANES, LANES), 1) // HEAD)
    eye_cat = (s_idx == t_idx).astype(F32)

    def chunk(ci, states):
        rows = pl.ds(pl.multiple_of(ci * c, c), c)
        tail_rows = pl.ds(pl.multiple_of(ci * c + c - 8, 8), 8)
        pairs = range(n_pairs)
        lanes = [slice(pi * LANES, (pi + 1) * LANES) for pi in pairs]
        each = lambda f: [f(pi) for pi in pairs]
        v = each(lambda pi: v_ref[rows, lanes[pi]])
        rh = each(lambda pi: rh_s[rows, lanes[pi]])
        ah = each(lambda pi: ah_s[rows, lanes[pi]])
        sc = each(lambda pi: _dot_nt(
            jnp.concatenate([ah[pi], rh[pi]], axis=0),
            jnp.concatenate([_head_stack(bt_s[rows, lanes[pi]], lane_head),
                             _head_stack(kt_s[rows, lanes[pi]], lane_head)], axis=0)))
        a_ak = each(lambda pi: jnp.where(strict, sc[pi][0:c, LANES:2 * LANES], 0.0).astype(BF16))
        a_rb = each(lambda pi: jnp.where(incl, sc[pi][c:2 * c, 0:LANES], 0.0).astype(BF16))
        a_rk = each(lambda pi: jnp.where(incl, sc[pi][c:2 * c, LANES:2 * LANES], 0.0).astype(BF16))

        x = each(lambda pi: jnp.where(strict, sc[pi][0:c, 0:LANES], 0.0))
        tinv = each(lambda pi: eye_cat + x[pi])
        p2 = 2
        while p2 < c:
            xb = each(lambda pi: x[pi].astype(BF16))
            x = each(lambda pi: _dot(xb[pi], _head_stack(xb[pi], lane_head)))
            tinv = each(lambda pi: tinv[pi] + _dot(tinv[pi].astype(BF16),
                                                   _head_stack(x[pi].astype(BF16), lane_head)))
            p2 *= 2
        t_cat = each(lambda pi: tinv[pi].astype(BF16))

        g2b = each(lambda pi: states[pi].astype(BF16))
        vs = each(lambda pi: _head_stack(v[pi], lane_head))
        p = each(lambda pi: _dot_nt(ah[pi], g2b[pi]) + _dot(a_ak[pi], vs[pi]))
        ub = each(lambda pi: _dot(t_cat[pi], _head_stack(p[pi].astype(BF16), lane_head)).astype(BF16))
        new_states = []
        for pi in pairs:
            y = (_dot_nt(rh[pi], g2b[pi]) + _dot(a_rb[pi], _head_stack(ub[pi], lane_head))
                 + _dot(a_rk[pi], vs[pi]))
            ybuf[rows, lanes[pi]] = y
            uv_t = jnp.concatenate([ub[pi], v[pi]], axis=0).T
            upd = _dot(uv_t, jnp.concatenate([bb_s[rows, lanes[pi]], kb_s[rows, lanes[pi]]], axis=0))
            decay = dec_s[tail_rows, lanes[pi]][7:8, :]
            new_states.append(states[pi] * decay + jnp.where(blk, upd, 0.0))
        return tuple(new_states)

    final = lax.fori_loop(0, n_chunks, chunk, tuple(state[pi] for pi in range(n_pairs)))
    for pi in range(n_pairs):
        state[pi] = final[pi]

    hb = hb_ref[...]
    gw = hb.shape[0]
    for l0 in range(0, width, gw):
        lanes = slice(l0, l0 + gw)
        y = ybuf[:, lanes]
        mean = _head_sums(y, hb) * (1.0 / HEAD)
        d = y - mean
        var = _head_sums(d * d, hb) * (1.0 / HEAD)
        yn = d * lax.rsqrt(var + RWKV_LN_EPS) * lng_ref[:, lanes] + lnb_ref[:, lanes]
        rk = r_ref[:, lanes].astype(F32) * k_ref[:, lanes].astype(F32) * rk_ref[:, lanes]
        bonus = _head_sums(rk, hb) * v_ref[:, lanes].astype(F32)
        o_ref[:, lanes] = ((yn + bonus) * _silu(g_ref[:, lanes].astype(F32))).astype(o_ref.dtype)


def _head_sums(x, m):
    return functools.reduce(jnp.add, [_dot(p, m) for p in _bf16_pieces(x, SUM_PIECES)])


RWKV_PAIRS = 8


def _rwkv_scan(prep, u3, ln_g, ln_b, r_k, w, gate_col0):
    r, k, v, na, bv, ld = prep
    b, s, _ = r.shape
    ts = _tile(s, 512)
    width = LANES * _tile(w // LANES, RWKV_PAIRS)
    gc0 = gate_col0 // width
    sb = _tile(ts, MXU_DIM)
    chunk_of = np.arange(sb) // RWKV_CHUNK
    same = chunk_of[:, None] == chunk_of[None, :]
    tri = np.arange(sb)[:, None] >= np.arange(sb)[None, :]
    cl = jnp.asarray(same & tri, BF16)
    cu = jnp.asarray(same & ~tri, BF16)
    gw = _tile(width, MXU_DIM)
    head_of = np.arange(gw) // HEAD
    hb = jnp.asarray(head_of[:, None] == head_of[None, :], BF16)
    row = pl.BlockSpec((None, ts, width), lambda i, p, t: (i, t, p))
    vec = pl.BlockSpec((1, width), lambda i, p, t: (0, p))
    const = lambda n: pl.BlockSpec((n, n), lambda i, p, t: (0, 0))
    tile_bf16 = pltpu.VMEM((ts, width), BF16)
    return pl.pallas_call(
        _rwkv_scan_kernel,
        grid=(b, w // width, s // ts),
        in_specs=[row] * 6 + [pl.BlockSpec((None, ts, width), lambda i, p, t: (i, t, gc0 + p)),
                              vec, vec, vec, const(sb), const(sb), const(gw)],
        out_specs=row,
        out_shape=jax.ShapeDtypeStruct((b, s, w), BF16),
        scratch_shapes=[pltpu.VMEM((width // LANES, LANES, LANES), F32), pltpu.VMEM((ts, width), F32),
                        pltpu.VMEM((ts, width), F32)] + [tile_bf16] * 6,
        compiler_params=_params(("parallel", "parallel", "arbitrary")),
        name="rwkv_scan",
    )(r, k, v, na, bv, ld, u3, ln_g, ln_b, r_k, cl, cu, hb)


FOX_TQ = 1024
FOX_TK = 1024
FOX_SUB = 16
FOX_ACC_ROWS = HEAD + 16


def _fold8(x, op):
    parts = [x[i:i + 8, :] for i in range(0, x.shape[0], 8)]
    return functools.reduce(op, parts)


def _fox_kernel(qi_ref, kj_ref, q_ref, k_ref, v_ref, kc_ref, g_ref, o_ref,
                m_ref, acc_ref, qx_ref, s0_ref, s1_ref, p0_ref, p1_ref):
    step = pl.program_id(2)
    qi, kj = qi_ref[step], kj_ref[step]
    tq, tk = q_ref.shape[0], k_ref.shape[0]
    lane = lax.broadcasted_iota(jnp.int32, (1, LANES), 1)

    @pl.when(kj == 0)
    def _():
        m_ref[...] = jnp.full_like(m_ref, NEG_BIG)
        acc_ref[...] = jnp.zeros_like(acc_ref)
        q2 = q_ref[...]
        for h in range(2):
            first = _forget_lane(h, 0)
            ones = jnp.where((lane >= first) & (lane < first + C_PIECES), 1.0, 0.0).astype(BF16)
            qx_ref[h, :, 0:LANES] = jnp.where(lane // HEAD == h, q2, jnp.zeros_like(q2))
            qx_ref[h, :, LANES:2 * LANES] = jnp.broadcast_to(ones, q2.shape)

    offset = qi * tq - kj * tk

    s_refs, p_refs = (s0_ref, s1_ref), (p0_ref, p1_ref)

    def step_body(masked):
        kx = jnp.concatenate([k_ref[...], kc_ref[...]], axis=1)
        for h in range(2):
            s_refs[h][...] = _dot_nt(kx, qx_ref[h])
        if masked:
            key_minus_query = (lax.broadcasted_iota(jnp.int32, (FOX_SUB, tq), 0)
                               - lax.broadcasted_iota(jnp.int32, (FOX_SUB, tq), 1))

        def scores(h, kb):
            blk = s_refs[h][kb * FOX_SUB:(kb + 1) * FOX_SUB, :]
            if masked:
                blk = jnp.where(key_minus_query <= offset - kb * FOX_SUB, blk, NEG_BIG)
            return blk

        def running_max(h):
            mx = None
            for kb in range(tk // FOX_SUB):
                part = _fold8(scores(h, kb), jnp.maximum)
                mx = part if mx is None else jnp.maximum(mx, part)
            m_prev = m_ref[h]
            m_new = jnp.maximum(m_prev, jnp.max(mx, axis=0, keepdims=True))
            m_ref[h] = m_new
            return m_new, jnp.exp2(m_prev - m_new)

        def probabilities(h, m_new):
            for kb in range(tk // FOX_SUB):
                p_refs[h][kb * FOX_SUB:(kb + 1) * FOX_SUB, :] = jnp.exp2(scores(h, kb) - m_new).astype(BF16)

        def accumulate(h, alpha):
            lhs = jnp.concatenate([v_t[h * HEAD:(h + 1) * HEAD, :], sum_rows], axis=0)
            acc_ref[h] = alpha * acc_ref[h] + _dot(lhs, p_refs[h][...])

        eye = (lax.broadcasted_iota(jnp.int32, (LANES, LANES), 0)
               == lax.broadcasted_iota(jnp.int32, (LANES, LANES), 1)).astype(BF16)
        v_t = _dot_nt(eye, v_ref[...]).astype(BF16)
        sum_rows = jnp.ones((FOX_ACC_ROWS - HEAD, tk), BF16)
        m0, alpha0 = running_max(0)
        probabilities(0, m0)
        m1, alpha1 = running_max(1)
        accumulate(0, alpha0)
        probabilities(1, m1)
        accumulate(1, alpha1)

    crosses_diagonal = (kj + 1) * tk - 1 > qi * tq

    @pl.when(jnp.logical_not(crosses_diagonal))
    def _():
        step_body(False)

    @pl.when(crosses_diagonal)
    def _():
        step_body(True)

    @pl.when((kj + 1) * tk == (qi + 1) * tq)
    def _():
        outs = [acc_ref[h, 0:HEAD, :] * (1.0 / acc_ref[h, HEAD:HEAD + 1, :]) for h in range(2)]
        out = jnp.concatenate(outs, axis=0).T
        o_ref[...] = (out * _silu(g_ref[...].astype(F32))).astype(o_ref.dtype)


def _fox_branch(u3, kc, w, q_col0, gate_col0):
    b, s, _ = u3.shape
    tq, tk = _tile(s, FOX_TQ), _tile(s, FOX_TK)
    n_pairs = w // LANES
    pairs = [(i, j) for i in range(s // tq) for j in range((i + 1) * tq // tk)]
    qi = jnp.asarray([p[0] for p in pairs], jnp.int32)
    kj = jnp.asarray([p[1] for p in pairs], jnp.int32)
    qc0, gc0 = q_col0 // LANES, gate_col0 // LANES
    grid_spec = pltpu.PrefetchScalarGridSpec(
        num_scalar_prefetch=2,
        grid=(b, n_pairs, len(pairs)),
        in_specs=[
            pl.BlockSpec((None, tq, LANES), lambda i, p, t, qi, kj: (i, qi[t], qc0 + p)),
            pl.BlockSpec((None, tk, LANES), lambda i, p, t, qi, kj: (i, kj[t], qc0 + n_pairs + p)),
            pl.BlockSpec((None, tk, LANES), lambda i, p, t, qi, kj: (i, kj[t], qc0 + 2 * n_pairs + p)),
            pl.BlockSpec((None, tk, LANES), lambda i, p, t, qi, kj: (i, kj[t], p)),
            pl.BlockSpec((None, tq, LANES), lambda i, p, t, qi, kj: (i, qi[t], gc0 + p)),
        ],
        out_specs=pl.BlockSpec((None, tq, LANES), lambda i, p, t, qi, kj: (i, qi[t], p)),
        scratch_shapes=[pltpu.VMEM((2, 1, tq), F32),
                        pltpu.VMEM((2, FOX_ACC_ROWS, tq), F32), pltpu.VMEM((2, tq, 2 * LANES), BF16),
                        pltpu.VMEM((tk, tq), F32), pltpu.VMEM((tk, tq), F32),
                        pltpu.VMEM((tk, tq), BF16), pltpu.VMEM((tk, tq), BF16)],
    )
    return pl.pallas_call(
        _fox_kernel,
        grid_spec=grid_spec,
        out_shape=jax.ShapeDtypeStruct((b, s, w), BF16),
        compiler_params=_params(("parallel", "parallel", "arbitrary")),
        name="fox_attention",
    )(qi, kj, u3, u3, u3, kc, u3)


def _merge_kernel(ya_ref, yb_ref, yc_ref, yd_ref, ma_ref, mb_ref, mc_ref, md_ref, bm_ref, wb_ref, o_ref):
    acc = None
    branches = ((ya_ref, ma_ref), (yb_ref, mb_ref), (yc_ref, mc_ref), (yd_ref, md_ref))
    for kbr, (y_ref, ml_ref) in enumerate(branches):
        proj = _dot(y_ref[...], wb_ref[kbr])
        gate = _sigmoid(ml_ref[...].astype(F32) + bm_ref[kbr:kbr + 1, :])
        acc = gate * proj if acc is None else acc + gate * proj
    o_ref[...] = acc.astype(o_ref.dtype)


def _merge(ys, u2, b_merge, w_branch_all, layer, ml_col0):
    t, w = ys[0].shape
    _, nb, _, d = w_branch_all.shape
    tm, tn = _tile(t, 1024), _tile(d, 512)
    nd = d // tn
    mc0 = ml_col0 // tn
    yspec = pl.BlockSpec((tm, w), lambda i, j: (i, 0))
    mspec = lambda k: pl.BlockSpec((tm, tn), lambda i, j, k=k: (i, mc0 + k * nd + j))
    return pl.pallas_call(
        _merge_kernel,
        grid=(t // tm, nd),
        in_specs=[yspec] * 4 + [mspec(k) for k in range(4)] + [
            pl.BlockSpec((nb, tn), lambda i, j: (0, j)),
            pl.BlockSpec((None, nb, w, tn), lambda i, j: (layer, 0, 0, j)),
        ],
        out_specs=pl.BlockSpec((tm, tn), lambda i, j: (i, j)),
        out_shape=jax.ShapeDtypeStruct((t, d), BF16),
        compiler_params=_params(("parallel", "arbitrary")),
        name="merge",
    )(*ys, u2, u2, u2, u2, b_merge, w_branch_all)


def _outproj_kernel(m_ref, w_ref, x_ref, o_ref):
    o_ref[...] = x_ref[...] + _dot(m_ref[...], w_ref[...])


def _outproj(merged, w_out_all, layer, x2):
    t, d = x2.shape
    tm, tn = _tile(t, 1024), _tile(d, 512)
    return pl.pallas_call(
        _outproj_kernel,
        grid=(t // tm, d // tn),
        in_specs=[
            pl.BlockSpec((tm, d), lambda i, j: (i, 0)),
            pl.BlockSpec((None, d, tn), lambda i, j: (layer, 0, j)),
            pl.BlockSpec((tm, tn), lambda i, j: (i, j)),
        ],
        out_specs=pl.BlockSpec((tm, tn), lambda i, j: (i, j)),
        out_shape=jax.ShapeDtypeStruct((t, d), F32),
        compiler_params=_params(("parallel", "arbitrary")),
        name="outproj",
    )(merged, w_out_all, x2)


def _final_norm_kernel(x_ref, g_ref, o_ref):
    o_ref[...] = _rms_rows(x_ref[...], g_ref[...])


def _final_norm(x2, g):
    t, d = x2.shape
    tm = _tile(t, 512)
    return pl.pallas_call(
        _final_norm_kernel,
        grid=(t // tm,),
        in_specs=[pl.BlockSpec((tm, d), lambda i: (i, 0)), pl.BlockSpec((1, d), lambda i: (0, 0))],
        out_specs=pl.BlockSpec((tm, d), lambda i: (i, 0)),
        out_shape=jax.ShapeDtypeStruct((t, d), F32),
        compiler_params=_params(("parallel",)),
        name="final_norm",
    )(x2, g)


def _split_in_weights(w_in, w, nh):
    lo0, q0, f0 = 7 * w, 8 * w + 2 * LORA, 11 * w + 2 * LORA
    w_main = jnp.concatenate([w_in[..., :lo0], w_in[..., lo0 + 2 * LORA:q0],
                              w_in[..., q0:q0 + w] * (HEAD ** -0.5 * LOG2E),
                              w_in[..., q0 + w:f0], w_in[..., f0 + nh:]], axis=-1).astype(BF16)
    pad = jnp.zeros(w_in.shape[:-1] + (LANES - nh,), w_in.dtype)
    w_small = jnp.concatenate([w_in[..., lo0:lo0 + 2 * LORA], w_in[..., f0:f0 + nh], pad], axis=-1).astype(BF16)
    return w_main, w_small


def _layer(x2, bsz, seq, layer, w_main_all, w_small, w_branch_all, w_out_all, norm_g, b_merge, conv_w, rwkv_mu,
           rwkv_w0, rwkv_w2, rwkv_a0, rwkv_a2, rwkv_kk, rwkv_ka, rwkv_rk, rwkv_ln_g, rwkv_ln_b, fox_bf, pool_w,
           pool_scale):
    t, d = x2.shape
    w = conv_w.shape[1]
    nh = fox_bf.shape[0]
    b_f = jnp.concatenate([fox_bf, jnp.zeros((LANES - nh,), fox_bf.dtype)]).reshape(1, LANES)
    row = lambda a: a.reshape(1, -1)

    u2 = _inproj(x2, row(norm_g), w_main_all, layer)
    u3 = u2.reshape(bsz, seq, -1)
    lora, kc = _side(x2.reshape(bsz, seq, d), row(norm_g), w_small, b_f, w)
    y_a = _conv_branch(u3, conv_w, w, 0)
    prep = _rwkv_prep(u3, lora, rwkv_mu[:3 * w].reshape(3, w), row(rwkv_mu[3 * w:]), row(rwkv_w0),
                      rwkv_w2.astype(BF16), row(rwkv_a0), rwkv_a2.astype(BF16), row(rwkv_kk), row(rwkv_ka),
                      w, 4 * w)
    y_b = _rwkv_scan(prep, u3, row(rwkv_ln_g), row(rwkv_ln_b), row(rwkv_rk), w, 7 * w)
    y_c = _fox_branch(u3, kc, w, 8 * w, 11 * w)
    y_d = _pool_branch(u3, pool_w.astype(BF16), row(pool_scale), w, 12 * w)
    ys = [y.reshape(t, w) for y in (y_a, y_b, y_c, y_d)]
    merged = _merge(ys, u2, b_merge, w_branch_all, layer, 14 * w)
    return _outproj(merged, w_out_all, layer, x2)


def kernel(x, norm_g, w_in, b_merge, conv_w, rwkv_mu, rwkv_w0, rwkv_w2, rwkv_a0, rwkv_a2, rwkv_kk, rwkv_ka,
           rwkv_rk, rwkv_ln_g, rwkv_ln_b, fox_bf, pool_w, pool_scale, w_branch, w_out, final_g):
    bsz, seq, d = x.shape
    x2 = x.reshape(bsz * seq, d)
    w_main_all, w_small_all = _split_in_weights(w_in, conv_w.shape[2], fox_bf.shape[1])
    w_branch_all = w_branch.astype(BF16)
    w_out_all = w_out.astype(BF16)
    for l in range(norm_g.shape[0]):
        x2 = _layer(x2, bsz, seq, l, w_main_all, w_small_all[l], w_branch_all, w_out_all, norm_g[l], b_merge[l],
                    conv_w[l], rwkv_mu[l], rwkv_w0[l], rwkv_w2[l], rwkv_a0[l], rwkv_a2[l], rwkv_kk[l], rwkv_ka[l],
                    rwkv_rk[l], rwkv_ln_g[l], rwkv_ln_b[l], fox_bf[l], pool_w[l], pool_scale[l])
    return _final_norm(x2, final_g.reshape(1, d)).reshape(bsz, seq, d)
```

```python
import functools

import jax
import jax.numpy as jnp
import numpy as np
from jax import lax
from jax.experimental import pallas as pl
from jax.experimental.pallas import tpu as pltpu

F32 = jnp.float32
BF16 = jnp.bfloat16
HIGHEST = lax.Precision.HIGHEST

NORM_EPS = 1e-6
RWKV_LN_EPS = 64e-5
HEAD = 64
LORA = 64
POOL_WINDOWS = (2, 4, 8, 16)
RWKV_CHUNK = 64
LANES = 128
MXU_DIM = 256
LOG2E = float(np.log2(np.e))
NEG_BIG = -1e30
VMEM_LIMIT = 48 * 1024 * 1024


def _params(sem):
    return pltpu.CompilerParams(dimension_semantics=sem, vmem_limit_bytes=VMEM_LIMIT)


def _tile(n, want):
    t = min(n, want)
    while n % t:
        t //= 2
    return t


def _sigmoid(x):
    return 1.0 / (1.0 + jnp.exp(-x))


def _silu(x):
    return x * _sigmoid(x)


def _rms_rows(xf, g):
    ms = jnp.mean(xf * xf, axis=-1, keepdims=True)
    return xf * lax.rsqrt(ms + NORM_EPS) * g


def _dot(a, b):
    return jnp.dot(a, b, preferred_element_type=F32)


def _dot_nt(a, b):
    return lax.dot_general(a, b, (((1,), (1,)), ((), ())), preferred_element_type=F32)


def _dot_f32(a, b):
    return jnp.dot(a, b, precision=HIGHEST, preferred_element_type=F32)


def _inproj_kernel(x_ref, g_ref, w_ref, o_ref, h_ref):
    @pl.when(pl.program_id(1) == 0)
    def _():
        h_ref[...] = _rms_rows(x_ref[...], g_ref[...]).astype(BF16)

    o_ref[...] = _dot(h_ref[...], w_ref[...]).astype(o_ref.dtype)


def _inproj(x2, g, w_all, layer):
    t, d = x2.shape
    n = w_all.shape[2]
    tm, tn = _tile(t, 1024), _tile(n, 1024)
    return pl.pallas_call(
        _inproj_kernel,
        grid=(t // tm, n // tn),
        in_specs=[
            pl.BlockSpec((tm, d), lambda i, j: (i, 0)),
            pl.BlockSpec((1, d), lambda i, j: (0, 0)),
            pl.BlockSpec((None, d, tn), lambda i, j: (layer, 0, j)),
        ],
        out_specs=pl.BlockSpec((tm, tn), lambda i, j: (i, j)),
        out_shape=jax.ShapeDtypeStruct((t, n), BF16),
        scratch_shapes=[pltpu.VMEM((tm, d), BF16)],
        compiler_params=_params(("parallel", "arbitrary")),
        name="inproj",
    )(x2, g, w_all)


C_PIECES = 3


def _bf16_pieces(x, n=C_PIECES):
    pieces = []
    for _ in range(n):
        p = x.astype(BF16)
        pieces.append(p)
        x = x - p.astype(F32)
    return pieces


def _side_kernel(x_ref, g_ref, ws_ref, bf_ref, sel_ref, lora_ref, kc_ref, carry_ref):
    @pl.when(pl.program_id(1) == 0)
    def _():
        carry_ref[...] = jnp.zeros_like(carry_ref)

    h = _rms_rows(x_ref[...], g_ref[...]).astype(BF16)
    sf = _dot(h, ws_ref[...])
    ns = lora_ref.shape[1]
    lora_ref[...] = sf[:, :ns]
    z = sf[:, ns:] + bf_ref[...]
    logf = jnp.minimum(z, 0.0) - jnp.log1p(jnp.exp(-jnp.abs(z)))
    ts = logf.shape[0]
    lower = (lax.broadcasted_iota(jnp.int32, (ts, ts), 0)
             >= lax.broadcasted_iota(jnp.int32, (ts, ts), 1)).astype(F32)
    c = _dot_f32(lower, logf) + carry_ref[...]
    carry_ref[...] = c[ts - 1:ts, :]
    kc = None
    for piece, sel in zip(_bf16_pieces(-LOG2E * c), (sel_ref[0], sel_ref[1], sel_ref[2])):
        term = _dot(piece, sel)
        kc = term if kc is None else kc + term
    kc_ref[...] = kc.astype(kc_ref.dtype)


def _forget_lane(head, piece):
    return C_PIECES * head + piece


def _side(x3, g, w_small, b_f, w):
    b, s, d = x3.shape
    ns = w_small.shape[1] - LANES
    ts = _tile(s, 512)
    nh = w // HEAD
    sel = np.zeros((C_PIECES, LANES, w), np.float32)
    for hd in range(nh):
        for piece in range(C_PIECES):
            sel[piece, hd, LANES * (hd // 2) + _forget_lane(hd % 2, piece)] = 1.0
    return pl.pallas_call(
        _side_kernel,
        grid=(b, s // ts),
        in_specs=[
            pl.BlockSpec((None, ts, d), lambda i, j: (i, j, 0)),
            pl.BlockSpec((1, d), lambda i, j: (0, 0)),
            pl.BlockSpec((d, ns + LANES), lambda i, j: (0, 0)),
            pl.BlockSpec((1, LANES), lambda i, j: (0, 0)),
            pl.BlockSpec((C_PIECES, LANES, w), lambda i, j: (0, 0, 0)),
        ],
        out_specs=[
            pl.BlockSpec((None, ts, ns), lambda i, j: (i, j, 0)),
            pl.BlockSpec((None, ts, w), lambda i, j: (i, j, 0)),
        ],
        out_shape=[
            jax.ShapeDtypeStruct((b, s, ns), F32),
            jax.ShapeDtypeStruct((b, s, w), BF16),
        ],
        scratch_shapes=[pltpu.VMEM((1, LANES), F32)],
        compiler_params=_params(("parallel", "arbitrary")),
        name="side",
    )(x3, g, w_small, b_f, jnp.asarray(sel, BF16))


HALO = 8


def _conv_kernel(bg_ref, cg_ref, xv_ref, g_ref, cw_ref, o_ref, pbuf):
    ts = o_ref.shape[0]

    @pl.when(pl.program_id(2) == 0)
    def _():
        pbuf[0:HALO, :] = jnp.zeros((HALO, pbuf.shape[1]), F32)

    p = cg_ref[...].astype(F32) * xv_ref[...].astype(F32)
    pbuf[HALO:HALO + ts, :] = p
    cw = cw_ref[...]
    z = (cw[0:1] * pbuf[HALO - 2:HALO - 2 + ts, :] + cw[1:2] * pbuf[HALO - 1:HALO - 1 + ts, :]
         + cw[2:3] * p)
    o_ref[...] = (bg_ref[...].astype(F32) * z * _silu(g_ref[...].astype(F32))).astype(o_ref.dtype)
    pbuf[0:HALO, :] = pbuf[ts:ts + HALO, :]


def _conv_branch(u3, conv_w, w, col0):
    b, s, _ = u3.shape
    ts, tw = _tile(s, 512), _tile(w, 512)
    nw = w // tw
    c0 = col0 // tw

    def spec(k):
        return pl.BlockSpec((None, ts, tw), lambda i, j, t, k=k: (i, t, c0 + k * nw + j))

    return pl.pallas_call(
        _conv_kernel,
        grid=(b, nw, s // ts),
        in_specs=[spec(0), spec(1), spec(2), spec(3),
                  pl.BlockSpec((conv_w.shape[0], tw), lambda i, j, t: (0, j))],
        out_specs=pl.BlockSpec((None, ts, tw), lambda i, j, t: (i, t, j)),
        out_shape=jax.ShapeDtypeStruct((b, s, w), BF16),
        scratch_shapes=[pltpu.VMEM((HALO + ts, tw), F32)],
        compiler_params=_params(("parallel", "parallel", "arbitrary")),
        name="conv_branch",
    )(u3, u3, u3, u3, conv_w)


POOL_HALO = 16


def _pool_kernel(x_ref, g_ref, pw_ref, sc_ref, o_ref, xbuf):
    ts, w = o_ref.shape
    gw = w // len(POOL_WINDOWS)
    s = pl.program_id(1)

    @pl.when(s == 0)
    def _():
        xbuf[0:POOL_HALO, :] = jnp.zeros((POOL_HALO, w), F32)

    x = x_ref[...].astype(F32)
    xbuf[POOL_HALO:POOL_HALO + ts, :] = x
    pos = s * ts + lax.broadcasted_iota(jnp.int32, (ts, 1), 0)
    for gi, win in enumerate(POOL_WINDOWS):
        lo, hi = gi * gw, (gi + 1) * gw
        xi = x[:, lo:hi]
        acc = xi
        for k in range(1, win):
            acc = acc + xbuf[POOL_HALO - k:POOL_HALO - k + ts, lo:hi]
        count = jnp.minimum(pos + 1, win).astype(F32)
        pooled = acc / count - xi
        y = _dot(pooled.astype(BF16), pw_ref[gi])
        o_ref[:, lo:hi] = (y * sc_ref[:, lo:hi] * _silu(g_ref[:, lo:hi].astype(F32))).astype(o_ref.dtype)
    xbuf[0:POOL_HALO, :] = xbuf[ts:ts + POOL_HALO, :]


def _pool_branch(u3, pool_w, pool_scale, w, col0):
    b, s, _ = u3.shape
    ts = _tile(s, 512)
    c0 = col0 // w
    ng, gw, _ = pool_w.shape
    return pl.pallas_call(
        _pool_kernel,
        grid=(b, s // ts),
        in_specs=[
            pl.BlockSpec((None, ts, w), lambda i, t: (i, t, c0)),
            pl.BlockSpec((None, ts, w), lambda i, t: (i, t, c0 + 1)),
            pl.BlockSpec((ng, gw, gw), lambda i, t: (0, 0, 0)),
            pl.BlockSpec((1, w), lambda i, t: (0, 0)),
        ],
        out_specs=pl.BlockSpec((None, ts, w), lambda i, t: (i, t, 0)),
        out_shape=jax.ShapeDtypeStruct((b, s, w), BF16),
        scratch_shapes=[pltpu.VMEM((POOL_HALO + ts, w), F32)],
        compiler_params=_params(("parallel", "arbitrary")),
        name="pool_branch",
    )(u3, u3, pool_w, pool_scale)


def _rwkv_prep_kernel(r_ref, k_ref, v_ref, lo_ref, mu_ref, mul_ref, w0_ref, w2_ref, a0_ref, a2_ref,
                      kkw_ref, kaw_ref, hb_ref,
                      ro_ref, ko_ref, vo_ref, na_ref, bv_ref, ld_ref, buf, lbuf):
    ts, w = ro_ref.shape

    @pl.when(pl.program_id(1) == 0)
    def _():
        buf[0:HALO, :] = jnp.zeros((HALO, buf.shape[1]), F32)
        lbuf[0:HALO, :] = jnp.zeros((HALO, lbuf.shape[1]), F32)

    for idx, ref in enumerate((r_ref, k_ref, v_ref)):
        buf[HALO:HALO + ts, idx * w:(idx + 1) * w] = ref[...].astype(F32)
    lbuf[HALO:HALO + ts, :] = lo_ref[...]

    def mixed(cur, prev, mu):
        return cur + (prev - cur) * mu

    mu = mu_ref[...]
    r = mixed(buf[HALO:HALO + ts, 0:w], buf[HALO - 1:HALO - 1 + ts, 0:w], mu[0:1])
    k = mixed(buf[HALO:HALO + ts, w:2 * w], buf[HALO - 1:HALO - 1 + ts, w:2 * w], mu[1:2])
    v = mixed(buf[HALO:HALO + ts, 2 * w:3 * w], buf[HALO - 1:HALO - 1 + ts, 2 * w:3 * w], mu[2:3])
    lo = mixed(lbuf[HALO:HALO + ts, :], lbuf[HALO - 1:HALO - 1 + ts, :], mul_ref[...])
    wl, al = lo[:, 0:LORA], lo[:, LORA:2 * LORA]

    z = w0_ref[...] + _dot(jnp.tanh(wl).astype(BF16), w2_ref[...])
    ld_ref[...] = -float(np.exp(-0.5)) * _sigmoid(z)
    a = _sigmoid(a0_ref[...] + _dot(al.astype(BF16), a2_ref[...]))

    kk = k * kkw_ref[...]
    hb = hb_ref[...]
    gw = hb.shape[0]
    kk2 = kk * kk
    n2 = jnp.concatenate([_head_sums(kk2[:, l0:l0 + gw], hb) for l0 in range(0, w, gw)], axis=1)
    kk = kk * lax.rsqrt(jnp.maximum(n2, 1e-24))
    ro_ref[...] = r.astype(ro_ref.dtype)
    ko_ref[...] = (k * (1.0 + (a - 1.0) * kaw_ref[...])).astype(ko_ref.dtype)
    vo_ref[...] = v.astype(vo_ref.dtype)
    na_ref[...] = (-kk).astype(na_ref.dtype)
    bv_ref[...] = (kk * a).astype(bv_ref.dtype)

    buf[0:HALO, :] = buf[ts:ts + HALO, :]
    lbuf[0:HALO, :] = lbuf[ts:ts + HALO, :]


def _rwkv_prep(u3, lora, mu3, mul, w0, w2, a0, a2, kkw, kaw, w, col0):
    b, s, _ = u3.shape
    ts = _tile(s, 256)
    c0 = col0 // w
    gw = _tile(w, MXU_DIM)
    head_of = np.arange(gw) // HEAD
    hb = jnp.asarray(head_of[:, None] == head_of[None, :], BF16)
    row = lambda k: pl.BlockSpec((None, ts, w), lambda i, t, k=k: (i, t, c0 + k))
    full = lambda shape: pl.BlockSpec(shape, lambda i, t: (0,) * len(shape))
    outw = pl.BlockSpec((None, ts, w), lambda i, t: (i, t, 0))
    sds = lambda dt: jax.ShapeDtypeStruct((b, s, w), dt)
    return pl.pallas_call(
        _rwkv_prep_kernel,
        grid=(b, s // ts),
        in_specs=[row(0), row(1), row(2),
                  pl.BlockSpec((None, ts, 2 * LORA), lambda i, t: (i, t, 0)),
                  full((3, w)), full((1, 2 * LORA)), full((1, w)), full((LORA, w)),
                  full((1, w)), full((LORA, w)), full((1, w)), full((1, w)),
                  full((gw, gw))],
        out_specs=[outw] * 6,
        out_shape=[sds(BF16)] * 5 + [sds(F32)],
        scratch_shapes=[pltpu.VMEM((HALO + ts, 3 * w), F32), pltpu.VMEM((HALO + ts, 2 * LORA), F32)],
        compiler_params=_params(("parallel", "arbitrary")),
        name="rwkv_prep",
    )(u3, u3, u3, lora, mu3, mul, w0, w2, a0, a2, kkw, kaw, hb)


def _head_stack(x, lane_head):
    zero = jnp.zeros_like(x)
    return jnp.concatenate([jnp.where(lane_head == 0, x, zero), jnp.where(lane_head == 1, x, zero)], axis=0)


SUM_PIECES = 2


def _rwkv_scan_kernel(r_ref, k_ref, v_ref, na_ref, bv_ref, ld_ref, g_ref, lng_ref, lnb_ref, rk_ref,
                      cl_ref, cu_ref, hb_ref, o_ref, state, ybuf, dec_s, rh_s, ah_s, bt_s, kt_s, bb_s, kb_s):
    ts, width = o_ref.shape
    n_pairs = width // LANES
    c = RWKV_CHUNK
    n_chunks = ts // c

    @pl.when(pl.program_id(2) == 0)
    def _():
        state[...] = jnp.zeros_like(state)

    sb = cl_ref.shape[0]
    lcs, lrests = [], []
    for r0 in range(0, ts, sb):
        pieces = jnp.concatenate(_bf16_pieces(ld_ref[r0:r0 + sb, :], SUM_PIECES), axis=1)
        for m_ref, outs in ((cl_ref, lcs), (cu_ref, lrests)):
            both = _dot(m_ref[...], pieces)
            outs.append(both[:, 0:width] + both[:, width:2 * width])
    ld = ld_ref[...]
    lc = jnp.concatenate(lcs, axis=0)
    lrest = jnp.concatenate(lrests, axis=0)
    e_in = jnp.exp(lc)
    e_neg = jnp.exp(-lc)
    e_end = jnp.exp(lrest)
    r_all = r_ref[...].astype(F32)
    k_all = k_ref[...].astype(F32)
    na_all = na_ref[...].astype(F32)
    bv_all = bv_ref[...].astype(F32)
    dec_s[...] = e_in
    rh_s[...] = (r_all * e_in).astype(BF16)
    ah_s[...] = (na_all * jnp.exp(lc - ld)).astype(BF16)
    bt_s[...] = (bv_all * e_neg).astype(BF16)
    kt_s[...] = (k_all * e_neg).astype(BF16)
    bb_s[...] = (bv_all * e_end).astype(BF16)
    kb_s[...] = (k_all * e_end).astype(BF16)

    lane_head = lax.broadcasted_iota(jnp.int32, (1, LANES), 1) // HEAD
    t_idx = lax.broadcasted_iota(jnp.int32, (c, LANES), 0)
    s_idx = lax.broadcasted_iota(jnp.int32, (c, LANES), 1) % c
    strict = s_idx < t_idx
    incl = s_idx <= t_idx
    blk = (lax.broadcasted_iota(jnp.int32, (LANES, LANES), 0) // HEAD
           == lax.broadcasted_iota(jnp.int32, (LANES, LANES), 1) // HEAD)
    eye_cat = (s_idx == t_idx).astype(F32)

    def chunk(ci, states):
        rows = pl.ds(pl.multiple_of(ci * c, c), c)
        tail_rows = pl.ds(pl.multiple_of(ci * c + c - 8, 8), 8)
        pairs = range(n_pairs)
        lanes = [slice(pi * LANES, (pi + 1) * LANES) for pi in pairs]
        each = lambda f: [f(pi) for pi in pairs]
        v = each(lambda pi: v_ref[rows, lanes[pi]])
        rh = each(lambda pi: rh_s[rows, lanes[pi]])
        ah = each(lambda pi: ah_s[rows, lanes[pi]])
        sc = each(lambda pi: _dot_nt(
            jnp.concatenate([ah[pi], rh[pi]], axis=0),
            jnp.concatenate([_head_stack(bt_s[rows, lanes[pi]], lane_head),
                             _head_stack(kt_s[rows, lanes[pi]], lane_head)], axis=0)))
        a_ak = each(lambda pi: jnp.where(strict, sc[pi][0:c, LANES:2 * LANES], 0.0).astype(BF16))
        a_rb = each(lambda pi: jnp.where(incl, sc[pi][c:2 * c, 0:LANES], 0.0).astype(BF16))
        a_rk = each(lambda pi: jnp.where(incl, sc[pi][c:2 * c, LANES:2 * LANES], 0.0).astype(BF16))

        x = each(lambda pi: jnp.where(strict, sc[pi][0:c, 0:LANES], 0.0))
        tinv = each(lambda pi: eye_cat + x[pi])
        p2 = 2
        while p2 < c:
            xb = each(lambda pi: x[pi].astype(BF16))
            x = each(lambda pi: _dot(xb[pi], _head_stack(xb[pi], lane_head)))
            tinv = each(lambda pi: tinv[pi] + _dot(tinv[pi].astype(BF16),
                                                   _head_stack(x[pi].astype(BF16), lane_head)))
            p2 *= 2
        t_cat = each(lambda pi: tinv[pi].astype(BF16))

        g2b = each(lambda pi: states[pi].astype(BF16))
        vs = each(lambda pi: _head_stack(v[pi], lane_head))
        p = each(lambda pi: _dot_nt(ah[pi], g2b[pi]) + _dot(a_ak[pi], vs[pi]))
        ub = each(lambda pi: _dot(t_cat[pi], _head_stack(p[pi].astype(BF16), lane_head)).astype(BF16))
        new_states = []
        for pi in pairs:
            y = (_dot_nt(rh[pi], g2b[pi]) + _dot(a_rb[pi], _head_stack(ub[pi], lane_head))
                 + _dot(a_rk[pi], vs[pi]))
            ybuf[rows, lanes[pi]] = y
            uv_t = jnp.concatenate([ub[pi], v[pi]], axis=0).T
            upd = _dot(uv_t, jnp.concatenate([bb_s[rows, lanes[pi]], kb_s[rows, lanes[pi]]], axis=0))
            decay = dec_s[tail_rows, lanes[pi]][7:8, :]
            new_states.append(states[pi] * decay + jnp.where(blk, upd, 0.0))
        return tuple(new_states)

    final = lax.fori_loop(0, n_chunks, chunk, tuple(state[pi] for pi in range(n_pairs)))
    for pi in range(n_pairs):
        state[pi] = final[pi]

    hb = hb_ref[...]
    gw = hb.shape[0]
    for l0 in range(0, width, gw):
        lanes = slice(l0, l0 + gw)
        y = ybuf[:, lanes]
        mean = _head_sums(y, hb) * (1.0 / HEAD)
        d = y - mean
        var = _head_sums(d * d, hb) * (1.0 / HEAD)
        yn = d * lax.rsqrt(var + RWKV_LN_EPS) * lng_ref[:, lanes] + lnb_ref[:, lanes]
        rk = r_ref[:, lanes].astype(F32) * k_ref[:, lanes].astype(F32) * rk_ref[:, lanes]
        bonus = _head_sums(rk, hb) * v_ref[:, lanes].astype(F32)
        o_ref[:, lanes] = ((yn + bonus) * _silu(g_ref[:, lanes].astype(F32))).astype(o_ref.dtype)


def _head_sums(x, m):
    return functools.reduce(jnp.add, [_dot(p, m) for p in _bf16_pieces(x, SUM_PIECES)])


RWKV_PAIRS = 8


def _rwkv_scan(prep, u3, ln_g, ln_b, r_k, w, gate_col0):
    r, k, v, na, bv, ld = prep
    b, s, _ = r.shape
    ts = _tile(s, 512)
    width = LANES * _tile(w // LANES, RWKV_PAIRS)
    gc0 = gate_col0 // width
    sb = _tile(ts, MXU_DIM)
    chunk_of = np.arange(sb) // RWKV_CHUNK
    same = chunk_of[:, None] == chunk_of[None, :]
    tri = np.arange(sb)[:, None] >= np.arange(sb)[None, :]
    cl = jnp.asarray(same & tri, BF16)
    cu = jnp.asarray(same & ~tri, BF16)
    gw = _tile(width, MXU_DIM)
    head_of = np.arange(gw) // HEAD
    hb = jnp.asarray(head_of[:, None] == head_of[None, :], BF16)
    row = pl.BlockSpec((None, ts, width), lambda i, p, t: (i, t, p))
    vec = pl.BlockSpec((1, width), lambda i, p, t: (0, p))
    const = lambda n: pl.BlockSpec((n, n), lambda i, p, t: (0, 0))
    tile_bf16 = pltpu.VMEM((ts, width), BF16)
    return pl.pallas_call(
        _rwkv_scan_kernel,
        grid=(b, w // width, s // ts),
        in_specs=[row] * 6 + [pl.BlockSpec((None, ts, width), lambda i, p, t: (i, t, gc0 + p)),
                              vec, vec, vec, const(sb), const(sb), const(gw)],
        out_specs=row,
        out_shape=jax.ShapeDtypeStruct((b, s, w), BF16),
        scratch_shapes=[pltpu.VMEM((width // LANES, LANES, LANES), F32), pltpu.VMEM((ts, width), F32),
                        pltpu.VMEM((ts, width), F32)] + [tile_bf16] * 6,
        compiler_params=_params(("parallel", "parallel", "arbitrary")),
        name="rwkv_scan",
    )(r, k, v, na, bv, ld, u3, ln_g, ln_b, r_k, cl, cu, hb)


FOX_BLOCK = 1024
FOX_SUB = 16
FOX_ACC_ROWS = HEAD + 16


def _fold8(x, op):
    parts = [x[i:i + 8, :] for i in range(0, x.shape[0], 8)]
    return functools.reduce(op, parts)


def _fox_kernel(qi_ref, kj_ref, q_ref, k_ref, v_ref, kc_ref, g_ref, o_ref,
                m_ref, acc_ref, qx_ref, s0_ref, s1_ref, p0_ref, p1_ref):
    step = pl.program_id(2)
    qi, kj = qi_ref[step], kj_ref[step]
    tq, tk = q_ref.shape[0], k_ref.shape[0]
    lane = lax.broadcasted_iota(jnp.int32, (1, LANES), 1)

    @pl.when(kj == 0)
    def _():
        m_ref[...] = jnp.full_like(m_ref, NEG_BIG)
        acc_ref[...] = jnp.zeros_like(acc_ref)
        q2 = q_ref[...]
        for h in range(2):
            first = _forget_lane(h, 0)
            ones = jnp.where((lane >= first) & (lane < first + C_PIECES), 1.0, 0.0).astype(BF16)
            qx_ref[h, :, 0:LANES] = jnp.where(lane // HEAD == h, q2, jnp.zeros_like(q2))
            qx_ref[h, :, LANES:2 * LANES] = jnp.broadcast_to(ones, q2.shape)

    s_refs, p_refs = (s0_ref, s1_ref), (p0_ref, p1_ref)

    def step_body(masked):
        kx = jnp.concatenate([k_ref[...], kc_ref[...]], axis=1)
        half = tk // 2
        for h in range(2):
            if masked:
                s_refs[h][0:half, :] = _dot_nt(kx[0:half, :], qx_ref[h])
                s_refs[h][half:, half:] = _dot_nt(kx[half:, :], qx_ref[h, half:, :])
            else:
                s_refs[h][...] = _dot_nt(kx, qx_ref[h])
        n_sub, n_grp = tk // FOX_SUB, tq // LANES

        def first_group(kb):
            return (kb * FOX_SUB) // LANES if masked else 0

        def group_scores(h, kb, g):
            blk = s_refs[h][kb * FOX_SUB:(kb + 1) * FOX_SUB, g * LANES:(g + 1) * LANES]
            if masked and g == first_group(kb):
                key = kb * FOX_SUB + lax.broadcasted_iota(jnp.int32, (FOX_SUB, LANES), 0)
                query = g * LANES + lax.broadcasted_iota(jnp.int32, (FOX_SUB, LANES), 1)
                blk = jnp.where(key <= query, blk, NEG_BIG)
            return blk

        def scores(h, kb, g0):
            return jnp.concatenate([group_scores(h, kb, g) for g in range(g0, n_grp)], axis=1)

        def running_max(h):
            mx = [None] * n_grp
            for kb in range(n_sub):
                g0 = first_group(kb)
                part = _fold8(scores(h, kb, g0), jnp.maximum)
                for g in range(g0, n_grp):
                    piece = part[:, (g - g0) * LANES:(g - g0 + 1) * LANES]
                    mx[g] = piece if mx[g] is None else jnp.maximum(mx[g], piece)
            m_prev = m_ref[h]
            m_new = jnp.maximum(m_prev, jnp.max(jnp.concatenate(mx, axis=1), axis=0, keepdims=True))
            m_ref[h] = m_new
            return m_new, jnp.exp2(m_prev - m_new)

        def probabilities(h, m_new):
            for kb in range(n_sub):
                g0 = first_group(kb)
                rows = slice(kb * FOX_SUB, (kb + 1) * FOX_SUB)
                p_refs[h][rows, g0 * LANES:] = jnp.exp2(scores(h, kb, g0) - m_new[:, g0 * LANES:]).astype(BF16)
                z0 = half if kb * FOX_SUB >= half else 0
                if g0 * LANES > z0:
                    p_refs[h][rows, z0:g0 * LANES] = jnp.zeros((FOX_SUB, g0 * LANES - z0), BF16)

        def accumulate(h, alpha):
            lhs = jnp.concatenate([v_t[h * HEAD:(h + 1) * HEAD, :], sum_rows], axis=0)
            if masked:
                acc_ref[h] = alpha * acc_ref[h] + _dot(lhs[:, 0:half], p_refs[h][0:half, :])
                acc_ref[h, :, half:] += _dot(lhs[:, half:], p_refs[h][half:, half:])
            else:
                acc_ref[h] = alpha * acc_ref[h] + _dot(lhs, p_refs[h][...])

        eye = (lax.broadcasted_iota(jnp.int32, (LANES, LANES), 0)
               == lax.broadcasted_iota(jnp.int32, (LANES, LANES), 1)).astype(BF16)
        v_t = _dot_nt(eye, v_ref[...]).astype(BF16)
        sum_rows = jnp.ones((FOX_ACC_ROWS - HEAD, tk), BF16)
        m0, alpha0 = running_max(0)
        probabilities(0, m0)
        m1, alpha1 = running_max(1)
        accumulate(0, alpha0)
        probabilities(1, m1)
        accumulate(1, alpha1)

    @pl.when(kj < qi)
    def _():
        step_body(False)

    @pl.when(kj == qi)
    def _():
        step_body(True)
        outs = [acc_ref[h, 0:HEAD, :] * (1.0 / acc_ref[h, HEAD:HEAD + 1, :]) for h in range(2)]
        out = jnp.concatenate(outs, axis=0).T
        o_ref[...] = (out * _silu(g_ref[...].astype(F32))).astype(o_ref.dtype)


def _fox_branch(u3, kc, w, q_col0, gate_col0):
    b, s, _ = u3.shape
    tq = tk = _tile(s, FOX_BLOCK)
    n_pairs = w // LANES
    pairs = [(i, j) for i in range(s // tq) for j in range(i + 1)]
    qi = jnp.asarray([p[0] for p in pairs], jnp.int32)
    kj = jnp.asarray([p[1] for p in pairs], jnp.int32)
    qc0, gc0 = q_col0 // LANES, gate_col0 // LANES
    grid_spec = pltpu.PrefetchScalarGridSpec(
        num_scalar_prefetch=2,
        grid=(b, n_pairs, len(pairs)),
        in_specs=[
            pl.BlockSpec((None, tq, LANES), lambda i, p, t, qi, kj: (i, qi[t], qc0 + p)),
            pl.BlockSpec((None, tk, LANES), lambda i, p, t, qi, kj: (i, kj[t], qc0 + n_pairs + p)),
            pl.BlockSpec((None, tk, LANES), lambda i, p, t, qi, kj: (i, kj[t], qc0 + 2 * n_pairs + p)),
            pl.BlockSpec((None, tk, LANES), lambda i, p, t, qi, kj: (i, kj[t], p)),
            pl.BlockSpec((None, tq, LANES), lambda i, p, t, qi, kj: (i, qi[t], gc0 + p)),
        ],
        out_specs=pl.BlockSpec((None, tq, LANES), lambda i, p, t, qi, kj: (i, qi[t], p)),
        scratch_shapes=[pltpu.VMEM((2, 1, tq), F32),
                        pltpu.VMEM((2, FOX_ACC_ROWS, tq), F32), pltpu.VMEM((2, tq, 2 * LANES), BF16),
                        pltpu.VMEM((tk, tq), F32), pltpu.VMEM((tk, tq), F32),
                        pltpu.VMEM((tk, tq), BF16), pltpu.VMEM((tk, tq), BF16)],
    )
    return pl.pallas_call(
        _fox_kernel,
        grid_spec=grid_spec,
        out_shape=jax.ShapeDtypeStruct((b, s, w), BF16),
        compiler_params=_params(("parallel", "parallel", "arbitrary")),
        name="fox_attention",
    )(qi, kj, u3, u3, u3, kc, u3)


def _merge_kernel(ya_ref, yb_ref, yc_ref, yd_ref, ma_ref, mb_ref, mc_ref, md_ref, bm_ref, wb_ref, o_ref):
    acc = None
    branches = ((ya_ref, ma_ref), (yb_ref, mb_ref), (yc_ref, mc_ref), (yd_ref, md_ref))
    for kbr, (y_ref, ml_ref) in enumerate(branches):
        proj = _dot(y_ref[...], wb_ref[kbr])
        gate = _sigmoid(ml_ref[...].astype(F32) + bm_ref[kbr:kbr + 1, :])
        acc = gate * proj if acc is None else acc + gate * proj
    o_ref[...] = acc.astype(o_ref.dtype)


def _merge(ys, u2, b_merge, w_branch_all, layer, ml_col0):
    t, w = ys[0].shape
    _, nb, _, d = w_branch_all.shape
    tm, tn = _tile(t, 1024), _tile(d, 512)
    nd = d // tn
    mc0 = ml_col0 // tn
    yspec = pl.BlockSpec((tm, w), lambda i, j: (i, 0))
    mspec = lambda k: pl.BlockSpec((tm, tn), lambda i, j, k=k: (i, mc0 + k * nd + j))
    return pl.pallas_call(
        _merge_kernel,
        grid=(t // tm, nd),
        in_specs=[yspec] * 4 + [mspec(k) for k in range(4)] + [
            pl.BlockSpec((nb, tn), lambda i, j: (0, j)),
            pl.BlockSpec((None, nb, w, tn), lambda i, j: (layer, 0, 0, j)),
        ],
        out_specs=pl.BlockSpec((tm, tn), lambda i, j: (i, j)),
        out_shape=jax.ShapeDtypeStruct((t, d), BF16),
        compiler_params=_params(("parallel", "arbitrary")),
        name="merge",
    )(*ys, u2, u2, u2, u2, b_merge, w_branch_all)


def _outproj_kernel(m_ref, w_ref, x_ref, o_ref):
    o_ref[...] = x_ref[...] + _dot(m_ref[...], w_ref[...])


def _outproj(merged, w_out_all, layer, x2):
    t, d = x2.shape
    tm, tn = _tile(t, 1024), _tile(d, 512)
    return pl.pallas_call(
        _outproj_kernel,
        grid=(t // tm, d // tn),
        in_specs=[
            pl.BlockSpec((tm, d), lambda i, j: (i, 0)),
            pl.BlockSpec((None, d, tn), lambda i, j: (layer, 0, j)),
            pl.BlockSpec((tm, tn), lambda i, j: (i, j)),
        ],
        out_specs=pl.BlockSpec((tm, tn), lambda i, j: (i, j)),
        out_shape=jax.ShapeDtypeStruct((t, d), F32),
        compiler_params=_params(("parallel", "arbitrary")),
        name="outproj",
    )(merged, w_out_all, x2)


def _final_norm_kernel(x_ref, g_ref, o_ref):
    o_ref[...] = _rms_rows(x_ref[...], g_ref[...])


def _final_norm(x2, g):
    t, d = x2.shape
    tm = _tile(t, 512)
    return pl.pallas_call(
        _final_norm_kernel,
        grid=(t // tm,),
        in_specs=[pl.BlockSpec((tm, d), lambda i: (i, 0)), pl.BlockSpec((1, d), lambda i: (0, 0))],
        out_specs=pl.BlockSpec((tm, d), lambda i: (i, 0)),
        out_shape=jax.ShapeDtypeStruct((t, d), F32),
        compiler_params=_params(("parallel",)),
        name="final_norm",
    )(x2, g)


def _split_in_weights(w_in, w, nh):
    lo0, q0, f0 = 7 * w, 8 * w + 2 * LORA, 11 * w + 2 * LORA
    cast = lambda a: a.astype(BF16)
    w_main = jnp.concatenate([cast(w_in[..., :lo0]), cast(w_in[..., lo0 + 2 * LORA:q0]),
                              cast(w_in[..., q0:q0 + w] * (HEAD ** -0.5 * LOG2E)),
                              cast(w_in[..., q0 + w:f0]), cast(w_in[..., f0 + nh:])], axis=-1)
    pad = jnp.zeros(w_in.shape[:-1] + (LANES - nh,), w_in.dtype)
    w_small = jnp.concatenate([w_in[..., lo0:lo0 + 2 * LORA], w_in[..., f0:f0 + nh], pad], axis=-1).astype(BF16)
    return w_main, w_small


def _layer(x2, bsz, seq, layer, w_main_all, w_small, w_branch_all, w_out_all, norm_g, b_merge, conv_w, rwkv_mu,
           rwkv_w0, rwkv_w2, rwkv_a0, rwkv_a2, rwkv_kk, rwkv_ka, rwkv_rk, rwkv_ln_g, rwkv_ln_b, fox_bf, pool_w,
           pool_scale):
    t, d = x2.shape
    w = conv_w.shape[1]
    nh = fox_bf.shape[0]
    b_f = jnp.concatenate([fox_bf, jnp.zeros((LANES - nh,), fox_bf.dtype)]).reshape(1, LANES)
    row = lambda a: a.reshape(1, -1)

    u2 = _inproj(x2, row(norm_g), w_main_all, layer)
    u3 = u2.reshape(bsz, seq, -1)
    lora, kc = _side(x2.reshape(bsz, seq, d), row(norm_g), w_small, b_f, w)
    y_a = _conv_branch(u3, conv_w, w, 0)
    prep = _rwkv_prep(u3, lora, rwkv_mu[:3 * w].reshape(3, w), row(rwkv_mu[3 * w:]), row(rwkv_w0),
                      rwkv_w2.astype(BF16), row(rwkv_a0), rwkv_a2.astype(BF16), row(rwkv_kk), row(rwkv_ka),
                      w, 4 * w)
    y_b = _rwkv_scan(prep, u3, row(rwkv_ln_g), row(rwkv_ln_b), row(rwkv_rk), w, 7 * w)
    y_c = _fox_branch(u3, kc, w, 8 * w, 11 * w)
    y_d = _pool_branch(u3, pool_w.astype(BF16), row(pool_scale), w, 12 * w)
    ys = [y.reshape(t, w) for y in (y_a, y_b, y_c, y_d)]
    merged = _merge(ys, u2, b_merge, w_branch_all, layer, 14 * w)
    return _outproj(merged, w_out_all, layer, x2)


def kernel(x, norm_g, w_in, b_merge, conv_w, rwkv_mu, rwkv_w0, rwkv_w2, rwkv_a0, rwkv_a2, rwkv_kk, rwkv_ka,
           rwkv_rk, rwkv_ln_g, rwkv_ln_b, fox_bf, pool_w, pool_scale, w_branch, w_out, final_g):
    bsz, seq, d = x.shape
    x2 = x.reshape(bsz * seq, d)
    w_main_all, w_small_all = _split_in_weights(w_in, conv_w.shape[2], fox_bf.shape[1])
    w_branch_all = w_branch.astype(BF16)
    w_out_all = w_out.astype(BF16)
    for l in range(norm_g.shape[0]):
        x2 = _layer(x2, bsz, seq, l, w_main_all, w_small_all[l], w_branch_all, w_out_all, norm_g[l], b_merge[l],
                    conv_w[l], rwkv_mu[l], rwkv_w0[l], rwkv_w2[l], rwkv_a0[l], rwkv_a2[l], rwkv_kk[l], rwkv_ka[l],
                    rwkv_rk[l], rwkv_ln_g[l], rwkv_ln_b[l], fox_bf[l], pool_w[l], pool_scale[l])
    return _final_norm(x2, final_g.reshape(1, d)).reshape(bsz, seq, d)
```

```python
import functools

import jax
import jax.numpy as jnp
import numpy as np
from jax import lax
from jax.experimental import pallas as pl
from jax.experimental.pallas import tpu as pltpu

F32 = jnp.float32
BF16 = jnp.bfloat16
HIGHEST = lax.Precision.HIGHEST

NORM_EPS = 1e-6
RWKV_LN_EPS = 64e-5
HEAD = 64
LORA = 64
POOL_WINDOWS = (2, 4, 8, 16)
RWKV_CHUNK = 64
LANES = 128
MXU_DIM = 256
LOG2E = float(np.log2(np.e))
NEG_BIG = -1e30
VMEM_LIMIT = 48 * 1024 * 1024


def _params(sem):
    return pltpu.CompilerParams(dimension_semantics=sem, vmem_limit_bytes=VMEM_LIMIT)


def _tile(n, want):
    t = min(n, want)
    while n % t:
        t //= 2
    return t


def _sigmoid(x):
    return 1.0 / (1.0 + jnp.exp(-x))


def _silu(x):
    return x * _sigmoid(x)


def _rms_rows(xf, g):
    ms = jnp.mean(xf * xf, axis=-1, keepdims=True)
    return xf * lax.rsqrt(ms + NORM_EPS) * g


def _dot(a, b):
    return jnp.dot(a, b, preferred_element_type=F32)


def _dot_nt(a, b):
    return lax.dot_general(a, b, (((1,), (1,)), ((), ())), preferred_element_type=F32)


def _dot_f32(a, b):
    return jnp.dot(a, b, precision=HIGHEST, preferred_element_type=F32)


def _inproj_kernel(x_ref, g_ref, w_ref, o_ref, h_ref):
    @pl.when(pl.program_id(1) == 0)
    def _():
        h_ref[...] = _rms_rows(x_ref[...], g_ref[...]).astype(BF16)

    o_ref[...] = _dot_nt(h_ref[...], w_ref[...]).astype(o_ref.dtype)


def _inproj(x2, g, w_all, layer):
    t, d = x2.shape
    n = w_all.shape[1]
    tm, tn = _tile(t, 1024), _tile(n, 1024)
    return pl.pallas_call(
        _inproj_kernel,
        grid=(t // tm, n // tn),
        in_specs=[
            pl.BlockSpec((tm, d), lambda i, j: (i, 0)),
            pl.BlockSpec((1, d), lambda i, j: (0, 0)),
            pl.BlockSpec((None, tn, d), lambda i, j: (layer, j, 0)),
        ],
        out_specs=pl.BlockSpec((tm, tn), lambda i, j: (i, j)),
        out_shape=jax.ShapeDtypeStruct((t, n), BF16),
        scratch_shapes=[pltpu.VMEM((tm, d), BF16)],
        compiler_params=_params(("parallel", "arbitrary")),
        name="inproj",
    )(x2, g, w_all)


C_PIECES = 3


def _bf16_pieces(x, n=C_PIECES):
    pieces = []
    for _ in range(n):
        p = x.astype(BF16)
        pieces.append(p)
        x = x - p.astype(F32)
    return pieces


def _side_kernel(x_ref, g_ref, ws_ref, bf_ref, sel_ref, lora_ref, kc_ref, carry_ref):
    @pl.when(pl.program_id(1) == 0)
    def _():
        carry_ref[...] = jnp.zeros_like(carry_ref)

    h = _rms_rows(x_ref[...], g_ref[...]).astype(BF16)
    sf = _dot_nt(h, ws_ref[...])
    ns = lora_ref.shape[1]
    lora_ref[...] = sf[:, :ns]
    z = sf[:, ns:] + bf_ref[...]
    logf = jnp.minimum(z, 0.0) - jnp.log1p(jnp.exp(-jnp.abs(z)))
    ts = logf.shape[0]
    lower = (lax.broadcasted_iota(jnp.int32, (ts, ts), 0)
             >= lax.broadcasted_iota(jnp.int32, (ts, ts), 1)).astype(F32)
    c = _dot_f32(lower, logf) + carry_ref[...]
    carry_ref[...] = c[ts - 1:ts, :]
    kc = None
    for piece, sel in zip(_bf16_pieces(-LOG2E * c), (sel_ref[0], sel_ref[1], sel_ref[2])):
        term = _dot(piece, sel)
        kc = term if kc is None else kc + term
    kc_ref[...] = kc.astype(kc_ref.dtype)


def _forget_lane(head, piece):
    return C_PIECES * head + piece


def _side(x3, g, w_small, b_f, w):
    b, s, d = x3.shape
    ns = w_small.shape[0] - LANES
    ts = _tile(s, 512)
    nh = w // HEAD
    sel = np.zeros((C_PIECES, LANES, w), np.float32)
    for hd in range(nh):
        for piece in range(C_PIECES):
            sel[piece, hd, LANES * (hd // 2) + _forget_lane(hd % 2, piece)] = 1.0
    return pl.pallas_call(
        _side_kernel,
        grid=(b, s // ts),
        in_specs=[
            pl.BlockSpec((None, ts, d), lambda i, j: (i, j, 0)),
            pl.BlockSpec((1, d), lambda i, j: (0, 0)),
            pl.BlockSpec((ns + LANES, d), lambda i, j: (0, 0)),
            pl.BlockSpec((1, LANES), lambda i, j: (0, 0)),
            pl.BlockSpec((C_PIECES, LANES, w), lambda i, j: (0, 0, 0)),
        ],
        out_specs=[
            pl.BlockSpec((None, ts, ns), lambda i, j: (i, j, 0)),
            pl.BlockSpec((None, ts, w), lambda i, j: (i, j, 0)),
        ],
        out_shape=[
            jax.ShapeDtypeStruct((b, s, ns), F32),
            jax.ShapeDtypeStruct((b, s, w), BF16),
        ],
        scratch_shapes=[pltpu.VMEM((1, LANES), F32)],
        compiler_params=_params(("parallel", "arbitrary")),
        name="side",
    )(x3, g, w_small, b_f, jnp.asarray(sel, BF16))


HALO = 8


def _conv_kernel(bg_ref, cg_ref, xv_ref, g_ref, cw_ref, o_ref, pbuf):
    ts = o_ref.shape[0]

    @pl.when(pl.program_id(2) == 0)
    def _():
        pbuf[0:HALO, :] = jnp.zeros((HALO, pbuf.shape[1]), F32)

    p = cg_ref[...].astype(F32) * xv_ref[...].astype(F32)
    pbuf[HALO:HALO + ts, :] = p
    cw = cw_ref[...]
    z = (cw[0:1] * pbuf[HALO - 2:HALO - 2 + ts, :] + cw[1:2] * pbuf[HALO - 1:HALO - 1 + ts, :]
         + cw[2:3] * p)
    o_ref[...] = (bg_ref[...].astype(F32) * z * _silu(g_ref[...].astype(F32))).astype(o_ref.dtype)
    pbuf[0:HALO, :] = pbuf[ts:ts + HALO, :]


def _conv_branch(u3, conv_w, w, col0):
    b, s, _ = u3.shape
    ts, tw = _tile(s, 512), _tile(w, 512)
    nw = w // tw
    c0 = col0 // tw

    def spec(k):
        return pl.BlockSpec((None, ts, tw), lambda i, j, t, k=k: (i, t, c0 + k * nw + j))

    return pl.pallas_call(
        _conv_kernel,
        grid=(b, nw, s // ts),
        in_specs=[spec(0), spec(1), spec(2), spec(3),
                  pl.BlockSpec((conv_w.shape[0], tw), lambda i, j, t: (0, j))],
        out_specs=pl.BlockSpec((None, ts, tw), lambda i, j, t: (i, t, j)),
        out_shape=jax.ShapeDtypeStruct((b, s, w), BF16),
        scratch_shapes=[pltpu.VMEM((HALO + ts, tw), F32)],
        compiler_params=_params(("parallel", "parallel", "arbitrary")),
        name="conv_branch",
    )(u3, u3, u3, u3, conv_w)


POOL_HALO = 16


def _pool_kernel(x_ref, g_ref, pw_ref, sc_ref, o_ref, xbuf):
    ts, w = o_ref.shape
    gw = w // len(POOL_WINDOWS)
    s = pl.program_id(1)

    @pl.when(s == 0)
    def _():
        xbuf[0:POOL_HALO, :] = jnp.zeros((POOL_HALO, w), F32)

    x = x_ref[...].astype(F32)
    xbuf[POOL_HALO:POOL_HALO + ts, :] = x
    pos = s * ts + lax.broadcasted_iota(jnp.int32, (ts, 1), 0)
    for gi, win in enumerate(POOL_WINDOWS):
        lo, hi = gi * gw, (gi + 1) * gw
        xi = x[:, lo:hi]
        acc = xi
        for k in range(1, win):
            acc = acc + xbuf[POOL_HALO - k:POOL_HALO - k + ts, lo:hi]
        count = jnp.minimum(pos + 1, win).astype(F32)
        pooled = acc / count - xi
        y = _dot(pooled.astype(BF16), pw_ref[gi])
        o_ref[:, lo:hi] = (y * sc_ref[:, lo:hi] * _silu(g_ref[:, lo:hi].astype(F32))).astype(o_ref.dtype)
    xbuf[0:POOL_HALO, :] = xbuf[ts:ts + POOL_HALO, :]


def _pool_branch(u3, pool_w, pool_scale, w, col0):
    b, s, _ = u3.shape
    ts = _tile(s, 512)
    c0 = col0 // w
    ng, gw, _ = pool_w.shape
    return pl.pallas_call(
        _pool_kernel,
        grid=(b, s // ts),
        in_specs=[
            pl.BlockSpec((None, ts, w), lambda i, t: (i, t, c0)),
            pl.BlockSpec((None, ts, w), lambda i, t: (i, t, c0 + 1)),
            pl.BlockSpec((ng, gw, gw), lambda i, t: (0, 0, 0)),
            pl.BlockSpec((1, w), lambda i, t: (0, 0)),
        ],
        out_specs=pl.BlockSpec((None, ts, w), lambda i, t: (i, t, 0)),
        out_shape=jax.ShapeDtypeStruct((b, s, w), BF16),
        scratch_shapes=[pltpu.VMEM((POOL_HALO + ts, w), F32)],
        compiler_params=_params(("parallel", "arbitrary")),
        name="pool_branch",
    )(u3, u3, pool_w, pool_scale)


def _rwkv_prep_kernel(r_ref, k_ref, v_ref, lo_ref, mu_ref, mul_ref, w0_ref, w2_ref, a0_ref, a2_ref,
                      kkw_ref, kaw_ref, hb_ref,
                      ro_ref, ko_ref, vo_ref, na_ref, bv_ref, ld_ref, buf, lbuf):
    ts, w = ro_ref.shape

    @pl.when(pl.program_id(1) == 0)
    def _():
        buf[0:HALO, :] = jnp.zeros((HALO, buf.shape[1]), F32)
        lbuf[0:HALO, :] = jnp.zeros((HALO, lbuf.shape[1]), F32)

    for idx, ref in enumerate((r_ref, k_ref, v_ref)):
        buf[HALO:HALO + ts, idx * w:(idx + 1) * w] = ref[...].astype(F32)
    lbuf[HALO:HALO + ts, :] = lo_ref[...]

    def mixed(cur, prev, mu):
        return cur + (prev - cur) * mu

    mu = mu_ref[...]
    r = mixed(buf[HALO:HALO + ts, 0:w], buf[HALO - 1:HALO - 1 + ts, 0:w], mu[0:1])
    k = mixed(buf[HALO:HALO + ts, w:2 * w], buf[HALO - 1:HALO - 1 + ts, w:2 * w], mu[1:2])
    v = mixed(buf[HALO:HALO + ts, 2 * w:3 * w], buf[HALO - 1:HALO - 1 + ts, 2 * w:3 * w], mu[2:3])
    lo = mixed(lbuf[HALO:HALO + ts, :], lbuf[HALO - 1:HALO - 1 + ts, :], mul_ref[...])
    wl, al = lo[:, 0:LORA], lo[:, LORA:2 * LORA]

    z = w0_ref[...] + _dot(jnp.tanh(wl).astype(BF16), w2_ref[...])
    ld_ref[...] = -float(np.exp(-0.5)) * _sigmoid(z)
    a = _sigmoid(a0_ref[...] + _dot(al.astype(BF16), a2_ref[...]))

    kk = k * kkw_ref[...]
    hb = hb_ref[...]
    gw = hb.shape[0]
    kk2 = kk * kk
    n2 = jnp.concatenate([_head_sums(kk2[:, l0:l0 + gw], hb) for l0 in range(0, w, gw)], axis=1)
    kk = kk * lax.rsqrt(jnp.maximum(n2, 1e-24))
    ro_ref[...] = r.astype(ro_ref.dtype)
    ko_ref[...] = (k * (1.0 + (a - 1.0) * kaw_ref[...])).astype(ko_ref.dtype)
    vo_ref[...] = v.astype(vo_ref.dtype)
    na_ref[...] = (-kk).astype(na_ref.dtype)
    bv_ref[...] = (kk * a).astype(bv_ref.dtype)

    buf[0:HALO, :] = buf[ts:ts + HALO, :]
    lbuf[0:HALO, :] = lbuf[ts:ts + HALO, :]


def _rwkv_prep(u3, lora, mu3, mul, w0, w2, a0, a2, kkw, kaw, w, col0):
    b, s, _ = u3.shape
    ts = _tile(s, 256)
    c0 = col0 // w
    gw = _tile(w, MXU_DIM)
    head_of = np.arange(gw) // HEAD
    hb = jnp.asarray(head_of[:, None] == head_of[None, :], BF16)
    row = lambda k: pl.BlockSpec((None, ts, w), lambda i, t, k=k: (i, t, c0 + k))
    full = lambda shape: pl.BlockSpec(shape, lambda i, t: (0,) * len(shape))
    outw = pl.BlockSpec((None, ts, w), lambda i, t: (i, t, 0))
    sds = lambda dt: jax.ShapeDtypeStruct((b, s, w), dt)
    return pl.pallas_call(
        _rwkv_prep_kernel,
        grid=(b, s // ts),
        in_specs=[row(0), row(1), row(2),
                  pl.BlockSpec((None, ts, 2 * LORA), lambda i, t: (i, t, 0)),
                  full((3, w)), full((1, 2 * LORA)), full((1, w)), full((LORA, w)),
                  full((1, w)), full((LORA, w)), full((1, w)), full((1, w)),
                  full((gw, gw))],
        out_specs=[outw] * 6,
        out_shape=[sds(BF16)] * 5 + [sds(F32)],
        scratch_shapes=[pltpu.VMEM((HALO + ts, 3 * w), F32), pltpu.VMEM((HALO + ts, 2 * LORA), F32)],
        compiler_params=_params(("parallel", "arbitrary")),
        name="rwkv_prep",
    )(u3, u3, u3, lora, mu3, mul, w0, w2, a0, a2, kkw, kaw, hb)


def _head_stack(x, lane_head):
    zero = jnp.zeros_like(x)
    return jnp.concatenate([jnp.where(lane_head == 0, x, zero), jnp.where(lane_head == 1, x, zero)], axis=0)


SUM_PIECES = 2


def _rwkv_scan_kernel(r_ref, k_ref, v_ref, na_ref, bv_ref, ld_ref, g_ref, lng_ref, lnb_ref, rk_ref,
                      cl_ref, cu_ref, hb_ref, o_ref, state, ybuf, dec_s, rh_s, ah_s, bt_s, kt_s, bb_s, kb_s):
    ts, width = o_ref.shape
    n_pairs = width // LANES
    c = RWKV_CHUNK
    n_chunks = ts // c

    @pl.when(pl.program_id(2) == 0)
    def _():
        state[...] = jnp.zeros_like(state)

    sb = cl_ref.shape[0]
    lcs, lrests = [], []
    for r0 in range(0, ts, sb):
        pieces = jnp.concatenate(_bf16_pieces(ld_ref[r0:r0 + sb, :], SUM_PIECES), axis=1)
        for m_ref, outs in ((cl_ref, lcs), (cu_ref, lrests)):
            both = _dot(m_ref[...], pieces)
            outs.append(both[:, 0:width] + both[:, width:2 * width])
    ld = ld_ref[...]
    lc = jnp.concatenate(lcs, axis=0)
    lrest = jnp.concatenate(lrests, axis=0)
    e_in = jnp.exp(lc)
    e_neg = jnp.exp(-lc)
    e_end = jnp.exp(lrest)
    r_all = r_ref[...].astype(F32)
    k_all = k_ref[...].astype(F32)
    na_all = na_ref[...].astype(F32)
    bv_all = bv_ref[...].astype(F32)
    dec_s[...] = e_in
    rh_s[...] = (r_all * e_in).astype(BF16)
    ah_s[...] = (na_all * jnp.exp(lc - ld)).astype(BF16)
    bt_s[...] = (bv_all * e_neg).astype(BF16)
    kt_s[...] = (k_all * e_neg).astype(BF16)
    bb_s[...] = (bv_all * e_end).astype(BF16)
    kb_s[...] = (k_all * e_end).astype(BF16)

    lane_head = lax.broadcasted_iota(jnp.int32, (1, LANES), 1) // HEAD
    t_idx = lax.broadcasted_iota(jnp.int32, (c, LANES), 0)
    s_idx = lax.broadcasted_iota(jnp.int32, (c, LANES), 1) % c
    strict = s_idx < t_idx
    incl = s_idx <= t_idx
    blk = (lax.broadcasted_iota(jnp.int32, (LANES, LANES), 0) // HEAD
           == lax.broadcasted_iota(jnp.int32, (LANES, LANES), 1) // HEAD)
    eye_cat = (s_idx == t_idx).astype(F32)

    def chunk(ci, states):
        rows = pl.ds(pl.multiple_of(ci * c, c), c)
        tail_rows = pl.ds(pl.multiple_of(ci * c + c - 8, 8), 8)
        pairs = range(n_pairs)
        lanes = [slice(pi * LANES, (pi + 1) * LANES) for pi in pairs]
        each = lambda f: [f(pi) for pi in pairs]
        v = each(lambda pi: v_ref[rows, lanes[pi]])
        rh = each(lambda pi: rh_s[rows, lanes[pi]])
        ah = each(lambda pi: ah_s[rows, lanes[pi]])
        sc = each(lambda pi: _dot_nt(
            jnp.concatenate([ah[pi], rh[pi]], axis=0),
            jnp.concatenate([_head_stack(bt_s[rows, lanes[pi]], lane_head),
                             _head_stack(kt_s[rows, lanes[pi]], lane_head)], axis=0)))
        a_ak = each(lambda pi: jnp.where(strict, sc[pi][0:c, LANES:2 * LANES], 0.0).astype(BF16))
        a_rb = each(lambda pi: jnp.where(incl, sc[pi][c:2 * c, 0:LANES], 0.0).astype(BF16))
        a_rk = each(lambda pi: jnp.where(incl, sc[pi][c:2 * c, LANES:2 * LANES], 0.0).astype(BF16))

        x = each(lambda pi: jnp.where(strict, sc[pi][0:c, 0:LANES], 0.0))
        tinv = each(lambda pi: eye_cat + x[pi])
        p2 = 2
        while p2 < c:
            xb = each(lambda pi: x[pi].astype(BF16))
            x = each(lambda pi: _dot(xb[pi], _head_stack(xb[pi], lane_head)))
            tinv = each(lambda pi: tinv[pi] + _dot(tinv[pi].astype(BF16),
                                                   _head_stack(x[pi].astype(BF16), lane_head)))
            p2 *= 2
        t_cat = each(lambda pi: tinv[pi].astype(BF16))

        g2b = each(lambda pi: states[pi].astype(BF16))
        vs = each(lambda pi: _head_stack(v[pi], lane_head))
        p = each(lambda pi: _dot_nt(ah[pi], g2b[pi]) + _dot(a_ak[pi], vs[pi]))
        ub = each(lambda pi: _dot(t_cat[pi], _head_stack(p[pi].astype(BF16), lane_head)).astype(BF16))
        new_states = []
        for pi in pairs:
            y = (_dot_nt(rh[pi], g2b[pi]) + _dot(a_rb[pi], _head_stack(ub[pi], lane_head))
                 + _dot(a_rk[pi], vs[pi]))
            ybuf[rows, lanes[pi]] = y
            uv_t = jnp.concatenate([ub[pi], v[pi]], axis=0).T
            upd = _dot(uv_t, jnp.concatenate([bb_s[rows, lanes[pi]], kb_s[rows, lanes[pi]]], axis=0))
            decay = dec_s[tail_rows, lanes[pi]][7:8, :]
            new_states.append(states[pi] * decay + jnp.where(blk, upd, 0.0))
        return tuple(new_states)

    final = lax.fori_loop(0, n_chunks, chunk, tuple(state[pi] for pi in range(n_pairs)))
    for pi in range(n_pairs):
        state[pi] = final[pi]

    hb = hb_ref[...]
    gw = hb.shape[0]
    for l0 in range(0, width, gw):
        lanes = slice(l0, l0 + gw)
        y = ybuf[:, lanes]
        mean = _head_sums(y, hb) * (1.0 / HEAD)
        d = y - mean
        var = _head_sums(d * d, hb) * (1.0 / HEAD)
        yn = d * lax.rsqrt(var + RWKV_LN_EPS) * lng_ref[:, lanes] + lnb_ref[:, lanes]
        rk = r_ref[:, lanes].astype(F32) * k_ref[:, lanes].astype(F32) * rk_ref[:, lanes]
        bonus = _head_sums(rk, hb) * v_ref[:, lanes].astype(F32)
        o_ref[:, lanes] = ((yn + bonus) * _silu(g_ref[:, lanes].astype(F32))).astype(o_ref.dtype)


def _head_sums(x, m):
    return functools.reduce(jnp.add, [_dot(p, m) for p in _bf16_pieces(x, SUM_PIECES)])


RWKV_PAIRS = 8


def _rwkv_scan(prep, u3, ln_g, ln_b, r_k, w, gate_col0):
    r, k, v, na, bv, ld = prep
    b, s, _ = r.shape
    ts = _tile(s, 512)
    width = LANES * _tile(w // LANES, RWKV_PAIRS)
    gc0 = gate_col0 // width
    sb = _tile(ts, MXU_DIM)
    chunk_of = np.arange(sb) // RWKV_CHUNK
    same = chunk_of[:, None] == chunk_of[None, :]
    tri = np.arange(sb)[:, None] >= np.arange(sb)[None, :]
    cl = jnp.asarray(same & tri, BF16)
    cu = jnp.asarray(same & ~tri, BF16)
    gw = _tile(width, MXU_DIM)
    head_of = np.arange(gw) // HEAD
    hb = jnp.asarray(head_of[:, None] == head_of[None, :], BF16)
    row = pl.BlockSpec((None, ts, width), lambda i, p, t: (i, t, p))
    vec = pl.BlockSpec((1, width), lambda i, p, t: (0, p))
    const = lambda n: pl.BlockSpec((n, n), lambda i, p, t: (0, 0))
    tile_bf16 = pltpu.VMEM((ts, width), BF16)
    return pl.pallas_call(
        _rwkv_scan_kernel,
        grid=(b, w // width, s // ts),
        in_specs=[row] * 6 + [pl.BlockSpec((None, ts, width), lambda i, p, t: (i, t, gc0 + p)),
                              vec, vec, vec, const(sb), const(sb), const(gw)],
        out_specs=row,
        out_shape=jax.ShapeDtypeStruct((b, s, w), BF16),
        scratch_shapes=[pltpu.VMEM((width // LANES, LANES, LANES), F32), pltpu.VMEM((ts, width), F32),
                        pltpu.VMEM((ts, width), F32)] + [tile_bf16] * 6,
        compiler_params=_params(("parallel", "parallel", "arbitrary")),
        name="rwkv_scan",
    )(r, k, v, na, bv, ld, u3, ln_g, ln_b, r_k, cl, cu, hb)


FOX_BLOCK = 1024
FOX_SUB = 16
FOX_ACC_ROWS = HEAD + 16


def _fold8(x, op):
    parts = [x[i:i + 8, :] for i in range(0, x.shape[0], 8)]
    return functools.reduce(op, parts)


def _fox_kernel(qi_ref, kj_ref, q_ref, k_ref, v_ref, kc_ref, g_ref, o_ref,
                m_ref, acc_ref, qx_ref, s0_ref, s1_ref, p0_ref, p1_ref):
    step = pl.program_id(2)
    qi, kj = qi_ref[step], kj_ref[step]
    tq, tk = q_ref.shape[0], k_ref.shape[0]
    lane = lax.broadcasted_iota(jnp.int32, (1, LANES), 1)

    @pl.when(kj == 0)
    def _():
        m_ref[...] = jnp.full_like(m_ref, NEG_BIG)
        acc_ref[...] = jnp.zeros_like(acc_ref)
        q2 = q_ref[...]
        for h in range(2):
            first = _forget_lane(h, 0)
            ones = jnp.where((lane >= first) & (lane < first + C_PIECES), 1.0, 0.0).astype(BF16)
            qx_ref[h, :, 0:LANES] = jnp.where(lane // HEAD == h, q2, jnp.zeros_like(q2))
            qx_ref[h, :, LANES:2 * LANES] = jnp.broadcast_to(ones, q2.shape)

    s_refs, p_refs = (s0_ref, s1_ref), (p0_ref, p1_ref)

    def step_body(masked):
        kx = jnp.concatenate([k_ref[...], kc_ref[...]], axis=1)
        half = tk // 2
        for h in range(2):
            if masked:
                s_refs[h][0:half, :] = _dot_nt(kx[0:half, :], qx_ref[h])
                s_refs[h][half:, half:] = _dot_nt(kx[half:, :], qx_ref[h, half:, :])
            else:
                s_refs[h][...] = _dot_nt(kx, qx_ref[h])
        n_sub, n_grp = tk // FOX_SUB, tq // LANES

        def first_group(kb):
            return (kb * FOX_SUB) // LANES if masked else 0

        def group_scores(h, kb, g):
            blk = s_refs[h][kb * FOX_SUB:(kb + 1) * FOX_SUB, g * LANES:(g + 1) * LANES]
            if masked and g == first_group(kb):
                key = kb * FOX_SUB + lax.broadcasted_iota(jnp.int32, (FOX_SUB, LANES), 0)
                query = g * LANES + lax.broadcasted_iota(jnp.int32, (FOX_SUB, LANES), 1)
                blk = jnp.where(key <= query, blk, NEG_BIG)
            return blk

        def scores(h, kb, g0):
            return jnp.concatenate([group_scores(h, kb, g) for g in range(g0, n_grp)], axis=1)

        def running_max(h):
            mx = [None] * n_grp
            for kb in range(n_sub):
                g0 = first_group(kb)
                part = _fold8(scores(h, kb, g0), jnp.maximum)
                for g in range(g0, n_grp):
                    piece = part[:, (g - g0) * LANES:(g - g0 + 1) * LANES]
                    mx[g] = piece if mx[g] is None else jnp.maximum(mx[g], piece)
            m_prev = m_ref[h]
            m_new = jnp.maximum(m_prev, jnp.max(jnp.concatenate(mx, axis=1), axis=0, keepdims=True))
            m_ref[h] = m_new
            return m_new, jnp.exp2(m_prev - m_new)

        def probabilities(h, m_new):
            for kb in range(n_sub):
                g0 = first_group(kb)
                rows = slice(kb * FOX_SUB, (kb + 1) * FOX_SUB)
                p_refs[h][rows, g0 * LANES:] = jnp.exp2(scores(h, kb, g0) - m_new[:, g0 * LANES:]).astype(BF16)
                z0 = half if kb * FOX_SUB >= half else 0
                if g0 * LANES > z0:
                    p_refs[h][rows, z0:g0 * LANES] = jnp.zeros((FOX_SUB, g0 * LANES - z0), BF16)

        def accumulate(h, alpha):
            lhs = jnp.concatenate([v_t[h * HEAD:(h + 1) * HEAD, :], sum_rows], axis=0)
            if masked:
                acc_ref[h] = alpha * acc_ref[h] + _dot(lhs[:, 0:half], p_refs[h][0:half, :])
                acc_ref[h, :, half:] += _dot(lhs[:, half:], p_refs[h][half:, half:])
            else:
                acc_ref[h] = alpha * acc_ref[h] + _dot(lhs, p_refs[h][...])

        eye = (lax.broadcasted_iota(jnp.int32, (LANES, LANES), 0)
               == lax.broadcasted_iota(jnp.int32, (LANES, LANES), 1)).astype(BF16)
        v_t = _dot_nt(eye, v_ref[...]).astype(BF16)
        sum_rows = jnp.ones((FOX_ACC_ROWS - HEAD, tk), BF16)
        m0, alpha0 = running_max(0)
        probabilities(0, m0)
        m1, alpha1 = running_max(1)
        accumulate(0, alpha0)
        probabilities(1, m1)
        accumulate(1, alpha1)

    @pl.when(kj < qi)
    def _():
        step_body(False)

    @pl.when(kj == qi)
    def _():
        step_body(True)
        outs = [acc_ref[h, 0:HEAD, :] * (1.0 / acc_ref[h, HEAD:HEAD + 1, :]) for h in range(2)]
        out = jnp.concatenate(outs, axis=0).T
        o_ref[...] = (out * _silu(g_ref[...].astype(F32))).astype(o_ref.dtype)


def _fox_branch(u3, kc, w, q_col0, gate_col0):
    b, s, _ = u3.shape
    tq = tk = _tile(s, FOX_BLOCK)
    n_pairs = w // LANES
    pairs = [(i, j) for i in range(s // tq) for j in range(i + 1)]
    qi = jnp.asarray([p[0] for p in pairs], jnp.int32)
    kj = jnp.asarray([p[1] for p in pairs], jnp.int32)
    qc0, gc0 = q_col0 // LANES, gate_col0 // LANES
    grid_spec = pltpu.PrefetchScalarGridSpec(
        num_scalar_prefetch=2,
        grid=(b, n_pairs, len(pairs)),
        in_specs=[
            pl.BlockSpec((None, tq, LANES), lambda i, p, t, qi, kj: (i, qi[t], qc0 + p)),
            pl.BlockSpec((None, tk, LANES), lambda i, p, t, qi, kj: (i, kj[t], qc0 + n_pairs + p)),
            pl.BlockSpec((None, tk, LANES), lambda i, p, t, qi, kj: (i, kj[t], qc0 + 2 * n_pairs + p)),
            pl.BlockSpec((None, tk, LANES), lambda i, p, t, qi, kj: (i, kj[t], p)),
            pl.BlockSpec((None, tq, LANES), lambda i, p, t, qi, kj: (i, qi[t], gc0 + p)),
        ],
        out_specs=pl.BlockSpec((None, tq, LANES), lambda i, p, t, qi, kj: (i, qi[t], p)),
        scratch_shapes=[pltpu.VMEM((2, 1, tq), F32),
                        pltpu.VMEM((2, FOX_ACC_ROWS, tq), F32), pltpu.VMEM((2, tq, 2 * LANES), BF16),
                        pltpu.VMEM((tk, tq), F32), pltpu.VMEM((tk, tq), F32),
                        pltpu.VMEM((tk, tq), BF16), pltpu.VMEM((tk, tq), BF16)],
    )
    return pl.pallas_call(
        _fox_kernel,
        grid_spec=grid_spec,
        out_shape=jax.ShapeDtypeStruct((b, s, w), BF16),
        compiler_params=_params(("parallel", "parallel", "arbitrary")),
        name="fox_attention",
    )(qi, kj, u3, u3, u3, kc, u3)


def _merge_kernel(ya_ref, yb_ref, yc_ref, yd_ref, ma_ref, mb_ref, mc_ref, md_ref, bm_ref, wb_ref, o_ref):
    acc = None
    branches = ((ya_ref, ma_ref), (yb_ref, mb_ref), (yc_ref, mc_ref), (yd_ref, md_ref))
    for kbr, (y_ref, ml_ref) in enumerate(branches):
        proj = _dot(y_ref[...], wb_ref[kbr])
        gate = _sigmoid(ml_ref[...].astype(F32) + bm_ref[kbr:kbr + 1, :])
        acc = gate * proj if acc is None else acc + gate * proj
    o_ref[...] = acc.astype(o_ref.dtype)


def _merge(ys, u2, b_merge, w_branch_all, layer, ml_col0):
    t, w = ys[0].shape
    _, nb, _, d = w_branch_all.shape
    tm, tn = _tile(t, 1024), _tile(d, 512)
    nd = d // tn
    mc0 = ml_col0 // tn
    yspec = pl.BlockSpec((tm, w), lambda i, j: (i, 0))
    mspec = lambda k: pl.BlockSpec((tm, tn), lambda i, j, k=k: (i, mc0 + k * nd + j))
    return pl.pallas_call(
        _merge_kernel,
        grid=(t // tm, nd),
        in_specs=[yspec] * 4 + [mspec(k) for k in range(4)] + [
            pl.BlockSpec((nb, tn), lambda i, j: (0, j)),
            pl.BlockSpec((None, nb, w, tn), lambda i, j: (layer, 0, 0, j)),
        ],
        out_specs=pl.BlockSpec((tm, tn), lambda i, j: (i, j)),
        out_shape=jax.ShapeDtypeStruct((t, d), BF16),
        compiler_params=_params(("parallel", "arbitrary")),
        name="merge",
    )(*ys, u2, u2, u2, u2, b_merge, w_branch_all)


def _outproj_kernel(m_ref, w_ref, x_ref, o_ref):
    o_ref[...] = x_ref[...] + _dot(m_ref[...], w_ref[...])


def _outproj(merged, w_out_all, layer, x2):
    t, d = x2.shape
    tm, tn = _tile(t, 1024), _tile(d, 512)
    return pl.pallas_call(
        _outproj_kernel,
        grid=(t // tm, d // tn),
        in_specs=[
            pl.BlockSpec((tm, d), lambda i, j: (i, 0)),
            pl.BlockSpec((None, d, tn), lambda i, j: (layer, 0, j)),
            pl.BlockSpec((tm, tn), lambda i, j: (i, j)),
        ],
        out_specs=pl.BlockSpec((tm, tn), lambda i, j: (i, j)),
        out_shape=jax.ShapeDtypeStruct((t, d), F32),
        compiler_params=_params(("parallel", "arbitrary")),
        name="outproj",
    )(merged, w_out_all, x2)


def _final_norm_kernel(x_ref, g_ref, o_ref):
    o_ref[...] = _rms_rows(x_ref[...], g_ref[...])


def _final_norm(x2, g):
    t, d = x2.shape
    tm = _tile(t, 512)
    return pl.pallas_call(
        _final_norm_kernel,
        grid=(t // tm,),
        in_specs=[pl.BlockSpec((tm, d), lambda i: (i, 0)), pl.BlockSpec((1, d), lambda i: (0, 0))],
        out_specs=pl.BlockSpec((tm, d), lambda i: (i, 0)),
        out_shape=jax.ShapeDtypeStruct((t, d), F32),
        compiler_params=_params(("parallel",)),
        name="final_norm",
    )(x2, g)


def _split_in_weights(w_in, w, nh):
    w_t = jnp.swapaxes(w_in, 1, 2)
    lo0, q0, f0 = 7 * w, 8 * w + 2 * LORA, 11 * w + 2 * LORA
    cast = lambda a: a.astype(BF16)
    w_main = jnp.concatenate([cast(w_t[:, :lo0]), cast(w_t[:, lo0 + 2 * LORA:q0]),
                              cast(w_t[:, q0:q0 + w] * (HEAD ** -0.5 * LOG2E)),
                              cast(w_t[:, q0 + w:f0]), cast(w_t[:, f0 + nh:])], axis=1)
    pad = jnp.zeros((w_t.shape[0], LANES - nh, w_t.shape[2]), w_t.dtype)
    w_small = jnp.concatenate([w_t[:, lo0:lo0 + 2 * LORA], w_t[:, f0:f0 + nh], pad], axis=1).astype(BF16)
    return w_main, w_small


def _layer(x2, bsz, seq, layer, w_main_all, w_small, w_branch_all, w_out_all, norm_g, b_merge, conv_w, rwkv_mu,
           rwkv_w0, rwkv_w2, rwkv_a0, rwkv_a2, rwkv_kk, rwkv_ka, rwkv_rk, rwkv_ln_g, rwkv_ln_b, fox_bf, pool_w,
           pool_scale):
    t, d = x2.shape
    w = conv_w.shape[1]
    nh = fox_bf.shape[0]
    b_f = jnp.concatenate([fox_bf, jnp.zeros((LANES - nh,), fox_bf.dtype)]).reshape(1, LANES)
    row = lambda a: a.reshape(1, -1)

    u2 = _inproj(x2, row(norm_g), w_main_all, layer)
    u3 = u2.reshape(bsz, seq, -1)
    lora, kc = _side(x2.reshape(bsz, seq, d), row(norm_g), w_small, b_f, w)
    y_a = _conv_branch(u3, conv_w, w, 0)
    prep = _rwkv_prep(u3, lora, rwkv_mu[:3 * w].reshape(3, w), row(rwkv_mu[3 * w:]), row(rwkv_w0),
                      rwkv_w2.astype(BF16), row(rwkv_a0), rwkv_a2.astype(BF16), row(rwkv_kk), row(rwkv_ka),
                      w, 4 * w)
    y_b = _rwkv_scan(prep, u3, row(rwkv_ln_g), row(rwkv_ln_b), row(rwkv_rk), w, 7 * w)
    y_c = _fox_branch(u3, kc, w, 8 * w, 11 * w)
    y_d = _pool_branch(u3, pool_w.astype(BF16), row(pool_scale), w, 12 * w)
    ys = [y.reshape(t, w) for y in (y_a, y_b, y_c, y_d)]
    merged = _merge(ys, u2, b_merge, w_branch_all, layer, 14 * w)
    return _outproj(merged, w_out_all, layer, x2)


def kernel(x, norm_g, w_in, b_merge, conv_w, rwkv_mu, rwkv_w0, rwkv_w2, rwkv_a0, rwkv_a2, rwkv_kk, rwkv_ka,
           rwkv_rk, rwkv_ln_g, rwkv_ln_b, fox_bf, pool_w, pool_scale, w_branch, w_out, final_g):
    bsz, seq, d = x.shape
    x2 = x.reshape(bsz * seq, d)
    w_main_all, w_small_all = _split_in_weights(w_in, conv_w.shape[2], fox_bf.shape[1])
    w_branch_all = w_branch.astype(BF16)
    w_out_all = w_out.astype(BF16)
    for l in range(norm_g.shape[0]):
        x2 = _layer(x2, bsz, seq, l, w_main_all, w_small_all[l], w_branch_all, w_out_all, norm_g[l], b_merge[l],
                    conv_w[l], rwkv_mu[l], rwkv_w0[l], rwkv_w2[l], rwkv_a0[l], rwkv_a2[l], rwkv_kk[l], rwkv_ka[l],
                    rwkv_rk[l], rwkv_ln_g[l], rwkv_ln_b[l], fox_bf[l], pool_w[l], pool_scale[l])
    return _final_norm(x2, final_g.reshape(1, d)).reshape(bsz, seq, d)
```

```python
import functools

import jax
import jax.numpy as jnp
import numpy as np
from jax import lax
from jax.experimental import pallas as pl
from jax.experimental.pallas import tpu as pltpu

F32 = jnp.float32
BF16 = jnp.bfloat16
HIGHEST = lax.Precision.HIGHEST

NORM_EPS = 1e-6
RWKV_LN_EPS = 64e-5
HEAD = 64
LORA = 64
POOL_WINDOWS = (2, 4, 8, 16)
RWKV_CHUNK = 64
LANES = 128
MXU_DIM = 256
LOG2E = float(np.log2(np.e))
NEG_BIG = -1e30
VMEM_LIMIT = 48 * 1024 * 1024


def _params(sem):
    return pltpu.CompilerParams(dimension_semantics=sem, vmem_limit_bytes=VMEM_LIMIT)


def _tile(n, want):
    t = min(n, want)
    while n % t:
        t //= 2
    return t


def _sigmoid(x):
    return 1.0 / (1.0 + jnp.exp(-x))


def _silu(x):
    return x * _sigmoid(x)


def _rms_rows(xf, g):
    ms = jnp.mean(xf * xf, axis=-1, keepdims=True)
    return xf * lax.rsqrt(ms + NORM_EPS) * g


def _dot(a, b):
    return jnp.dot(a, b, preferred_element_type=F32)


def _dot_nt(a, b):
    return lax.dot_general(a, b, (((1,), (1,)), ((), ())), preferred_element_type=F32)


def _dot_f32(a, b):
    return jnp.dot(a, b, precision=HIGHEST, preferred_element_type=F32)


def _inproj_kernel(x_ref, g_ref, w_ref, o_ref, h_ref):
    @pl.when(pl.program_id(1) == 0)
    def _():
        h_ref[...] = _rms_rows(x_ref[...], g_ref[...]).astype(BF16)

    o_ref[...] = _dot_nt(h_ref[...], w_ref[...]).astype(o_ref.dtype)


def _inproj(x2, g, w_all, layer):
    t, d = x2.shape
    n = w_all.shape[1]
    tm, tn = _tile(t, 1024), _tile(n, 1024)
    return pl.pallas_call(
        _inproj_kernel,
        grid=(t // tm, n // tn),
        in_specs=[
            pl.BlockSpec((tm, d), lambda i, j: (i, 0)),
            pl.BlockSpec((1, d), lambda i, j: (0, 0)),
            pl.BlockSpec((None, tn, d), lambda i, j: (layer, j, 0)),
        ],
        out_specs=pl.BlockSpec((tm, tn), lambda i, j: (i, j)),
        out_shape=jax.ShapeDtypeStruct((t, n), BF16),
        scratch_shapes=[pltpu.VMEM((tm, d), BF16)],
        compiler_params=_params(("parallel", "arbitrary")),
        name="inproj",
    )(x2, g, w_all)


C_PIECES = 3


def _bf16_pieces(x, n=C_PIECES):
    pieces = []
    for _ in range(n):
        p = x.astype(BF16)
        pieces.append(p)
        x = x - p.astype(F32)
    return pieces


def _side_kernel(x_ref, g_ref, ws_ref, bf_ref, sel_ref, lora_ref, kc_ref, carry_ref):
    @pl.when(pl.program_id(1) == 0)
    def _():
        carry_ref[...] = jnp.zeros_like(carry_ref)

    h = _rms_rows(x_ref[...], g_ref[...]).astype(BF16)
    sf = _dot_nt(h, ws_ref[...])
    ns = lora_ref.shape[1]
    lora_ref[...] = sf[:, :ns]
    z = sf[:, ns:] + bf_ref[...]
    logf = jnp.minimum(z, 0.0) - jnp.log1p(jnp.exp(-jnp.abs(z)))
    ts = logf.shape[0]
    lower = (lax.broadcasted_iota(jnp.int32, (ts, ts), 0)
             >= lax.broadcasted_iota(jnp.int32, (ts, ts), 1)).astype(F32)
    c = _dot_f32(lower, logf) + carry_ref[...]
    carry_ref[...] = c[ts - 1:ts, :]
    kc = None
    for piece, sel in zip(_bf16_pieces(-LOG2E * c), (sel_ref[0], sel_ref[1], sel_ref[2])):
        term = _dot(piece, sel)
        kc = term if kc is None else kc + term
    kc_ref[...] = kc.astype(kc_ref.dtype)


def _forget_lane(head, piece):
    return C_PIECES * head + piece


def _side(x3, g, w_small, b_f, w):
    b, s, d = x3.shape
    ns = w_small.shape[0] - LANES
    ts = _tile(s, 512)
    nh = w // HEAD
    sel = np.zeros((C_PIECES, LANES, w), np.float32)
    for hd in range(nh):
        for piece in range(C_PIECES):
            sel[piece, hd, LANES * (hd // 2) + _forget_lane(hd % 2, piece)] = 1.0
    return pl.pallas_call(
        _side_kernel,
        grid=(b, s // ts),
        in_specs=[
            pl.BlockSpec((None, ts, d), lambda i, j: (i, j, 0)),
            pl.BlockSpec((1, d), lambda i, j: (0, 0)),
            pl.BlockSpec((ns + LANES, d), lambda i, j: (0, 0)),
            pl.BlockSpec((1, LANES), lambda i, j: (0, 0)),
            pl.BlockSpec((C_PIECES, LANES, w), lambda i, j: (0, 0, 0)),
        ],
        out_specs=[
            pl.BlockSpec((None, ts, ns), lambda i, j: (i, j, 0)),
            pl.BlockSpec((None, ts, w), lambda i, j: (i, j, 0)),
        ],
        out_shape=[
            jax.ShapeDtypeStruct((b, s, ns), F32),
            jax.ShapeDtypeStruct((b, s, w), BF16),
        ],
        scratch_shapes=[pltpu.VMEM((1, LANES), F32)],
        compiler_params=_params(("parallel", "arbitrary")),
        name="side",
    )(x3, g, w_small, b_f, jnp.asarray(sel, BF16))


HALO = 8


def _conv_kernel(bg_ref, cg_ref, xv_ref, g_ref, cw_ref, o_ref, pbuf):
    ts = o_ref.shape[0]

    @pl.when(pl.program_id(2) == 0)
    def _():
        pbuf[0:HALO, :] = jnp.zeros((HALO, pbuf.shape[1]), F32)

    p = cg_ref[...].astype(F32) * xv_ref[...].astype(F32)
    pbuf[HALO:HALO + ts, :] = p
    cw = cw_ref[...]
    z = (cw[0:1] * pbuf[HALO - 2:HALO - 2 + ts, :] + cw[1:2] * pbuf[HALO - 1:HALO - 1 + ts, :]
         + cw[2:3] * p)
    o_ref[...] = (bg_ref[...].astype(F32) * z * _silu(g_ref[...].astype(F32))).astype(o_ref.dtype)
    pbuf[0:HALO, :] = pbuf[ts:ts + HALO, :]


def _conv_branch(u3, conv_w, w, col0):
    b, s, _ = u3.shape
    ts, tw = _tile(s, 512), _tile(w, 512)
    nw = w // tw
    c0 = col0 // tw

    def spec(k):
        return pl.BlockSpec((None, ts, tw), lambda i, j, t, k=k: (i, t, c0 + k * nw + j))

    return pl.pallas_call(
        _conv_kernel,
        grid=(b, nw, s // ts),
        in_specs=[spec(0), spec(1), spec(2), spec(3),
                  pl.BlockSpec((conv_w.shape[0], tw), lambda i, j, t: (0, j))],
        out_specs=pl.BlockSpec((None, ts, tw), lambda i, j, t: (i, t, j)),
        out_shape=jax.ShapeDtypeStruct((b, s, w), BF16),
        scratch_shapes=[pltpu.VMEM((HALO + ts, tw), F32)],
        compiler_params=_params(("parallel", "parallel", "arbitrary")),
        name="conv_branch",
    )(u3, u3, u3, u3, conv_w)


POOL_HALO = 16


def _pool_kernel(x_ref, g_ref, pw_ref, sc_ref, o_ref, xbuf):
    ts, w = o_ref.shape
    gw = w // len(POOL_WINDOWS)
    s = pl.program_id(1)

    @pl.when(s == 0)
    def _():
        xbuf[0:POOL_HALO, :] = jnp.zeros((POOL_HALO, w), F32)

    x = x_ref[...].astype(F32)
    xbuf[POOL_HALO:POOL_HALO + ts, :] = x
    pos = s * ts + lax.broadcasted_iota(jnp.int32, (ts, 1), 0)
    for gi, win in enumerate(POOL_WINDOWS):
        lo, hi = gi * gw, (gi + 1) * gw
        xi = x[:, lo:hi]
        acc = xi
        for k in range(1, win):
            acc = acc + xbuf[POOL_HALO - k:POOL_HALO - k + ts, lo:hi]
        count = jnp.minimum(pos + 1, win).astype(F32)
        pooled = acc / count - xi
        y = _dot(pooled.astype(BF16), pw_ref[gi])
        o_ref[:, lo:hi] = (y * sc_ref[:, lo:hi] * _silu(g_ref[:, lo:hi].astype(F32))).astype(o_ref.dtype)
    xbuf[0:POOL_HALO, :] = xbuf[ts:ts + POOL_HALO, :]


def _pool_branch(u3, pool_w, pool_scale, w, col0):
    b, s, _ = u3.shape
    ts = _tile(s, 512)
    c0 = col0 // w
    ng, gw, _ = pool_w.shape
    return pl.pallas_call(
        _pool_kernel,
        grid=(b, s // ts),
        in_specs=[
            pl.BlockSpec((None, ts, w), lambda i, t: (i, t, c0)),
            pl.BlockSpec((None, ts, w), lambda i, t: (i, t, c0 + 1)),
            pl.BlockSpec((ng, gw, gw), lambda i, t: (0, 0, 0)),
            pl.BlockSpec((1, w), lambda i, t: (0, 0)),
        ],
        out_specs=pl.BlockSpec((None, ts, w), lambda i, t: (i, t, 0)),
        out_shape=jax.ShapeDtypeStruct((b, s, w), BF16),
        scratch_shapes=[pltpu.VMEM((POOL_HALO + ts, w), F32)],
        compiler_params=_params(("parallel", "arbitrary")),
        name="pool_branch",
    )(u3, u3, pool_w, pool_scale)


def _rwkv_prep_kernel(r_ref, k_ref, v_ref, lo_ref, mu_ref, mul_ref, w0_ref, w2_ref, a0_ref, a2_ref,
                      kkw_ref, kaw_ref, hb_ref,
                      ro_ref, ko_ref, vo_ref, na_ref, bv_ref, ld_ref, buf, lbuf):
    ts, w = ro_ref.shape

    @pl.when(pl.program_id(1) == 0)
    def _():
        buf[0:HALO, :] = jnp.zeros((HALO, buf.shape[1]), F32)
        lbuf[0:HALO, :] = jnp.zeros((HALO, lbuf.shape[1]), F32)

    for idx, ref in enumerate((r_ref, k_ref, v_ref)):
        buf[HALO:HALO + ts, idx * w:(idx + 1) * w] = ref[...].astype(F32)
    lbuf[HALO:HALO + ts, :] = lo_ref[...]

    def mixed(cur, prev, mu):
        return cur + (prev - cur) * mu

    mu = mu_ref[...]
    r = mixed(buf[HALO:HALO + ts, 0:w], buf[HALO - 1:HALO - 1 + ts, 0:w], mu[0:1])
    k = mixed(buf[HALO:HALO + ts, w:2 * w], buf[HALO - 1:HALO - 1 + ts, w:2 * w], mu[1:2])
    v = mixed(buf[HALO:HALO + ts, 2 * w:3 * w], buf[HALO - 1:HALO - 1 + ts, 2 * w:3 * w], mu[2:3])
    lo = mixed(lbuf[HALO:HALO + ts, :], lbuf[HALO - 1:HALO - 1 + ts, :], mul_ref[...])
    wl, al = lo[:, 0:LORA], lo[:, LORA:2 * LORA]

    z = w0_ref[...] + _dot(jnp.tanh(wl).astype(BF16), w2_ref[...])
    ld_ref[...] = -float(np.exp(-0.5)) * _sigmoid(z)
    a = _sigmoid(a0_ref[...] + _dot(al.astype(BF16), a2_ref[...]))

    kk = k * kkw_ref[...]
    hb = hb_ref[...]
    gw = hb.shape[0]
    kk2 = kk * kk
    n2 = jnp.concatenate([_head_sums(kk2[:, l0:l0 + gw], hb) for l0 in range(0, w, gw)], axis=1)
    kk = kk * lax.rsqrt(jnp.maximum(n2, 1e-24))
    ro_ref[...] = r.astype(ro_ref.dtype)
    ko_ref[...] = (k * (1.0 + (a - 1.0) * kaw_ref[...])).astype(ko_ref.dtype)
    vo_ref[...] = v.astype(vo_ref.dtype)
    na_ref[...] = (-kk).astype(na_ref.dtype)
    bv_ref[...] = (kk * a).astype(bv_ref.dtype)

    buf[0:HALO, :] = buf[ts:ts + HALO, :]
    lbuf[0:HALO, :] = lbuf[ts:ts + HALO, :]


def _rwkv_prep(u3, lora, mu3, mul, w0, w2, a0, a2, kkw, kaw, w, col0):
    b, s, _ = u3.shape
    ts = _tile(s, 256)
    c0 = col0 // w
    gw = _tile(w, MXU_DIM)
    head_of = np.arange(gw) // HEAD
    hb = jnp.asarray(head_of[:, None] == head_of[None, :], BF16)
    row = lambda k: pl.BlockSpec((None, ts, w), lambda i, t, k=k: (i, t, c0 + k))
    full = lambda shape: pl.BlockSpec(shape, lambda i, t: (0,) * len(shape))
    outw = pl.BlockSpec((None, ts, w), lambda i, t: (i, t, 0))
    sds = lambda dt: jax.ShapeDtypeStruct((b, s, w), dt)
    return pl.pallas_call(
        _rwkv_prep_kernel,
        grid=(b, s // ts),
        in_specs=[row(0), row(1), row(2),
                  pl.BlockSpec((None, ts, 2 * LORA), lambda i, t: (i, t, 0)),
                  full((3, w)), full((1, 2 * LORA)), full((1, w)), full((LORA, w)),
                  full((1, w)), full((LORA, w)), full((1, w)), full((1, w)),
                  full((gw, gw))],
        out_specs=[outw] * 6,
        out_shape=[sds(BF16)] * 5 + [sds(F32)],
        scratch_shapes=[pltpu.VMEM((HALO + ts, 3 * w), F32), pltpu.VMEM((HALO + ts, 2 * LORA), F32)],
        compiler_params=_params(("parallel", "arbitrary")),
        name="rwkv_prep",
    )(u3, u3, u3, lora, mu3, mul, w0, w2, a0, a2, kkw, kaw, hb)


def _head_stack(x, lane_head):
    zero = jnp.zeros_like(x)
    return jnp.concatenate([jnp.where(lane_head == 0, x, zero), jnp.where(lane_head == 1, x, zero)], axis=0)


SUM_PIECES = 2


def _rwkv_scan_kernel(r_ref, k_ref, v_ref, na_ref, bv_ref, ld_ref, g_ref, lng_ref, lnb_ref, rk_ref,
                      cl_ref, cu_ref, hb_ref, o_ref, state, ybuf, dec_s, rh_s, ah_s, bt_s, kt_s, bb_s, kb_s):
    ts, width = o_ref.shape
    n_pairs = width // LANES
    c = RWKV_CHUNK
    n_chunks = ts // c

    @pl.when(pl.program_id(2) == 0)
    def _():
        state[...] = jnp.zeros_like(state)

    sb = cl_ref.shape[0]
    lcs, lrests = [], []
    for r0 in range(0, ts, sb):
        pieces = jnp.concatenate(_bf16_pieces(ld_ref[r0:r0 + sb, :], SUM_PIECES), axis=1)
        for m_ref, outs in ((cl_ref, lcs), (cu_ref, lrests)):
            both = _dot(m_ref[...], pieces)
            outs.append(both[:, 0:width] + both[:, width:2 * width])
    ld = ld_ref[...]
    lc = jnp.concatenate(lcs, axis=0)
    lrest = jnp.concatenate(lrests, axis=0)
    e_in = jnp.exp(lc)
    e_neg = jnp.exp(-lc)
    e_end = jnp.exp(lrest)
    r_all = r_ref[...].astype(F32)
    k_all = k_ref[...].astype(F32)
    na_all = na_ref[...].astype(F32)
    bv_all = bv_ref[...].astype(F32)
    dec_s[...] = e_in
    rh_s[...] = (r_all * e_in).astype(BF16)
    ah_s[...] = (na_all * jnp.exp(lc - ld)).astype(BF16)
    bt_s[...] = (bv_all * e_neg).astype(BF16)
    kt_s[...] = (k_all * e_neg).astype(BF16)
    bb_s[...] = (bv_all * e_end).astype(BF16)
    kb_s[...] = (k_all * e_end).astype(BF16)

    lane_head = lax.broadcasted_iota(jnp.int32, (1, LANES), 1) // HEAD
    t_idx = lax.broadcasted_iota(jnp.int32, (c, LANES), 0)
    s_idx = lax.broadcasted_iota(jnp.int32, (c, LANES), 1) % c
    strict = s_idx < t_idx
    incl = s_idx <= t_idx
    blk = (lax.broadcasted_iota(jnp.int32, (LANES, LANES), 0) // HEAD
           == lax.broadcasted_iota(jnp.int32, (LANES, LANES), 1) // HEAD)
    eye_cat = (s_idx == t_idx).astype(F32)

    def chunk(ci, states):
        rows = pl.ds(pl.multiple_of(ci * c, c), c)
        tail_rows = pl.ds(pl.multiple_of(ci * c + c - 8, 8), 8)
        pairs = range(n_pairs)
        lanes = [slice(pi * LANES, (pi + 1) * LANES) for pi in pairs]
        each = lambda f: [f(pi) for pi in pairs]
        v = each(lambda pi: v_ref[rows, lanes[pi]])
        rh = each(lambda pi: rh_s[rows, lanes[pi]])
        ah = each(lambda pi: ah_s[rows, lanes[pi]])
        sc = each(lambda pi: _dot_nt(
            jnp.concatenate([ah[pi], rh[pi]], axis=0),
            jnp.concatenate([_head_stack(bt_s[rows, lanes[pi]], lane_head),
                             _head_stack(kt_s[rows, lanes[pi]], lane_head)], axis=0)))
        a_ak = each(lambda pi: jnp.where(strict, sc[pi][0:c, LANES:2 * LANES], 0.0).astype(BF16))
        a_rb = each(lambda pi: jnp.where(incl, sc[pi][c:2 * c, 0:LANES], 0.0).astype(BF16))
        a_rk = each(lambda pi: jnp.where(incl, sc[pi][c:2 * c, LANES:2 * LANES], 0.0).astype(BF16))

        x = each(lambda pi: jnp.where(strict, sc[pi][0:c, 0:LANES], 0.0))
        tinv = each(lambda pi: eye_cat + x[pi])
        p2 = 2
        while p2 < c:
            xb = each(lambda pi: x[pi].astype(BF16))
            x = each(lambda pi: _dot(xb[pi], _head_stack(xb[pi], lane_head)))
            tinv = each(lambda pi: tinv[pi] + _dot(tinv[pi].astype(BF16),
                                                   _head_stack(x[pi].astype(BF16), lane_head)))
            p2 *= 2
        t_cat = each(lambda pi: tinv[pi].astype(BF16))

        g2b = each(lambda pi: states[pi].astype(BF16))
        vs = each(lambda pi: _head_stack(v[pi], lane_head))
        p = each(lambda pi: _dot_nt(ah[pi], g2b[pi]) + _dot(a_ak[pi], vs[pi]))
        ub = each(lambda pi: _dot(t_cat[pi], _head_stack(p[pi].astype(BF16), lane_head)).astype(BF16))
        new_states = []
        for pi in pairs:
            y = (_dot_nt(rh[pi], g2b[pi]) + _dot(a_rb[pi], _head_stack(ub[pi], lane_head))
                 + _dot(a_rk[pi], vs[pi]))
            ybuf[rows, lanes[pi]] = y
            uv_t = jnp.concatenate([ub[pi], v[pi]], axis=0).T
            upd = _dot(uv_t, jnp.concatenate([bb_s[rows, lanes[pi]], kb_s[rows, lanes[pi]]], axis=0))
            decay = dec_s[tail_rows, lanes[pi]][7:8, :]
            new_states.append(states[pi] * decay + jnp.where(blk, upd, 0.0))
        return tuple(new_states)

    final = lax.fori_loop(0, n_chunks, chunk, tuple(state[pi] for pi in range(n_pairs)))
    for pi in range(n_pairs):
        state[pi] = final[pi]

    hb = hb_ref[...]
    gw = hb.shape[0]
    for l0 in range(0, width, gw):
        lanes = slice(l0, l0 + gw)
        y = ybuf[:, lanes]
        mean = _head_sums(y, hb) * (1.0 / HEAD)
        d = y - mean
        var = _head_sums(d * d, hb) * (1.0 / HEAD)
        yn = d * lax.rsqrt(var + RWKV_LN_EPS) * lng_ref[:, lanes] + lnb_ref[:, lanes]
        rk = r_ref[:, lanes].astype(F32) * k_ref[:, lanes].astype(F32) * rk_ref[:, lanes]
        bonus = _head_sums(rk, hb) * v_ref[:, lanes].astype(F32)
        o_ref[:, lanes] = ((yn + bonus) * _silu(g_ref[:, lanes].astype(F32))).astype(o_ref.dtype)


def _head_sums(x, m):
    return functools.reduce(jnp.add, [_dot(p, m) for p in _bf16_pieces(x, SUM_PIECES)])


RWKV_PAIRS = 8


def _rwkv_scan(prep, u3, ln_g, ln_b, r_k, w, gate_col0):
    r, k, v, na, bv, ld = prep
    b, s, _ = r.shape
    ts = _tile(s, 512)
    width = LANES * _tile(w // LANES, RWKV_PAIRS)
    gc0 = gate_col0 // width
    sb = _tile(ts, MXU_DIM)
    chunk_of = np.arange(sb) // RWKV_CHUNK
    same = chunk_of[:, None] == chunk_of[None, :]
    tri = np.arange(sb)[:, None] >= np.arange(sb)[None, :]
    cl = jnp.asarray(same & tri, BF16)
    cu = jnp.asarray(same & ~tri, BF16)
    gw = _tile(width, MXU_DIM)
    head_of = np.arange(gw) // HEAD
    hb = jnp.asarray(head_of[:, None] == head_of[None, :], BF16)
    row = pl.BlockSpec((None, ts, width), lambda i, p, t: (i, t, p))
    vec = pl.BlockSpec((1, width), lambda i, p, t: (0, p))
    const = lambda n: pl.BlockSpec((n, n), lambda i, p, t: (0, 0))
    tile_bf16 = pltpu.VMEM((ts, width), BF16)
    return pl.pallas_call(
        _rwkv_scan_kernel,
        grid=(b, w // width, s // ts),
        in_specs=[row] * 6 + [pl.BlockSpec((None, ts, width), lambda i, p, t: (i, t, gc0 + p)),
                              vec, vec, vec, const(sb), const(sb), const(gw)],
        out_specs=row,
        out_shape=jax.ShapeDtypeStruct((b, s, w), BF16),
        scratch_shapes=[pltpu.VMEM((width // LANES, LANES, LANES), F32), pltpu.VMEM((ts, width), F32),
                        pltpu.VMEM((ts, width), F32)] + [tile_bf16] * 6,
        compiler_params=_params(("parallel", "parallel", "arbitrary")),
        name="rwkv_scan",
    )(r, k, v, na, bv, ld, u3, ln_g, ln_b, r_k, cl, cu, hb)


FOX_BLOCK = 1024
FOX_SUB = 16
FOX_ACC_ROWS = HEAD + 16


def _fold8(x, op):
    parts = [x[i:i + 8, :] for i in range(0, x.shape[0], 8)]
    return functools.reduce(op, parts)


def _fox_kernel(qi_ref, kj_ref, q_ref, k_ref, v_ref, kc_ref, g_ref, o_ref,
                m_ref, acc_ref, qx_ref, s0_ref, s1_ref, p0_ref, p1_ref):
    step = pl.program_id(2)
    qi, kj = qi_ref[step], kj_ref[step]
    tq, tk = q_ref.shape[0], k_ref.shape[0]
    lane = lax.broadcasted_iota(jnp.int32, (1, LANES), 1)

    @pl.when(kj == 0)
    def _():
        m_ref[...] = jnp.full_like(m_ref, NEG_BIG)
        acc_ref[...] = jnp.zeros_like(acc_ref)
        q2 = q_ref[...]
        for h in range(2):
            first = _forget_lane(h, 0)
            ones = jnp.where((lane >= first) & (lane < first + C_PIECES), 1.0, 0.0).astype(BF16)
            qx_ref[h, :, 0:LANES] = jnp.where(lane // HEAD == h, q2, jnp.zeros_like(q2))
            qx_ref[h, :, LANES:2 * LANES] = jnp.broadcast_to(ones, q2.shape)

    s_refs, p_refs = (s0_ref, s1_ref), (p0_ref, p1_ref)

    def step_body(masked):
        kx = jnp.concatenate([k_ref[...], kc_ref[...]], axis=1)
        half = tk // 2
        block_max = [None, None]
        for h in range(2):
            if masked:
                s_refs[h][0:half, :] = _dot_nt(kx[0:half, :], qx_ref[h])
                s_refs[h][half:, half:] = _dot_nt(kx[half:, :], qx_ref[h, half:, :])
            else:
                s_val = _dot_nt(kx, qx_ref[h])
                s_refs[h][...] = s_val
                block_max[h] = _fold8(s_val, jnp.maximum)
        n_sub, n_grp = tk // FOX_SUB, tq // LANES

        def first_group(kb):
            return (kb * FOX_SUB) // LANES if masked else 0

        def group_scores(h, kb, g):
            blk = s_refs[h][kb * FOX_SUB:(kb + 1) * FOX_SUB, g * LANES:(g + 1) * LANES]
            if masked and g == first_group(kb):
                key = kb * FOX_SUB + lax.broadcasted_iota(jnp.int32, (FOX_SUB, LANES), 0)
                query = g * LANES + lax.broadcasted_iota(jnp.int32, (FOX_SUB, LANES), 1)
                blk = jnp.where(key <= query, blk, NEG_BIG)
            return blk

        def scores(h, kb, g0):
            return jnp.concatenate([group_scores(h, kb, g) for g in range(g0, n_grp)], axis=1)

        def running_max(h):
            if masked:
                mx = [None] * n_grp
                for kb in range(n_sub):
                    g0 = first_group(kb)
                    part = _fold8(scores(h, kb, g0), jnp.maximum)
                    for g in range(g0, n_grp):
                        piece = part[:, (g - g0) * LANES:(g - g0 + 1) * LANES]
                        mx[g] = piece if mx[g] is None else jnp.maximum(mx[g], piece)
                mx8 = jnp.concatenate(mx, axis=1)
            else:
                mx8 = block_max[h]
            m_prev = m_ref[h]
            m_new = jnp.maximum(m_prev, jnp.max(mx8, axis=0, keepdims=True))
            m_ref[h] = m_new
            return m_new, jnp.exp2(m_prev - m_new)

        def probabilities(h, m_new):
            for kb in range(n_sub):
                g0 = first_group(kb)
                rows = slice(kb * FOX_SUB, (kb + 1) * FOX_SUB)
                p_refs[h][rows, g0 * LANES:] = jnp.exp2(scores(h, kb, g0) - m_new[:, g0 * LANES:]).astype(BF16)
                z0 = half if kb * FOX_SUB >= half else 0
                if g0 * LANES > z0:
                    p_refs[h][rows, z0:g0 * LANES] = jnp.zeros((FOX_SUB, g0 * LANES - z0), BF16)

        def accumulate(h, alpha):
            lhs = jnp.concatenate([v_t[h * HEAD:(h + 1) * HEAD, :], sum_rows], axis=0)
            if masked:
                acc_ref[h] = alpha * acc_ref[h] + _dot(lhs[:, 0:half], p_refs[h][0:half, :])
                acc_ref[h, :, half:] += _dot(lhs[:, half:], p_refs[h][half:, half:])
            else:
                acc_ref[h] = alpha * acc_ref[h] + _dot(lhs, p_refs[h][...])

        eye = (lax.broadcasted_iota(jnp.int32, (LANES, LANES), 0)
               == lax.broadcasted_iota(jnp.int32, (LANES, LANES), 1)).astype(BF16)
        v_t = _dot_nt(eye, v_ref[...]).astype(BF16)
        sum_rows = jnp.ones((FOX_ACC_ROWS - HEAD, tk), BF16)
        m0, alpha0 = running_max(0)
        probabilities(0, m0)
        m1, alpha1 = running_max(1)
        accumulate(0, alpha0)
        probabilities(1, m1)
        accumulate(1, alpha1)

    @pl.when(kj < qi)
    def _():
        step_body(False)

    @pl.when(kj == qi)
    def _():
        step_body(True)
        outs = [acc_ref[h, 0:HEAD, :] * (1.0 / acc_ref[h, HEAD:HEAD + 1, :]) for h in range(2)]
        out = jnp.concatenate(outs, axis=0).T
        o_ref[...] = (out * _silu(g_ref[...].astype(F32))).astype(o_ref.dtype)


def _fox_branch(u3, kc, w, q_col0, gate_col0):
    b, s, _ = u3.shape
    tq = tk = _tile(s, FOX_BLOCK)
    n_pairs = w // LANES
    pairs = [(i, j) for i in range(s // tq) for j in range(i + 1)]
    qi = jnp.asarray([p[0] for p in pairs], jnp.int32)
    kj = jnp.asarray([p[1] for p in pairs], jnp.int32)
    qc0, gc0 = q_col0 // LANES, gate_col0 // LANES
    grid_spec = pltpu.PrefetchScalarGridSpec(
        num_scalar_prefetch=2,
        grid=(b, n_pairs, len(pairs)),
        in_specs=[
            pl.BlockSpec((None, tq, LANES), lambda i, p, t, qi, kj: (i, qi[t], qc0 + p)),
            pl.BlockSpec((None, tk, LANES), lambda i, p, t, qi, kj: (i, kj[t], qc0 + n_pairs + p)),
            pl.BlockSpec((None, tk, LANES), lambda i, p, t, qi, kj: (i, kj[t], qc0 + 2 * n_pairs + p)),
            pl.BlockSpec((None, tk, LANES), lambda i, p, t, qi, kj: (i, kj[t], p)),
            pl.BlockSpec((None, tq, LANES), lambda i, p, t, qi, kj: (i, qi[t], gc0 + p)),
        ],
        out_specs=pl.BlockSpec((None, tq, LANES), lambda i, p, t, qi, kj: (i, qi[t], p)),
        scratch_shapes=[pltpu.VMEM((2, 1, tq), F32),
                        pltpu.VMEM((2, FOX_ACC_ROWS, tq), F32), pltpu.VMEM((2, tq, 2 * LANES), BF16),
                        pltpu.VMEM((tk, tq), F32), pltpu.VMEM((tk, tq), F32),
                        pltpu.VMEM((tk, tq), BF16), pltpu.VMEM((tk, tq), BF16)],
    )
    return pl.pallas_call(
        _fox_kernel,
        grid_spec=grid_spec,
        out_shape=jax.ShapeDtypeStruct((b, s, w), BF16),
        compiler_params=_params(("parallel", "parallel", "arbitrary")),
        name="fox_attention",
    )(qi, kj, u3, u3, u3, kc, u3)


def _merge_kernel(ya_ref, yb_ref, yc_ref, yd_ref, ma_ref, mb_ref, mc_ref, md_ref, bm_ref, wb_ref, o_ref):
    acc = None
    branches = ((ya_ref, ma_ref), (yb_ref, mb_ref), (yc_ref, mc_ref), (yd_ref, md_ref))
    for kbr, (y_ref, ml_ref) in enumerate(branches):
        proj = _dot(y_ref[...], wb_ref[kbr])
        gate = _sigmoid(ml_ref[...].astype(F32) + bm_ref[kbr:kbr + 1, :])
        acc = gate * proj if acc is None else acc + gate * proj
    o_ref[...] = acc.astype(o_ref.dtype)


def _merge(ys, u2, b_merge, w_branch_all, layer, ml_col0):
    t, w = ys[0].shape
    _, nb, _, d = w_branch_all.shape
    tm, tn = _tile(t, 1024), _tile(d, 512)
    nd = d // tn
    mc0 = ml_col0 // tn
    yspec = pl.BlockSpec((tm, w), lambda i, j: (i, 0))
    mspec = lambda k: pl.BlockSpec((tm, tn), lambda i, j, k=k: (i, mc0 + k * nd + j))
    return pl.pallas_call(
        _merge_kernel,
        grid=(t // tm, nd),
        in_specs=[yspec] * 4 + [mspec(k) for k in range(4)] + [
            pl.BlockSpec((nb, tn), lambda i, j: (0, j)),
            pl.BlockSpec((None, nb, w, tn), lambda i, j: (layer, 0, 0, j)),
        ],
        out_specs=pl.BlockSpec((tm, tn), lambda i, j: (i, j)),
        out_shape=jax.ShapeDtypeStruct((t, d), BF16),
        compiler_params=_params(("parallel", "arbitrary")),
        name="merge",
    )(*ys, u2, u2, u2, u2, b_merge, w_branch_all)


def _outproj_kernel(m_ref, w_ref, x_ref, o_ref):
    o_ref[...] = x_ref[...] + _dot(m_ref[...], w_ref[...])


def _outproj(merged, w_out_all, layer, x2):
    t, d = x2.shape
    tm, tn = _tile(t, 1024), _tile(d, 512)
    return pl.pallas_call(
        _outproj_kernel,
        grid=(t // tm, d // tn),
        in_specs=[
            pl.BlockSpec((tm, d), lambda i, j: (i, 0)),
            pl.BlockSpec((None, d, tn), lambda i, j: (layer, 0, j)),
            pl.BlockSpec((tm, tn), lambda i, j: (i, j)),
        ],
        out_specs=pl.BlockSpec((tm, tn), lambda i, j: (i, j)),
        out_shape=jax.ShapeDtypeStruct((t, d), F32),
        compiler_params=_params(("parallel", "arbitrary")),
        name="outproj",
    )(merged, w_out_all, x2)


def _final_norm_kernel(x_ref, g_ref, o_ref):
    o_ref[...] = _rms_rows(x_ref[...], g_ref[...])


def _final_norm(x2, g):
    t, d = x2.shape
    tm = _tile(t, 512)
    return pl.pallas_call(
        _final_norm_kernel,
        grid=(t // tm,),
        in_specs=[pl.BlockSpec((tm, d), lambda i: (i, 0)), pl.BlockSpec((1, d), lambda i: (0, 0))],
        out_specs=pl.BlockSpec((tm, d), lambda i: (i, 0)),
        out_shape=jax.ShapeDtypeStruct((t, d), F32),
        compiler_params=_params(("parallel",)),
        name="final_norm",
    )(x2, g)


def _split_in_weights(w_in, w, nh):
    w_t = jnp.swapaxes(w_in, 1, 2)
    lo0, q0, f0 = 7 * w, 8 * w + 2 * LORA, 11 * w + 2 * LORA
    cast = lambda a: a.astype(BF16)
    w_main = jnp.concatenate([cast(w_t[:, :lo0]), cast(w_t[:, lo0 + 2 * LORA:q0]),
                              cast(w_t[:, q0:q0 + w] * (HEAD ** -0.5 * LOG2E)),
                              cast(w_t[:, q0 + w:f0]), cast(w_t[:, f0 + nh:])], axis=1)
    pad = jnp.zeros((w_t.shape[0], LANES - nh, w_t.shape[2]), w_t.dtype)
    w_small = jnp.concatenate([w_t[:, lo0:lo0 + 2 * LORA], w_t[:, f0:f0 + nh], pad], axis=1).astype(BF16)
    return w_main, w_small


def _layer(x2, bsz, seq, layer, w_main_all, w_small, w_branch_all, w_out_all, norm_g, b_merge, conv_w, rwkv_mu,
           rwkv_w0, rwkv_w2, rwkv_a0, rwkv_a2, rwkv_kk, rwkv_ka, rwkv_rk, rwkv_ln_g, rwkv_ln_b, fox_bf, pool_w,
           pool_scale):
    t, d = x2.shape
    w = conv_w.shape[1]
    nh = fox_bf.shape[0]
    b_f = jnp.concatenate([fox_bf, jnp.zeros((LANES - nh,), fox_bf.dtype)]).reshape(1, LANES)
    row = lambda a: a.reshape(1, -1)

    u2 = _inproj(x2, row(norm_g), w_main_all, layer)
    u3 = u2.reshape(bsz, seq, -1)
    lora, kc = _side(x2.reshape(bsz, seq, d), row(norm_g), w_small, b_f, w)
    y_a = _conv_branch(u3, conv_w, w, 0)
    prep = _rwkv_prep(u3, lora, rwkv_mu[:3 * w].reshape(3, w), row(rwkv_mu[3 * w:]), row(rwkv_w0),
                      rwkv_w2.astype(BF16), row(rwkv_a0), rwkv_a2.astype(BF16), row(rwkv_kk), row(rwkv_ka),
                      w, 4 * w)
    y_b = _rwkv_scan(prep, u3, row(rwkv_ln_g), row(rwkv_ln_b), row(rwkv_rk), w, 7 * w)
    y_c = _fox_branch(u3, kc, w, 8 * w, 11 * w)
    y_d = _pool_branch(u3, pool_w.astype(BF16), row(pool_scale), w, 12 * w)
    ys = [y.reshape(t, w) for y in (y_a, y_b, y_c, y_d)]
    merged = _merge(ys, u2, b_merge, w_branch_all, layer, 14 * w)
    return _outproj(merged, w_out_all, layer, x2)


def kernel(x, norm_g, w_in, b_merge, conv_w, rwkv_mu, rwkv_w0, rwkv_w2, rwkv_a0, rwkv_a2, rwkv_kk, rwkv_ka,
           rwkv_rk, rwkv_ln_g, rwkv_ln_b, fox_bf, pool_w, pool_scale, w_branch, w_out, final_g):
    bsz, seq, d = x.shape
    x2 = x.reshape(bsz * seq, d)
    w_main_all, w_small_all = _split_in_weights(w_in, conv_w.shape[2], fox_bf.shape[1])
    w_branch_all = w_branch.astype(BF16)
    w_out_all = w_out.astype(BF16)
    for l in range(norm_g.shape[0]):
        x2 = _layer(x2, bsz, seq, l, w_main_all, w_small_all[l], w_branch_all, w_out_all, norm_g[l], b_merge[l],
                    conv_w[l], rwkv_mu[l], rwkv_w0[l], rwkv_w2[l], rwkv_a0[l], rwkv_a2[l], rwkv_kk[l], rwkv_ka[l],
                    rwkv_rk[l], rwkv_ln_g[l], rwkv_ln_b[l], fox_bf[l], pool_w[l], pool_scale[l])
    return _final_norm(x2, final_g.reshape(1, d)).reshape(bsz, seq, d)
```

```python
import functools

import jax
import jax.numpy as jnp
import numpy as np
from jax import lax
from jax.experimental import pallas as pl
from jax.experimental.pallas import tpu as pltpu

F32 = jnp.float32
BF16 = jnp.bfloat16
HIGHEST = lax.Precision.HIGHEST

NORM_EPS = 1e-6
RWKV_LN_EPS = 64e-5
HEAD = 64
LORA = 64
POOL_WINDOWS = (2, 4, 8, 16)
RWKV_CHUNK = 64
LANES = 128
MXU_DIM = 256
LOG2E = float(np.log2(np.e))
NEG_BIG = -1e30
VMEM_LIMIT = 48 * 1024 * 1024


def _params(sem):
    return pltpu.CompilerParams(dimension_semantics=sem, vmem_limit_bytes=VMEM_LIMIT)


def _tile(n, want):
    t = min(n, want)
    while n % t:
        t //= 2
    return t


def _sigmoid(x):
    return 1.0 / (1.0 + jnp.exp(-x))


def _silu(x):
    return x * _sigmoid(x)


def _rms_rows(xf, g):
    ms = jnp.mean(xf * xf, axis=-1, keepdims=True)
    return xf * lax.rsqrt(ms + NORM_EPS) * g


def _dot(a, b):
    return jnp.dot(a, b, preferred_element_type=F32)


def _dot_nt(a, b):
    return lax.dot_general(a, b, (((1,), (1,)), ((), ())), preferred_element_type=F32)


def _dot_f32(a, b):
    return jnp.dot(a, b, precision=HIGHEST, preferred_element_type=F32)


def _inproj_kernel(x_ref, g_ref, w_ref, o_ref, h_ref):
    @pl.when(pl.program_id(1) == 0)
    def _():
        h_ref[...] = _rms_rows(x_ref[...], g_ref[...]).astype(BF16)

    o_ref[...] = _dot_nt(h_ref[...], w_ref[...]).astype(o_ref.dtype)


def _inproj(x2, g, w_all, layer):
    t, d = x2.shape
    n = w_all.shape[1]
    tm, tn = _tile(t, 1024), _tile(n, 1024)
    return pl.pallas_call(
        _inproj_kernel,
        grid=(t // tm, n // tn),
        in_specs=[
            pl.BlockSpec((tm, d), lambda i, j: (i, 0)),
            pl.BlockSpec((1, d), lambda i, j: (0, 0)),
            pl.BlockSpec((None, tn, d), lambda i, j: (layer, j, 0)),
        ],
        out_specs=pl.BlockSpec((tm, tn), lambda i, j: (i, j)),
        out_shape=jax.ShapeDtypeStruct((t, n), BF16),
        scratch_shapes=[pltpu.VMEM((tm, d), BF16)],
        compiler_params=_params(("parallel", "arbitrary")),
        name="inproj",
    )(x2, g, w_all)


C_PIECES = 3


def _bf16_pieces(x, n=C_PIECES):
    pieces = []
    for _ in range(n):
        p = x.astype(BF16)
        pieces.append(p)
        x = x - p.astype(F32)
    return pieces


def _side_kernel(x_ref, g_ref, ws_ref, bf_ref, sel_ref, lora_ref, kc_ref, carry_ref):
    @pl.when(pl.program_id(1) == 0)
    def _():
        carry_ref[...] = jnp.zeros_like(carry_ref)

    h = _rms_rows(x_ref[...], g_ref[...]).astype(BF16)
    sf = _dot_nt(h, ws_ref[...])
    ns = lora_ref.shape[1]
    lora_ref[...] = sf[:, :ns]
    z = sf[:, ns:] + bf_ref[...]
    logf = jnp.minimum(z, 0.0) - jnp.log1p(jnp.exp(-jnp.abs(z)))
    ts = logf.shape[0]
    lower = (lax.broadcasted_iota(jnp.int32, (ts, ts), 0)
             >= lax.broadcasted_iota(jnp.int32, (ts, ts), 1)).astype(BF16)
    c = functools.reduce(jnp.add, [_dot(lower, p) for p in _bf16_pieces(logf)]) + carry_ref[...]
    carry_ref[...] = c[ts - 1:ts, :]
    kc = None
    for piece, sel in zip(_bf16_pieces(-LOG2E * c), (sel_ref[0], sel_ref[1], sel_ref[2])):
        term = _dot(piece, sel)
        kc = term if kc is None else kc + term
    kc_ref[...] = kc.astype(kc_ref.dtype)


def _forget_lane(head, piece):
    return C_PIECES * head + piece


def _side(x3, g, w_small, b_f, w):
    b, s, d = x3.shape
    ns = w_small.shape[0] - LANES
    ts = _tile(s, 512)
    nh = w // HEAD
    assert C_PIECES * nh <= LANES
    sel = np.zeros((C_PIECES, LANES, LANES), np.float32)
    for hd in range(nh):
        for piece in range(C_PIECES):
            sel[piece, hd, _forget_lane(hd, piece)] = 1.0
    return pl.pallas_call(
        _side_kernel,
        grid=(b, s // ts),
        in_specs=[
            pl.BlockSpec((None, ts, d), lambda i, j: (i, j, 0)),
            pl.BlockSpec((1, d), lambda i, j: (0, 0)),
            pl.BlockSpec((ns + LANES, d), lambda i, j: (0, 0)),
            pl.BlockSpec((1, LANES), lambda i, j: (0, 0)),
            pl.BlockSpec((C_PIECES, LANES, LANES), lambda i, j: (0, 0, 0)),
        ],
        out_specs=[
            pl.BlockSpec((None, ts, ns), lambda i, j: (i, j, 0)),
            pl.BlockSpec((None, ts, LANES), lambda i, j: (i, j, 0)),
        ],
        out_shape=[
            jax.ShapeDtypeStruct((b, s, ns), F32),
            jax.ShapeDtypeStruct((b, s, LANES), BF16),
        ],
        scratch_shapes=[pltpu.VMEM((1, LANES), F32)],
        compiler_params=_params(("parallel", "arbitrary")),
        name="side",
    )(x3, g, w_small, b_f, jnp.asarray(sel, BF16))


HALO = 8


def _conv_kernel(bg_ref, cg_ref, xv_ref, g_ref, cw_ref, o_ref, pbuf):
    ts = o_ref.shape[0]

    @pl.when(pl.program_id(2) == 0)
    def _():
        pbuf[0:HALO, :] = jnp.zeros((HALO, pbuf.shape[1]), F32)

    p = cg_ref[...].astype(F32) * xv_ref[...].astype(F32)
    pbuf[HALO:HALO + ts, :] = p
    cw = cw_ref[...]
    z = (cw[0:1] * pbuf[HALO - 2:HALO - 2 + ts, :] + cw[1:2] * pbuf[HALO - 1:HALO - 1 + ts, :]
         + cw[2:3] * p)
    o_ref[...] = (bg_ref[...].astype(F32) * z * _silu(g_ref[...].astype(F32))).astype(o_ref.dtype)
    pbuf[0:HALO, :] = pbuf[ts:ts + HALO, :]


def _conv_branch(u3, conv_w, w, col0):
    b, s, _ = u3.shape
    ts, tw = _tile(s, 512), _tile(w, 512)
    nw = w // tw
    c0 = col0 // tw

    def spec(k):
        return pl.BlockSpec((None, ts, tw), lambda i, j, t, k=k: (i, t, c0 + k * nw + j))

    return pl.pallas_call(
        _conv_kernel,
        grid=(b, nw, s // ts),
        in_specs=[spec(0), spec(1), spec(2), spec(3),
                  pl.BlockSpec((conv_w.shape[0], tw), lambda i, j, t: (0, j))],
        out_specs=pl.BlockSpec((None, ts, tw), lambda i, j, t: (i, t, j)),
        out_shape=jax.ShapeDtypeStruct((b, s, w), BF16),
        scratch_shapes=[pltpu.VMEM((HALO + ts, tw), F32)],
        compiler_params=_params(("parallel", "parallel", "arbitrary")),
        name="conv_branch",
    )(u3, u3, u3, u3, conv_w)


POOL_HALO = 16


def _pool_kernel(x_ref, g_ref, pw_ref, sc_ref, o_ref, xbuf):
    ts, w = o_ref.shape
    gw = w // len(POOL_WINDOWS)
    s = pl.program_id(1)

    @pl.when(s == 0)
    def _():
        xbuf[0:POOL_HALO, :] = jnp.zeros((POOL_HALO, w), F32)

    x = x_ref[...].astype(F32)
    xbuf[POOL_HALO:POOL_HALO + ts, :] = x
    pos = s * ts + lax.broadcasted_iota(jnp.int32, (ts, 1), 0)
    for gi, win in enumerate(POOL_WINDOWS):
        lo, hi = gi * gw, (gi + 1) * gw
        xi = x[:, lo:hi]
        acc = xi
        for k in range(1, win):
            acc = acc + xbuf[POOL_HALO - k:POOL_HALO - k + ts, lo:hi]
        count = jnp.minimum(pos + 1, win).astype(F32)
        pooled = acc / count - xi
        y = _dot(pooled.astype(BF16), pw_ref[gi])
        o_ref[:, lo:hi] = (y * sc_ref[:, lo:hi] * _silu(g_ref[:, lo:hi].astype(F32))).astype(o_ref.dtype)
    xbuf[0:POOL_HALO, :] = xbuf[ts:ts + POOL_HALO, :]


def _pool_branch(u3, pool_w, pool_scale, w, col0):
    b, s, _ = u3.shape
    ts = _tile(s, 512)
    c0 = col0 // w
    ng, gw, _ = pool_w.shape
    return pl.pallas_call(
        _pool_kernel,
        grid=(b, s // ts),
        in_specs=[
            pl.BlockSpec((None, ts, w), lambda i, t: (i, t, c0)),
            pl.BlockSpec((None, ts, w), lambda i, t: (i, t, c0 + 1)),
            pl.BlockSpec((ng, gw, gw), lambda i, t: (0, 0, 0)),
            pl.BlockSpec((1, w), lambda i, t: (0, 0)),
        ],
        out_specs=pl.BlockSpec((None, ts, w), lambda i, t: (i, t, 0)),
        out_shape=jax.ShapeDtypeStruct((b, s, w), BF16),
        scratch_shapes=[pltpu.VMEM((POOL_HALO + ts, w), F32)],
        compiler_params=_params(("parallel", "arbitrary")),
        name="pool_branch",
    )(u3, u3, pool_w, pool_scale)


def _rwkv_prep_kernel(r_ref, k_ref, v_ref, lo_ref, mu_ref, mul_ref, w0_ref, w2_ref, a0_ref, a2_ref,
                      kkw_ref, kaw_ref, hb_ref,
                      ro_ref, ko_ref, vo_ref, na_ref, bv_ref, ld_ref, buf, lbuf):
    ts, w = ro_ref.shape

    @pl.when(pl.program_id(1) == 0)
    def _():
        buf[0:HALO, :] = jnp.zeros((HALO, buf.shape[1]), F32)
        lbuf[0:HALO, :] = jnp.zeros((HALO, lbuf.shape[1]), F32)

    for idx, ref in enumerate((r_ref, k_ref, v_ref)):
        buf[HALO:HALO + ts, idx * w:(idx + 1) * w] = ref[...].astype(F32)
    lbuf[HALO:HALO + ts, :] = lo_ref[...]

    def mixed(cur, prev, mu):
        return cur + (prev - cur) * mu

    mu = mu_ref[...]
    r = mixed(buf[HALO:HALO + ts, 0:w], buf[HALO - 1:HALO - 1 + ts, 0:w], mu[0:1])
    k = mixed(buf[HALO:HALO + ts, w:2 * w], buf[HALO - 1:HALO - 1 + ts, w:2 * w], mu[1:2])
    v = mixed(buf[HALO:HALO + ts, 2 * w:3 * w], buf[HALO - 1:HALO - 1 + ts, 2 * w:3 * w], mu[2:3])
    lo = mixed(lbuf[HALO:HALO + ts, :], lbuf[HALO - 1:HALO - 1 + ts, :], mul_ref[...])
    wl, al = lo[:, 0:LORA], lo[:, LORA:2 * LORA]

    z = w0_ref[...] + _dot(jnp.tanh(wl).astype(BF16), w2_ref[...])
    ld_ref[...] = -float(np.exp(-0.5)) * _sigmoid(z)
    a = _sigmoid(a0_ref[...] + _dot(al.astype(BF16), a2_ref[...]))

    kk = k * kkw_ref[...]
    hb = hb_ref[...]
    gw = hb.shape[0]
    kk2 = kk * kk
    n2 = jnp.concatenate([_head_sums(kk2[:, l0:l0 + gw], hb) for l0 in range(0, w, gw)], axis=1)
    kk = kk * lax.rsqrt(jnp.maximum(n2, 1e-24))
    ro_ref[...] = r.astype(ro_ref.dtype)
    ko_ref[...] = (k * (1.0 + (a - 1.0) * kaw_ref[...])).astype(ko_ref.dtype)
    vo_ref[...] = v.astype(vo_ref.dtype)
    na_ref[...] = (-kk).astype(na_ref.dtype)
    bv_ref[...] = (kk * a).astype(bv_ref.dtype)

    buf[0:HALO, :] = buf[ts:ts + HALO, :]
    lbuf[0:HALO, :] = lbuf[ts:ts + HALO, :]


def _rwkv_prep(u3, lora, mu3, mul, w0, w2, a0, a2, kkw, kaw, w, col0):
    b, s, _ = u3.shape
    ts = _tile(s, 256)
    c0 = col0 // w
    gw = _tile(w, MXU_DIM)
    head_of = np.arange(gw) // HEAD
    hb = jnp.asarray(head_of[:, None] == head_of[None, :], BF16)
    row = lambda k: pl.BlockSpec((None, ts, w), lambda i, t, k=k: (i, t, c0 + k))
    full = lambda shape: pl.BlockSpec(shape, lambda i, t: (0,) * len(shape))
    outw = pl.BlockSpec((None, ts, w), lambda i, t: (i, t, 0))
    sds = lambda dt: jax.ShapeDtypeStruct((b, s, w), dt)
    return pl.pallas_call(
        _rwkv_prep_kernel,
        grid=(b, s // ts),
        in_specs=[row(0), row(1), row(2),
                  pl.BlockSpec((None, ts, 2 * LORA), lambda i, t: (i, t, 0)),
                  full((3, w)), full((1, 2 * LORA)), full((1, w)), full((LORA, w)),
                  full((1, w)), full((LORA, w)), full((1, w)), full((1, w)),
                  full((gw, gw))],
        out_specs=[outw] * 6,
        out_shape=[sds(BF16)] * 5 + [sds(F32)],
        scratch_shapes=[pltpu.VMEM((HALO + ts, 3 * w), F32), pltpu.VMEM((HALO + ts, 2 * LORA), F32)],
        compiler_params=_params(("parallel", "arbitrary")),
        name="rwkv_prep",
    )(u3, u3, u3, lora, mu3, mul, w0, w2, a0, a2, kkw, kaw, hb)


def _head_stack(x, lane_head):
    zero = jnp.zeros_like(x)
    return jnp.concatenate([jnp.where(lane_head == 0, x, zero), jnp.where(lane_head == 1, x, zero)], axis=0)


SUM_PIECES = 2


def _rwkv_scan_kernel(r_ref, k_ref, v_ref, na_ref, bv_ref, ld_ref, g_ref, lng_ref, lnb_ref, rk_ref,
                      cl_ref, cu_ref, hb_ref, o_ref, state, ybuf, dec_s, rh_s, ah_s, bt_s, kt_s, bb_s, kb_s):
    ts, width = o_ref.shape
    n_pairs = width // LANES
    c = RWKV_CHUNK
    n_chunks = ts // c

    @pl.when(pl.program_id(2) == 0)
    def _():
        state[...] = jnp.zeros_like(state)

    sb = cl_ref.shape[0]
    lcs, lrests = [], []
    for r0 in range(0, ts, sb):
        pieces = jnp.concatenate(_bf16_pieces(ld_ref[r0:r0 + sb, :], SUM_PIECES), axis=1)
        for m_ref, outs in ((cl_ref, lcs), (cu_ref, lrests)):
            both = _dot(m_ref[...], pieces)
            outs.append(both[:, 0:width] + both[:, width:2 * width])
    ld = ld_ref[...]
    lc = jnp.concatenate(lcs, axis=0)
    lrest = jnp.concatenate(lrests, axis=0)
    e_in = jnp.exp(lc)
    e_neg = jnp.exp(-lc)
    e_end = jnp.exp(lrest)
    r_all = r_ref[...].astype(F32)
    k_all = k_ref[...].astype(F32)
    na_all = na_ref[...].astype(F32)
    bv_all = bv_ref[...].astype(F32)
    dec_s[...] = e_in
    rh_s[...] = (r_all * e_in).astype(BF16)
    ah_s[...] = (na_all * jnp.exp(lc - ld)).astype(BF16)
    bt_s[...] = (bv_all * e_neg).astype(BF16)
    kt_s[...] = (k_all * e_neg).astype(BF16)
    bb_s[...] = (bv_all * e_end).astype(BF16)
    kb_s[...] = (k_all * e_end).astype(BF16)

    lane_head = lax.broadcasted_iota(jnp.int32, (1, LANES), 1) // HEAD
    t_idx = lax.broadcasted_iota(jnp.int32, (c, LANES), 0)
    s_idx = lax.broadcasted_iota(jnp.int32, (c, LANES), 1) % c
    strict = s_idx < t_idx
    incl = s_idx <= t_idx
    blk = (lax.broadcasted_iota(jnp.int32, (LANES, LANES), 0) // HEAD
           == lax.broadcasted_iota(jnp.int32, (LANES, LANES), 1) // HEAD)
    eye_cat = (s_idx == t_idx).astype(F32)

    def chunk(ci, states):
        rows = pl.ds(pl.multiple_of(ci * c, c), c)
        tail_rows = pl.ds(pl.multiple_of(ci * c + c - 8, 8), 8)
        pairs = range(n_pairs)
        lanes = [slice(pi * LANES, (pi + 1) * LANES) for pi in pairs]
        each = lambda f: [f(pi) for pi in pairs]
        v = each(lambda pi: v_ref[rows, lanes[pi]])
        rh = each(lambda pi: rh_s[rows, lanes[pi]])
        ah = each(lambda pi: ah_s[rows, lanes[pi]])
        sc = each(lambda pi: _dot_nt(
            jnp.concatenate([ah[pi], rh[pi]], axis=0),
            jnp.concatenate([_head_stack(bt_s[rows, lanes[pi]], lane_head),
                             _head_stack(kt_s[rows, lanes[pi]], lane_head)], axis=0)))
        a_ak = each(lambda pi: jnp.where(strict, sc[pi][0:c, LANES:2 * LANES], 0.0).astype(BF16))
        a_rb = each(lambda pi: jnp.where(incl, sc[pi][c:2 * c, 0:LANES], 0.0).astype(BF16))
        a_rk = each(lambda pi: jnp.where(incl, sc[pi][c:2 * c, LANES:2 * LANES], 0.0).astype(BF16))

        x = each(lambda pi: jnp.where(strict, sc[pi][0:c, 0:LANES], 0.0))
        tinv = each(lambda pi: eye_cat + x[pi])
        p2 = 2
        while p2 < c:
            xb = each(lambda pi: x[pi].astype(BF16))
            x = each(lambda pi: _dot(xb[pi], _head_stack(xb[pi], lane_head)))
            tinv = each(lambda pi: tinv[pi] + _dot(tinv[pi].astype(BF16),
                                                   _head_stack(x[pi].astype(BF16), lane_head)))
            p2 *= 2
        t_cat = each(lambda pi: tinv[pi].astype(BF16))

        g2b = each(lambda pi: states[pi].astype(BF16))
        vs = each(lambda pi: _head_stack(v[pi], lane_head))
        p = each(lambda pi: _dot_nt(ah[pi], g2b[pi]) + _dot(a_ak[pi], vs[pi]))
        ub = each(lambda pi: _dot(t_cat[pi], _head_stack(p[pi].astype(BF16), lane_head)).astype(BF16))
        new_states = []
        for pi in pairs:
            y = (_dot_nt(rh[pi], g2b[pi]) + _dot(a_rb[pi], _head_stack(ub[pi], lane_head))
                 + _dot(a_rk[pi], vs[pi]))
            ybuf[rows, lanes[pi]] = y
            uv_t = jnp.concatenate([ub[pi], v[pi]], axis=0).T
            upd = _dot(uv_t, jnp.concatenate([bb_s[rows, lanes[pi]], kb_s[rows, lanes[pi]]], axis=0))
            decay = dec_s[tail_rows, lanes[pi]][7:8, :]
            new_states.append(states[pi] * decay + jnp.where(blk, upd, 0.0))
        return tuple(new_states)

    final = lax.fori_loop(0, n_chunks, chunk, tuple(state[pi] for pi in range(n_pairs)))
    for pi in range(n_pairs):
        state[pi] = final[pi]

    hb = hb_ref[...]
    gw = hb.shape[0]
    for l0 in range(0, width, gw):
        lanes = slice(l0, l0 + gw)
        y = ybuf[:, lanes]
        mean = _head_sums(y, hb) * (1.0 / HEAD)
        d = y - mean
        var = _head_sums(d * d, hb) * (1.0 / HEAD)
        yn = d * lax.rsqrt(var + RWKV_LN_EPS) * lng_ref[:, lanes] + lnb_ref[:, lanes]
        rk = r_ref[:, lanes].astype(F32) * k_ref[:, lanes].astype(F32) * rk_ref[:, lanes]
        bonus = _head_sums(rk, hb) * v_ref[:, lanes].astype(F32)
        o_ref[:, lanes] = ((yn + bonus) * _silu(g_ref[:, lanes].astype(F32))).astype(o_ref.dtype)


def _head_sums(x, m):
    return functools.reduce(jnp.add, [_dot(p, m) for p in _bf16_pieces(x, SUM_PIECES)])


RWKV_PAIRS = 8


def _rwkv_scan(prep, u3, ln_g, ln_b, r_k, w, gate_col0):
    r, k, v, na, bv, ld = prep
    b, s, _ = r.shape
    ts = _tile(s, 512)
    width = LANES * _tile(w // LANES, RWKV_PAIRS)
    gc0 = gate_col0 // width
    sb = _tile(ts, MXU_DIM)
    chunk_of = np.arange(sb) // RWKV_CHUNK
    same = chunk_of[:, None] == chunk_of[None, :]
    tri = np.arange(sb)[:, None] >= np.arange(sb)[None, :]
    cl = jnp.asarray(same & tri, BF16)
    cu = jnp.asarray(same & ~tri, BF16)
    gw = _tile(width, MXU_DIM)
    head_of = np.arange(gw) // HEAD
    hb = jnp.asarray(head_of[:, None] == head_of[None, :], BF16)
    row = pl.BlockSpec((None, ts, width), lambda i, p, t: (i, t, p))
    vec = pl.BlockSpec((1, width), lambda i, p, t: (0, p))
    const = lambda n: pl.BlockSpec((n, n), lambda i, p, t: (0, 0))
    tile_bf16 = pltpu.VMEM((ts, width), BF16)
    return pl.pallas_call(
        _rwkv_scan_kernel,
        grid=(b, w // width, s // ts),
        in_specs=[row] * 6 + [pl.BlockSpec((None, ts, width), lambda i, p, t: (i, t, gc0 + p)),
                              vec, vec, vec, const(sb), const(sb), const(gw)],
        out_specs=row,
        out_shape=jax.ShapeDtypeStruct((b, s, w), BF16),
        scratch_shapes=[pltpu.VMEM((width // LANES, LANES, LANES), F32), pltpu.VMEM((ts, width), F32),
                        pltpu.VMEM((ts, width), F32)] + [tile_bf16] * 6,
        compiler_params=_params(("parallel", "parallel", "arbitrary")),
        name="rwkv_scan",
    )(r, k, v, na, bv, ld, u3, ln_g, ln_b, r_k, cl, cu, hb)


FOX_BLOCK = 1024
FOX_SUB = 16
FOX_ACC_ROWS = HEAD + 16


def _fold8(x, op):
    parts = [x[i:i + 8, :] for i in range(0, x.shape[0], 8)]
    return functools.reduce(op, parts)


def _fox_kernel(qi_ref, kj_ref, q_ref, k_ref, v_ref, kc_ref, g_ref, o_ref,
                m_ref, acc_ref, qx_ref, s0_ref, s1_ref, p0_ref, p1_ref):
    pair, step = pl.program_id(1), pl.program_id(2)
    qi, kj = qi_ref[step], kj_ref[step]
    tq, tk = q_ref.shape[0], k_ref.shape[0]
    lane = lax.broadcasted_iota(jnp.int32, (1, LANES), 1)

    @pl.when(kj == 0)
    def _():
        m_ref[...] = jnp.full_like(m_ref, NEG_BIG)
        acc_ref[...] = jnp.zeros_like(acc_ref)
        q2 = q_ref[...]
        for h in range(2):
            first = _forget_lane(2 * pair + h, 0)
            ones = jnp.where((lane >= first) & (lane < first + C_PIECES), 1.0, 0.0).astype(BF16)
            qx_ref[h, :, 0:LANES] = jnp.where(lane // HEAD == h, q2, jnp.zeros_like(q2))
            qx_ref[h, :, LANES:2 * LANES] = jnp.broadcast_to(ones, q2.shape)

    s_refs, p_refs = (s0_ref, s1_ref), (p0_ref, p1_ref)

    def step_body(masked):
        kx = jnp.concatenate([k_ref[...], kc_ref[...]], axis=1)
        half = tk // 2
        block_max = [None, None]
        for h in range(2):
            if masked:
                s_refs[h][0:half, :] = _dot_nt(kx[0:half, :], qx_ref[h])
                s_refs[h][half:, half:] = _dot_nt(kx[half:, :], qx_ref[h, half:, :])
            else:
                s_val = _dot_nt(kx, qx_ref[h])
                s_refs[h][...] = s_val
                block_max[h] = _fold8(s_val, jnp.maximum)
        n_sub, n_grp = tk // FOX_SUB, tq // LANES

        def first_group(kb):
            return (kb * FOX_SUB) // LANES if masked else 0

        def group_scores(h, kb, g):
            blk = s_refs[h][kb * FOX_SUB:(kb + 1) * FOX_SUB, g * LANES:(g + 1) * LANES]
            if masked and g == first_group(kb):
                key = kb * FOX_SUB + lax.broadcasted_iota(jnp.int32, (FOX_SUB, LANES), 0)
                query = g * LANES + lax.broadcasted_iota(jnp.int32, (FOX_SUB, LANES), 1)
                blk = jnp.where(key <= query, blk, NEG_BIG)
            return blk

        def scores(h, kb, g0):
            return jnp.concatenate([group_scores(h, kb, g) for g in range(g0, n_grp)], axis=1)

        def running_max(h):
            if masked:
                mx = [None] * n_grp
                for kb in range(n_sub):
                    g0 = first_group(kb)
                    part = _fold8(scores(h, kb, g0), jnp.maximum)
                    for g in range(g0, n_grp):
                        piece = part[:, (g - g0) * LANES:(g - g0 + 1) * LANES]
                        mx[g] = piece if mx[g] is None else jnp.maximum(mx[g], piece)
                mx8 = jnp.concatenate(mx, axis=1)
            else:
                mx8 = block_max[h]
            m_prev = m_ref[h]
            m_new = jnp.maximum(m_prev, jnp.max(mx8, axis=0, keepdims=True))
            m_ref[h] = m_new
            return m_new, jnp.exp2(m_prev - m_new)

        def probabilities(h, m_new):
            for kb in range(n_sub):
                g0 = first_group(kb)
                rows = slice(kb * FOX_SUB, (kb + 1) * FOX_SUB)
                p_refs[h][rows, g0 * LANES:] = jnp.exp2(scores(h, kb, g0) - m_new[:, g0 * LANES:]).astype(BF16)
                z0 = half if kb * FOX_SUB >= half else 0
                if g0 * LANES > z0:
                    p_refs[h][rows, z0:g0 * LANES] = jnp.zeros((FOX_SUB, g0 * LANES - z0), BF16)

        def accumulate(h, alpha):
            lhs = jnp.concatenate([v_t[h * HEAD:(h + 1) * HEAD, :], sum_rows], axis=0)
            if masked:
                acc_ref[h] = alpha * acc_ref[h] + _dot(lhs[:, 0:half], p_refs[h][0:half, :])
                acc_ref[h, :, half:] += _dot(lhs[:, half:], p_refs[h][half:, half:])
            else:
                acc_ref[h] = alpha * acc_ref[h] + _dot(lhs, p_refs[h][...])

        eye = (lax.broadcasted_iota(jnp.int32, (LANES, LANES), 0)
               == lax.broadcasted_iota(jnp.int32, (LANES, LANES), 1)).astype(BF16)
        v_t = _dot_nt(eye, v_ref[...]).astype(BF16)
        sum_rows = jnp.ones((FOX_ACC_ROWS - HEAD, tk), BF16)
        m0, alpha0 = running_max(0)
        probabilities(0, m0)
        m1, alpha1 = running_max(1)
        accumulate(0, alpha0)
        probabilities(1, m1)
        accumulate(1, alpha1)

    @pl.when(kj < qi)
    def _():
        step_body(False)

    @pl.when(kj == qi)
    def _():
        step_body(True)
        outs = [acc_ref[h, 0:HEAD, :] * (1.0 / acc_ref[h, HEAD:HEAD + 1, :]) for h in range(2)]
        out = jnp.concatenate(outs, axis=0).T
        o_ref[...] = (out * _silu(g_ref[...].astype(F32))).astype(o_ref.dtype)


def _fox_branch(u3, kc, w, q_col0, gate_col0):
    b, s, _ = u3.shape
    tq = tk = _tile(s, FOX_BLOCK)
    n_pairs = w // LANES
    pairs = [(i, j) for i in range(s // tq) for j in range(i + 1)]
    qi = jnp.asarray([p[0] for p in pairs], jnp.int32)
    kj = jnp.asarray([p[1] for p in pairs], jnp.int32)
    qc0, gc0 = q_col0 // LANES, gate_col0 // LANES
    grid_spec = pltpu.PrefetchScalarGridSpec(
        num_scalar_prefetch=2,
        grid=(b, n_pairs, len(pairs)),
        in_specs=[
            pl.BlockSpec((None, tq, LANES), lambda i, p, t, qi, kj: (i, qi[t], qc0 + p)),
            pl.BlockSpec((None, tk, LANES), lambda i, p, t, qi, kj: (i, kj[t], qc0 + n_pairs + p)),
            pl.BlockSpec((None, tk, LANES), lambda i, p, t, qi, kj: (i, kj[t], qc0 + 2 * n_pairs + p)),
            pl.BlockSpec((None, tk, LANES), lambda i, p, t, qi, kj: (i, kj[t], 0)),
            pl.BlockSpec((None, tq, LANES), lambda i, p, t, qi, kj: (i, qi[t], gc0 + p)),
        ],
        out_specs=pl.BlockSpec((None, tq, LANES), lambda i, p, t, qi, kj: (i, qi[t], p)),
        scratch_shapes=[pltpu.VMEM((2, 1, tq), F32),
                        pltpu.VMEM((2, FOX_ACC_ROWS, tq), F32), pltpu.VMEM((2, tq, 2 * LANES), BF16),
                        pltpu.VMEM((tk, tq), F32), pltpu.VMEM((tk, tq), F32),
                        pltpu.VMEM((tk, tq), BF16), pltpu.VMEM((tk, tq), BF16)],
    )
    return pl.pallas_call(
        _fox_kernel,
        grid_spec=grid_spec,
        out_shape=jax.ShapeDtypeStruct((b, s, w), BF16),
        compiler_params=_params(("parallel", "parallel", "arbitrary")),
        name="fox_attention",
    )(qi, kj, u3, u3, u3, kc, u3)


def _merge_kernel(ya_ref, yb_ref, yc_ref, yd_ref, ma_ref, mb_ref, mc_ref, md_ref, bm_ref, wb_ref, o_ref):
    acc = None
    branches = ((ya_ref, ma_ref), (yb_ref, mb_ref), (yc_ref, mc_ref), (yd_ref, md_ref))
    for kbr, (y_ref, ml_ref) in enumerate(branches):
        proj = _dot(y_ref[...], wb_ref[kbr])
        gate = _sigmoid(ml_ref[...].astype(F32) + bm_ref[kbr:kbr + 1, :])
        acc = gate * proj if acc is None else acc + gate * proj
    o_ref[...] = acc.astype(o_ref.dtype)


def _merge(ys, u2, b_merge, w_branch_all, layer, ml_col0):
    t, w = ys[0].shape
    _, nb, _, d = w_branch_all.shape
    tm, tn = _tile(t, 1024), _tile(d, 512)
    nd = d // tn
    mc0 = ml_col0 // tn
    yspec = pl.BlockSpec((tm, w), lambda i, j: (i, 0))
    mspec = lambda k: pl.BlockSpec((tm, tn), lambda i, j, k=k: (i, mc0 + k * nd + j))
    return pl.pallas_call(
        _merge_kernel,
        grid=(t // tm, nd),
        in_specs=[yspec] * 4 + [mspec(k) for k in range(4)] + [
            pl.BlockSpec((nb, tn), lambda i, j: (0, j)),
            pl.BlockSpec((None, nb, w, tn), lambda i, j: (layer, 0, 0, j)),
        ],
        out_specs=pl.BlockSpec((tm, tn), lambda i, j: (i, j)),
        out_shape=jax.ShapeDtypeStruct((t, d), BF16),
        compiler_params=_params(("parallel", "arbitrary")),
        name="merge",
    )(*ys, u2, u2, u2, u2, b_merge, w_branch_all)


def _outproj_kernel(m_ref, w_ref, x_ref, o_ref):
    o_ref[...] = x_ref[...] + _dot(m_ref[...], w_ref[...])


def _outproj(merged, w_out_all, layer, x2):
    t, d = x2.shape
    tm, tn = _tile(t, 1024), _tile(d, 512)
    return pl.pallas_call(
        _outproj_kernel,
        grid=(t // tm, d // tn),
        in_specs=[
            pl.BlockSpec((tm, d), lambda i, j: (i, 0)),
            pl.BlockSpec((None, d, tn), lambda i, j: (layer, 0, j)),
            pl.BlockSpec((tm, tn), lambda i, j: (i, j)),
        ],
        out_specs=pl.BlockSpec((tm, tn), lambda i, j: (i, j)),
        out_shape=jax.ShapeDtypeStruct((t, d), F32),
        compiler_params=_params(("parallel", "arbitrary")),
        name="outproj",
    )(merged, w_out_all, x2)


def _final_norm_kernel(x_ref, g_ref, o_ref):
    o_ref[...] = _rms_rows(x_ref[...], g_ref[...])


def _final_norm(x2, g):
    t, d = x2.shape
    tm = _tile(t, 512)
    return pl.pallas_call(
        _final_norm_kernel,
        grid=(t // tm,),
        in_specs=[pl.BlockSpec((tm, d), lambda i: (i, 0)), pl.BlockSpec((1, d), lambda i: (0, 0))],
        out_specs=pl.BlockSpec((tm, d), lambda i: (i, 0)),
        out_shape=jax.ShapeDtypeStruct((t, d), F32),
        compiler_params=_params(("parallel",)),
        name="final_norm",
    )(x2, g)


def _split_in_weights(w_in, w, nh):
    w_t = jnp.swapaxes(w_in, 1, 2)
    lo0, q0, f0 = 7 * w, 8 * w + 2 * LORA, 11 * w + 2 * LORA
    cast = lambda a: a.astype(BF16)
    w_main = jnp.concatenate([cast(w_t[:, :lo0]), cast(w_t[:, lo0 + 2 * LORA:q0]),
                              cast(w_t[:, q0:q0 + w] * (HEAD ** -0.5 * LOG2E)),
                              cast(w_t[:, q0 + w:f0]), cast(w_t[:, f0 + nh:])], axis=1)
    pad = jnp.zeros((w_t.shape[0], LANES - nh, w_t.shape[2]), w_t.dtype)
    w_small = jnp.concatenate([w_t[:, lo0:lo0 + 2 * LORA], w_t[:, f0:f0 + nh], pad], axis=1).astype(BF16)
    return w_main, w_small


def _layer(x2, bsz, seq, layer, w_main_all, w_small, w_branch_all, w_out_all, norm_g, b_merge, conv_w, rwkv_mu,
           rwkv_w0, rwkv_w2, rwkv_a0, rwkv_a2, rwkv_kk, rwkv_ka, rwkv_rk, rwkv_ln_g, rwkv_ln_b, fox_bf, pool_w,
           pool_scale):
    t, d = x2.shape
    w = conv_w.shape[1]
    nh = fox_bf.shape[0]
    b_f = jnp.concatenate([fox_bf, jnp.zeros((LANES - nh,), fox_bf.dtype)]).reshape(1, LANES)
    row = lambda a: a.reshape(1, -1)

    u2 = _inproj(x2, row(norm_g), w_main_all, layer)
    u3 = u2.reshape(bsz, seq, -1)
    lora, kc = _side(x2.reshape(bsz, seq, d), row(norm_g), w_small, b_f, w)
    y_a = _conv_branch(u3, conv_w, w, 0)
    prep = _rwkv_prep(u3, lora, rwkv_mu[:3 * w].reshape(3, w), row(rwkv_mu[3 * w:]), row(rwkv_w0),
                      rwkv_w2.astype(BF16), row(rwkv_a0), rwkv_a2.astype(BF16), row(rwkv_kk), row(rwkv_ka),
                      w, 4 * w)
    y_b = _rwkv_scan(prep, u3, row(rwkv_ln_g), row(rwkv_ln_b), row(rwkv_rk), w, 7 * w)
    y_c = _fox_branch(u3, kc, w, 8 * w, 11 * w)
    y_d = _pool_branch(u3, pool_w.astype(BF16), row(pool_scale), w, 12 * w)
    ys = [y.reshape(t, w) for y in (y_a, y_b, y_c, y_d)]
    merged = _merge(ys, u2, b_merge, w_branch_all, layer, 14 * w)
    return _outproj(merged, w_out_all, layer, x2)


def kernel(x, norm_g, w_in, b_merge, conv_w, rwkv_mu, rwkv_w0, rwkv_w2, rwkv_a0, rwkv_a2, rwkv_kk, rwkv_ka,
           rwkv_rk, rwkv_ln_g, rwkv_ln_b, fox_bf, pool_w, pool_scale, w_branch, w_out, final_g):
    bsz, seq, d = x.shape
    x2 = x.reshape(bsz * seq, d)
    w_main_all, w_small_all = _split_in_weights(w_in, conv_w.shape[2], fox_bf.shape[1])
    w_branch_all = w_branch.astype(BF16)
    w_out_all = w_out.astype(BF16)
    for l in range(norm_g.shape[0]):
        x2 = _layer(x2, bsz, seq, l, w_main_all, w_small_all[l], w_branch_all, w_out_all, norm_g[l], b_merge[l],
                    conv_w[l], rwkv_mu[l], rwkv_w0[l], rwkv_w2[l], rwkv_a0[l], rwkv_a2[l], rwkv_kk[l], rwkv_ka[l],
                    rwkv_rk[l], rwkv_ln_g[l], rwkv_ln_b[l], fox_bf[l], pool_w[l], pool_scale[l])
    return _final_norm(x2, final_g.reshape(1, d)).reshape(bsz, seq, d)
```

```python
import functools

import jax
import jax.numpy as jnp
import numpy as np
from jax import lax
from jax.experimental import pallas as pl
from jax.experimental.pallas import tpu as pltpu

F32 = jnp.float32
BF16 = jnp.bfloat16
HIGHEST = lax.Precision.HIGHEST

NORM_EPS = 1e-6
RWKV_LN_EPS = 64e-5
HEAD = 64
LORA = 64
POOL_WINDOWS = (2, 4, 8, 16)
RWKV_CHUNK = 64
LANES = 128
MXU_DIM = 256
LOG2E = float(np.log2(np.e))
NEG_BIG = -1e30
VMEM_LIMIT = 48 * 1024 * 1024


def _params(sem):
    return pltpu.CompilerParams(dimension_semantics=sem, vmem_limit_bytes=VMEM_LIMIT)


def _tile(n, want):
    t = min(n, want)
    while n % t:
        t //= 2
    return t


def _sigmoid(x):
    return 1.0 / (1.0 + jnp.exp(-x))


def _silu(x):
    return x * _sigmoid(x)


def _rms_rows(xf, g):
    ms = jnp.mean(xf * xf, axis=-1, keepdims=True)
    return xf * lax.rsqrt(ms + NORM_EPS) * g


def _dot(a, b):
    return jnp.dot(a, b, preferred_element_type=F32)


def _dot_nt(a, b):
    return lax.dot_general(a, b, (((1,), (1,)), ((), ())), preferred_element_type=F32)


def _dot_f32(a, b):
    return jnp.dot(a, b, precision=HIGHEST, preferred_element_type=F32)


def _inproj_kernel(x_ref, g_ref, sc_ref, w_ref, o_ref, h_ref):
    @pl.when(pl.program_id(1) == 0)
    def _():
        h_ref[...] = _rms_rows(x_ref[...], g_ref[...]).astype(BF16)

    o_ref[...] = (_dot_nt(h_ref[...], w_ref[0]) * sc_ref[...]).astype(o_ref.dtype)


def _inproj(x2, g, w_all, layer, segments, col_scale):
    t, d = x2.shape
    n = col_scale.shape[1]
    bounds = [c0 for c0, _ in segments] + [n]
    tn = _tile(functools.reduce(np.gcd, np.diff(bounds)), 1024)
    tm = _tile(t, 1024)

    align = int(functools.reduce(np.gcd, [tn] + [r0 for _, r0 in segments[1:]]))

    def weight_row(j):
        row = j * tn
        for (c0, r0), (c1, r1) in zip(segments[:-1], segments[1:]):
            row = row + jnp.where(j * tn >= c1, (r1 - c1) - (r0 - c0), 0)
        return pl.multiple_of(row + (segments[0][1] - segments[0][0]), align)

    return pl.pallas_call(
        _inproj_kernel,
        grid=(t // tm, n // tn),
        in_specs=[
            pl.BlockSpec((tm, d), lambda i, j: (i, 0)),
            pl.BlockSpec((1, d), lambda i, j: (0, 0)),
            pl.BlockSpec((1, tn), lambda i, j: (0, j)),
            pl.BlockSpec((pl.Element(1), pl.Element(tn), pl.Element(d)),
                         lambda i, j: (layer, weight_row(j), 0)),
        ],
        out_specs=pl.BlockSpec((tm, tn), lambda i, j: (i, j)),
        out_shape=jax.ShapeDtypeStruct((t, n), BF16),
        scratch_shapes=[pltpu.VMEM((tm, d), BF16)],
        compiler_params=_params(("parallel", "arbitrary")),
        name="inproj",
    )(x2, g, col_scale, w_all)


C_PIECES = 3


def _bf16_pieces(x, n=C_PIECES):
    pieces = []
    for _ in range(n):
        p = x.astype(BF16)
        pieces.append(p)
        x = x - p.astype(F32)
    return pieces


def _side_kernel(x_ref, g_ref, ws_ref, bf_ref, sel_ref, lora_ref, kc_ref, carry_ref):
    @pl.when(pl.program_id(1) == 0)
    def _():
        carry_ref[...] = jnp.zeros_like(carry_ref)

    h = _rms_rows(x_ref[...], g_ref[...]).astype(BF16)
    sf = _dot_nt(h, ws_ref[...])
    ns = lora_ref.shape[1]
    lora_ref[...] = sf[:, :ns]
    z = sf[:, ns:] + bf_ref[...]
    logf = jnp.minimum(z, 0.0) - jnp.log1p(jnp.exp(-jnp.abs(z)))
    ts = logf.shape[0]
    lower = (lax.broadcasted_iota(jnp.int32, (ts, ts), 0)
             >= lax.broadcasted_iota(jnp.int32, (ts, ts), 1)).astype(BF16)
    c = functools.reduce(jnp.add, [_dot(lower, p) for p in _bf16_pieces(logf)]) + carry_ref[...]
    carry_ref[...] = c[ts - 1:ts, :]
    kc = None
    for piece, sel in zip(_bf16_pieces(-LOG2E * c), (sel_ref[0], sel_ref[1], sel_ref[2])):
        term = _dot(piece, sel)
        kc = term if kc is None else kc + term
    kc_ref[...] = kc.astype(kc_ref.dtype)


def _forget_lane(head, piece):
    return C_PIECES * head + piece


def _side(x3, g, w_small, b_f, w):
    b, s, d = x3.shape
    ns = w_small.shape[0] - LANES
    ts = _tile(s, 512)
    nh = w // HEAD
    assert C_PIECES * nh <= LANES
    sel = np.zeros((C_PIECES, LANES, LANES), np.float32)
    for hd in range(nh):
        for piece in range(C_PIECES):
            sel[piece, hd, _forget_lane(hd, piece)] = 1.0
    return pl.pallas_call(
        _side_kernel,
        grid=(b, s // ts),
        in_specs=[
            pl.BlockSpec((None, ts, d), lambda i, j: (i, j, 0)),
            pl.BlockSpec((1, d), lambda i, j: (0, 0)),
            pl.BlockSpec((ns + LANES, d), lambda i, j: (0, 0)),
            pl.BlockSpec((1, LANES), lambda i, j: (0, 0)),
            pl.BlockSpec((C_PIECES, LANES, LANES), lambda i, j: (0, 0, 0)),
        ],
        out_specs=[
            pl.BlockSpec((None, ts, ns), lambda i, j: (i, j, 0)),
            pl.BlockSpec((None, ts, LANES), lambda i, j: (i, j, 0)),
        ],
        out_shape=[
            jax.ShapeDtypeStruct((b, s, ns), F32),
            jax.ShapeDtypeStruct((b, s, LANES), BF16),
        ],
        scratch_shapes=[pltpu.VMEM((1, LANES), F32)],
        compiler_params=_params(("parallel", "arbitrary")),
        name="side",
    )(x3, g, w_small, b_f, jnp.asarray(sel, BF16))


HALO = 8


def _conv_kernel(bg_ref, cg_ref, xv_ref, g_ref, cw_ref, o_ref, pbuf):
    ts = o_ref.shape[0]

    @pl.when(pl.program_id(2) == 0)
    def _():
        pbuf[0:HALO, :] = jnp.zeros((HALO, pbuf.shape[1]), F32)

    p = cg_ref[...].astype(F32) * xv_ref[...].astype(F32)
    pbuf[HALO:HALO + ts, :] = p
    cw = cw_ref[...]
    z = (cw[0:1] * pbuf[HALO - 2:HALO - 2 + ts, :] + cw[1:2] * pbuf[HALO - 1:HALO - 1 + ts, :]
         + cw[2:3] * p)
    o_ref[...] = (bg_ref[...].astype(F32) * z * _silu(g_ref[...].astype(F32))).astype(o_ref.dtype)
    pbuf[0:HALO, :] = pbuf[ts:ts + HALO, :]


def _conv_branch(u3, conv_w, w, col0):
    b, s, _ = u3.shape
    ts, tw = _tile(s, 512), _tile(w, 512)
    nw = w // tw
    c0 = col0 // tw

    def spec(k):
        return pl.BlockSpec((None, ts, tw), lambda i, j, t, k=k: (i, t, c0 + k * nw + j))

    return pl.pallas_call(
        _conv_kernel,
        grid=(b, nw, s // ts),
        in_specs=[spec(0), spec(1), spec(2), spec(3),
                  pl.BlockSpec((conv_w.shape[0], tw), lambda i, j, t: (0, j))],
        out_specs=pl.BlockSpec((None, ts, tw), lambda i, j, t: (i, t, j)),
        out_shape=jax.ShapeDtypeStruct((b, s, w), BF16),
        scratch_shapes=[pltpu.VMEM((HALO + ts, tw), F32)],
        compiler_params=_params(("parallel", "parallel", "arbitrary")),
        name="conv_branch",
    )(u3, u3, u3, u3, conv_w)


POOL_HALO = 16


def _pool_kernel(x_ref, g_ref, pw_ref, sc_ref, o_ref, xbuf):
    ts, w = o_ref.shape
    gw = w // len(POOL_WINDOWS)
    s = pl.program_id(1)

    @pl.when(s == 0)
    def _():
        xbuf[0:POOL_HALO, :] = jnp.zeros((POOL_HALO, w), F32)

    x = x_ref[...].astype(F32)
    xbuf[POOL_HALO:POOL_HALO + ts, :] = x
    pos = s * ts + lax.broadcasted_iota(jnp.int32, (ts, 1), 0)
    for gi, win in enumerate(POOL_WINDOWS):
        lo, hi = gi * gw, (gi + 1) * gw
        xi = x[:, lo:hi]
        acc = xi
        for k in range(1, win):
            acc = acc + xbuf[POOL_HALO - k:POOL_HALO - k + ts, lo:hi]
        count = jnp.minimum(pos + 1, win).astype(F32)
        pooled = acc / count - xi
        y = _dot(pooled.astype(BF16), pw_ref[gi])
        o_ref[:, lo:hi] = (y * sc_ref[:, lo:hi] * _silu(g_ref[:, lo:hi].astype(F32))).astype(o_ref.dtype)
    xbuf[0:POOL_HALO, :] = xbuf[ts:ts + POOL_HALO, :]


def _pool_branch(u3, pool_w, pool_scale, w, col0):
    b, s, _ = u3.shape
    ts = _tile(s, 512)
    c0 = col0 // w
    ng, gw, _ = pool_w.shape
    return pl.pallas_call(
        _pool_kernel,
        grid=(b, s // ts),
        in_specs=[
            pl.BlockSpec((None, ts, w), lambda i, t: (i, t, c0)),
            pl.BlockSpec((None, ts, w), lambda i, t: (i, t, c0 + 1)),
            pl.BlockSpec((ng, gw, gw), lambda i, t: (0, 0, 0)),
            pl.BlockSpec((1, w), lambda i, t: (0, 0)),
        ],
        out_specs=pl.BlockSpec((None, ts, w), lambda i, t: (i, t, 0)),
        out_shape=jax.ShapeDtypeStruct((b, s, w), BF16),
        scratch_shapes=[pltpu.VMEM((POOL_HALO + ts, w), F32)],
        compiler_params=_params(("parallel", "arbitrary")),
        name="pool_branch",
    )(u3, u3, pool_w, pool_scale)


def _rwkv_prep_kernel(r_ref, k_ref, v_ref, lo_ref, mu_ref, mul_ref, w0_ref, w2_ref, a0_ref, a2_ref,
                      kkw_ref, kaw_ref, hb_ref,
                      ro_ref, ko_ref, vo_ref, na_ref, bv_ref, ld_ref, buf, lbuf):
    ts, w = ro_ref.shape

    @pl.when(pl.program_id(1) == 0)
    def _():
        buf[0:HALO, :] = jnp.zeros((HALO, buf.shape[1]), F32)
        lbuf[0:HALO, :] = jnp.zeros((HALO, lbuf.shape[1]), F32)

    for idx, ref in enumerate((r_ref, k_ref, v_ref)):
        buf[HALO:HALO + ts, idx * w:(idx + 1) * w] = ref[...].astype(F32)
    lbuf[HALO:HALO + ts, :] = lo_ref[...]

    def mixed(cur, prev, mu):
        return cur + (prev - cur) * mu

    mu = mu_ref[...]
    r = mixed(buf[HALO:HALO + ts, 0:w], buf[HALO - 1:HALO - 1 + ts, 0:w], mu[0:1])
    k = mixed(buf[HALO:HALO + ts, w:2 * w], buf[HALO - 1:HALO - 1 + ts, w:2 * w], mu[1:2])
    v = mixed(buf[HALO:HALO + ts, 2 * w:3 * w], buf[HALO - 1:HALO - 1 + ts, 2 * w:3 * w], mu[2:3])
    lo = mixed(lbuf[HALO:HALO + ts, :], lbuf[HALO - 1:HALO - 1 + ts, :], mul_ref[...])
    wl, al = lo[:, 0:LORA], lo[:, LORA:2 * LORA]

    z = w0_ref[...] + _dot(jnp.tanh(wl).astype(BF16), w2_ref[...])
    ld_ref[...] = -float(np.exp(-0.5)) * _sigmoid(z)
    a = _sigmoid(a0_ref[...] + _dot(al.astype(BF16), a2_ref[...]))

    kk = k * kkw_ref[...]
    hb = hb_ref[...]
    gw = hb.shape[0]
    kk2 = kk * kk
    n2 = jnp.concatenate([_head_sums(kk2[:, l0:l0 + gw], hb) for l0 in range(0, w, gw)], axis=1)
    kk = kk * lax.rsqrt(jnp.maximum(n2, 1e-24))
    ro_ref[...] = r.astype(ro_ref.dtype)
    ko_ref[...] = (k * (1.0 + (a - 1.0) * kaw_ref[...])).astype(ko_ref.dtype)
    vo_ref[...] = v.astype(vo_ref.dtype)
    na_ref[...] = (-kk).astype(na_ref.dtype)
    bv_ref[...] = (kk * a).astype(bv_ref.dtype)

    buf[0:HALO, :] = buf[ts:ts + HALO, :]
    lbuf[0:HALO, :] = lbuf[ts:ts + HALO, :]


def _rwkv_prep(u3, lora, mu3, mul, w0, w2, a0, a2, kkw, kaw, w, col0):
    b, s, _ = u3.shape
    ts = _tile(s, 256)
    c0 = col0 // w
    gw = _tile(w, MXU_DIM)
    head_of = np.arange(gw) // HEAD
    hb = jnp.asarray(head_of[:, None] == head_of[None, :], BF16)
    row = lambda k: pl.BlockSpec((None, ts, w), lambda i, t, k=k: (i, t, c0 + k))
    full = lambda shape: pl.BlockSpec(shape, lambda i, t: (0,) * len(shape))
    outw = pl.BlockSpec((None, ts, w), lambda i, t: (i, t, 0))
    sds = lambda dt: jax.ShapeDtypeStruct((b, s, w), dt)
    return pl.pallas_call(
        _rwkv_prep_kernel,
        grid=(b, s // ts),
        in_specs=[row(0), row(1), row(2),
                  pl.BlockSpec((None, ts, 2 * LORA), lambda i, t: (i, t, 0)),
                  full((3, w)), full((1, 2 * LORA)), full((1, w)), full((LORA, w)),
                  full((1, w)), full((LORA, w)), full((1, w)), full((1, w)),
                  full((gw, gw))],
        out_specs=[outw] * 6,
        out_shape=[sds(BF16)] * 5 + [sds(F32)],
        scratch_shapes=[pltpu.VMEM((HALO + ts, 3 * w), F32), pltpu.VMEM((HALO + ts, 2 * LORA), F32)],
        compiler_params=_params(("parallel", "arbitrary")),
        name="rwkv_prep",
    )(u3, u3, u3, lora, mu3, mul, w0, w2, a0, a2, kkw, kaw, hb)


def _head_stack(x, lane_head):
    zero = jnp.zeros_like(x)
    return jnp.concatenate([jnp.where(lane_head == 0, x, zero), jnp.where(lane_head == 1, x, zero)], axis=0)


SUM_PIECES = 2


def _rwkv_scan_kernel(r_ref, k_ref, v_ref, na_ref, bv_ref, ld_ref, g_ref, lng_ref, lnb_ref, rk_ref,
                      cl_ref, cu_ref, hb_ref, o_ref, state, ybuf, dec_s, rh_s, ah_s, bt_s, kt_s, bb_s, kb_s):
    ts, width = o_ref.shape
    n_pairs = width // LANES
    c = RWKV_CHUNK
    n_chunks = ts // c

    @pl.when(pl.program_id(2) == 0)
    def _():
        state[...] = jnp.zeros_like(state)

    sb = cl_ref.shape[0]
    lcs, lrests = [], []
    for r0 in range(0, ts, sb):
        pieces = jnp.concatenate(_bf16_pieces(ld_ref[r0:r0 + sb, :], SUM_PIECES), axis=1)
        for m_ref, outs in ((cl_ref, lcs), (cu_ref, lrests)):
            both = _dot(m_ref[...], pieces)
            outs.append(both[:, 0:width] + both[:, width:2 * width])
    ld = ld_ref[...]
    lc = jnp.concatenate(lcs, axis=0)
    lrest = jnp.concatenate(lrests, axis=0)
    e_in = jnp.exp(lc)
    e_neg = jnp.exp(-lc)
    e_end = jnp.exp(lrest)
    r_all = r_ref[...].astype(F32)
    k_all = k_ref[...].astype(F32)
    na_all = na_ref[...].astype(F32)
    bv_all = bv_ref[...].astype(F32)
    dec_s[...] = e_in
    rh_s[...] = (r_all * e_in).astype(BF16)
    ah_s[...] = (na_all * jnp.exp(lc - ld)).astype(BF16)
    bt_s[...] = (bv_all * e_neg).astype(BF16)
    kt_s[...] = (k_all * e_neg).astype(BF16)
    bb_s[...] = (bv_all * e_end).astype(BF16)
    kb_s[...] = (k_all * e_end).astype(BF16)

    lane_head = lax.broadcasted_iota(jnp.int32, (1, LANES), 1) // HEAD
    t_idx = lax.broadcasted_iota(jnp.int32, (c, LANES), 0)
    s_idx = lax.broadcasted_iota(jnp.int32, (c, LANES), 1) % c
    strict = s_idx < t_idx
    incl = s_idx <= t_idx
    blk = (lax.broadcasted_iota(jnp.int32, (LANES, LANES), 0) // HEAD
           == lax.broadcasted_iota(jnp.int32, (LANES, LANES), 1) // HEAD)
    eye_cat = (s_idx == t_idx).astype(F32)

    def chunk(ci, states):
        rows = pl.ds(pl.multiple_of(ci * c, c), c)
        tail_rows = pl.ds(pl.multiple_of(ci * c + c - 8, 8), 8)
        pairs = range(n_pairs)
        lanes = [slice(pi * LANES, (pi + 1) * LANES) for pi in pairs]
        each = lambda f: [f(pi) for pi in pairs]
        v = each(lambda pi: v_ref[rows, lanes[pi]])
        rh = each(lambda pi: rh_s[rows, lanes[pi]])
        ah = each(lambda pi: ah_s[rows, lanes[pi]])
        sc = each(lambda pi: _dot_nt(
            jnp.concatenate([ah[pi], rh[pi]], axis=0),
            jnp.concatenate([_head_stack(bt_s[rows, lanes[pi]], lane_head),
                             _head_stack(kt_s[rows, lanes[pi]], lane_head)], axis=0)))
        a_ak = each(lambda pi: jnp.where(strict, sc[pi][0:c, LANES:2 * LANES], 0.0).astype(BF16))
        a_rb = each(lambda pi: jnp.where(incl, sc[pi][c:2 * c, 0:LANES], 0.0).astype(BF16))
        a_rk = each(lambda pi: jnp.where(incl, sc[pi][c:2 * c, LANES:2 * LANES], 0.0).astype(BF16))

        x = each(lambda pi: jnp.where(strict, sc[pi][0:c, 0:LANES], 0.0))
        tinv = each(lambda pi: eye_cat + x[pi])
        p2 = 2
        while p2 < c:
            xb = each(lambda pi: x[pi].astype(BF16))
            x = each(lambda pi: _dot(xb[pi], _head_stack(xb[pi], lane_head)))
            tinv = each(lambda pi: tinv[pi] + _dot(tinv[pi].astype(BF16),
                                                   _head_stack(x[pi].astype(BF16), lane_head)))
            p2 *= 2
        t_cat = each(lambda pi: tinv[pi].astype(BF16))

        g2b = each(lambda pi: states[pi].astype(BF16))
        vs = each(lambda pi: _head_stack(v[pi], lane_head))
        p = each(lambda pi: _dot_nt(ah[pi], g2b[pi]) + _dot(a_ak[pi], vs[pi]))
        ub = each(lambda pi: _dot(t_cat[pi], _head_stack(p[pi].astype(BF16), lane_head)).astype(BF16))
        new_states = []
        for pi in pairs:
            y = (_dot_nt(rh[pi], g2b[pi]) + _dot(a_rb[pi], _head_stack(ub[pi], lane_head))
                 + _dot(a_rk[pi], vs[pi]))
            ybuf[rows, lanes[pi]] = y
            uv_t = jnp.concatenate([ub[pi], v[pi]], axis=0).T
            upd = _dot(uv_t, jnp.concatenate([bb_s[rows, lanes[pi]], kb_s[rows, lanes[pi]]], axis=0))
            decay = dec_s[tail_rows, lanes[pi]][7:8, :]
            new_states.append(states[pi] * decay + jnp.where(blk, upd, 0.0))
        return tuple(new_states)

    final = lax.fori_loop(0, n_chunks, chunk, tuple(state[pi] for pi in range(n_pairs)))
    for pi in range(n_pairs):
        state[pi] = final[pi]

    hb = hb_ref[...]
    gw = hb.shape[0]
    for l0 in range(0, width, gw):
        lanes = slice(l0, l0 + gw)
        y = ybuf[:, lanes]
        mean = _head_sums(y, hb) * (1.0 / HEAD)
        d = y - mean
        var = _head_sums(d * d, hb) * (1.0 / HEAD)
        yn = d * lax.rsqrt(var + RWKV_LN_EPS) * lng_ref[:, lanes] + lnb_ref[:, lanes]
        rk = r_ref[:, lanes].astype(F32) * k_ref[:, lanes].astype(F32) * rk_ref[:, lanes]
        bonus = _head_sums(rk, hb) * v_ref[:, lanes].astype(F32)
        o_ref[:, lanes] = ((yn + bonus) * _silu(g_ref[:, lanes].astype(F32))).astype(o_ref.dtype)


def _head_sums(x, m):
    return functools.reduce(jnp.add, [_dot(p, m) for p in _bf16_pieces(x, SUM_PIECES)])


RWKV_PAIRS = 8


def _rwkv_scan(prep, u3, ln_g, ln_b, r_k, w, gate_col0):
    r, k, v, na, bv, ld = prep
    b, s, _ = r.shape
    ts = _tile(s, 512)
    width = LANES * _tile(w // LANES, RWKV_PAIRS)
    gc0 = gate_col0 // width
    sb = _tile(ts, MXU_DIM)
    chunk_of = np.arange(sb) // RWKV_CHUNK
    same = chunk_of[:, None] == chunk_of[None, :]
    tri = np.arange(sb)[:, None] >= np.arange(sb)[None, :]
    cl = jnp.asarray(same & tri, BF16)
    cu = jnp.asarray(same & ~tri, BF16)
    gw = _tile(width, MXU_DIM)
    head_of = np.arange(gw) // HEAD
    hb = jnp.asarray(head_of[:, None] == head_of[None, :], BF16)
    row = pl.BlockSpec((None, ts, width), lambda i, p, t: (i, t, p))
    vec = pl.BlockSpec((1, width), lambda i, p, t: (0, p))
    const = lambda n: pl.BlockSpec((n, n), lambda i, p, t: (0, 0))
    tile_bf16 = pltpu.VMEM((ts, width), BF16)
    return pl.pallas_call(
        _rwkv_scan_kernel,
        grid=(b, w // width, s // ts),
        in_specs=[row] * 6 + [pl.BlockSpec((None, ts, width), lambda i, p, t: (i, t, gc0 + p)),
                              vec, vec, vec, const(sb), const(sb), const(gw)],
        out_specs=row,
        out_shape=jax.ShapeDtypeStruct((b, s, w), BF16),
        scratch_shapes=[pltpu.VMEM((width // LANES, LANES, LANES), F32), pltpu.VMEM((ts, width), F32),
                        pltpu.VMEM((ts, width), F32)] + [tile_bf16] * 6,
        compiler_params=_params(("parallel", "parallel", "arbitrary")),
        name="rwkv_scan",
    )(r, k, v, na, bv, ld, u3, ln_g, ln_b, r_k, cl, cu, hb)


FOX_BLOCK = 1024
FOX_SUB = 16
FOX_ACC_ROWS = HEAD + 16


def _fold8(x, op):
    parts = [x[i:i + 8, :] for i in range(0, x.shape[0], 8)]
    return functools.reduce(op, parts)


def _fox_kernel(qi_ref, kj_ref, q_ref, k_ref, v_ref, kc_ref, g_ref, o_ref,
                m_ref, acc_ref, qx_ref, s0_ref, s1_ref, p0_ref, p1_ref):
    pair, step = pl.program_id(1), pl.program_id(2)
    qi, kj = qi_ref[step], kj_ref[step]
    tq, tk = q_ref.shape[0], k_ref.shape[0]
    lane = lax.broadcasted_iota(jnp.int32, (1, LANES), 1)

    @pl.when(kj == 0)
    def _():
        m_ref[...] = jnp.full_like(m_ref, NEG_BIG)
        acc_ref[...] = jnp.zeros_like(acc_ref)
        q2 = q_ref[...]
        for h in range(2):
            first = _forget_lane(2 * pair + h, 0)
            ones = jnp.where((lane >= first) & (lane < first + C_PIECES), 1.0, 0.0).astype(BF16)
            qx_ref[h, :, 0:LANES] = jnp.where(lane // HEAD == h, q2, jnp.zeros_like(q2))
            qx_ref[h, :, LANES:2 * LANES] = jnp.broadcast_to(ones, q2.shape)

    s_refs, p_refs = (s0_ref, s1_ref), (p0_ref, p1_ref)

    def step_body(masked):
        kx = jnp.concatenate([k_ref[...], kc_ref[...]], axis=1)
        half = tk // 2
        block_max = [None, None]
        for h in range(2):
            if masked:
                s_refs[h][0:half, :] = _dot_nt(kx[0:half, :], qx_ref[h])
                s_refs[h][half:, half:] = _dot_nt(kx[half:, :], qx_ref[h, half:, :])
            else:
                s_val = _dot_nt(kx, qx_ref[h])
                s_refs[h][...] = s_val
                block_max[h] = _fold8(s_val, jnp.maximum)
        n_sub, n_grp = tk // FOX_SUB, tq // LANES

        def first_group(kb):
            return (kb * FOX_SUB) // LANES if masked else 0

        def group_scores(h, kb, g):
            blk = s_refs[h][kb * FOX_SUB:(kb + 1) * FOX_SUB, g * LANES:(g + 1) * LANES]
            if masked and g == first_group(kb):
                key = kb * FOX_SUB + lax.broadcasted_iota(jnp.int32, (FOX_SUB, LANES), 0)
                query = g * LANES + lax.broadcasted_iota(jnp.int32, (FOX_SUB, LANES), 1)
                blk = jnp.where(key <= query, blk, NEG_BIG)
            return blk

        def scores(h, kb, g0):
            return jnp.concatenate([group_scores(h, kb, g) for g in range(g0, n_grp)], axis=1)

        def running_max(h):
            if masked:
                mx = [None] * n_grp
                for kb in range(n_sub):
                    g0 = first_group(kb)
                    part = _fold8(scores(h, kb, g0), jnp.maximum)
                    for g in range(g0, n_grp):
                        piece = part[:, (g - g0) * LANES:(g - g0 + 1) * LANES]
                        mx[g] = piece if mx[g] is None else jnp.maximum(mx[g], piece)
                mx8 = jnp.concatenate(mx, axis=1)
            else:
                mx8 = block_max[h]
            m_prev = m_ref[h]
            m_new = jnp.maximum(m_prev, jnp.max(mx8, axis=0, keepdims=True))
            m_ref[h] = m_new
            return m_new, jnp.exp2(m_prev - m_new)

        def probabilities(h, m_new):
            for kb in range(n_sub):
                g0 = first_group(kb)
                rows = slice(kb * FOX_SUB, (kb + 1) * FOX_SUB)
                p_refs[h][rows, g0 * LANES:] = jnp.exp2(scores(h, kb, g0) - m_new[:, g0 * LANES:]).astype(BF16)
                z0 = half if kb * FOX_SUB >= half else 0
                if g0 * LANES > z0:
                    p_refs[h][rows, z0:g0 * LANES] = jnp.zeros((FOX_SUB, g0 * LANES - z0), BF16)

        def accumulate(h, alpha):
            lhs = jnp.concatenate([v_t[h * HEAD:(h + 1) * HEAD, :], sum_rows], axis=0)
            if masked:
                acc_ref[h] = alpha * acc_ref[h] + _dot(lhs[:, 0:half], p_refs[h][0:half, :])
                acc_ref[h, :, half:] += _dot(lhs[:, half:], p_refs[h][half:, half:])
            else:
                acc_ref[h] = alpha * acc_ref[h] + _dot(lhs, p_refs[h][...])

        eye = (lax.broadcasted_iota(jnp.int32, (LANES, LANES), 0)
               == lax.broadcasted_iota(jnp.int32, (LANES, LANES), 1)).astype(BF16)
        v_t = _dot_nt(eye, v_ref[...]).astype(BF16)
        sum_rows = jnp.ones((FOX_ACC_ROWS - HEAD, tk), BF16)
        m0, alpha0 = running_max(0)
        probabilities(0, m0)
        m1, alpha1 = running_max(1)
        accumulate(0, alpha0)
        probabilities(1, m1)
        accumulate(1, alpha1)

    @pl.when(kj < qi)
    def _():
        step_body(False)

    @pl.when(kj == qi)
    def _():
        step_body(True)
        outs = [acc_ref[h, 0:HEAD, :] * (1.0 / acc_ref[h, HEAD:HEAD + 1, :]) for h in range(2)]
        out = jnp.concatenate(outs, axis=0).T
        o_ref[...] = (out * _silu(g_ref[...].astype(F32))).astype(o_ref.dtype)


def _fox_branch(u3, kc, w, q_col0, gate_col0):
    b, s, _ = u3.shape
    tq = tk = _tile(s, FOX_BLOCK)
    n_pairs = w // LANES
    pairs = [(i, j) for i in range(s // tq) for j in range(i + 1)]
    qi = jnp.asarray([p[0] for p in pairs], jnp.int32)
    kj = jnp.asarray([p[1] for p in pairs], jnp.int32)
    qc0, gc0 = q_col0 // LANES, gate_col0 // LANES
    grid_spec = pltpu.PrefetchScalarGridSpec(
        num_scalar_prefetch=2,
        grid=(b, n_pairs, len(pairs)),
        in_specs=[
            pl.BlockSpec((None, tq, LANES), lambda i, p, t, qi, kj: (i, qi[t], qc0 + p)),
            pl.BlockSpec((None, tk, LANES), lambda i, p, t, qi, kj: (i, kj[t], qc0 + n_pairs + p)),
            pl.BlockSpec((None, tk, LANES), lambda i, p, t, qi, kj: (i, kj[t], qc0 + 2 * n_pairs + p)),
            pl.BlockSpec((None, tk, LANES), lambda i, p, t, qi, kj: (i, kj[t], 0)),
            pl.BlockSpec((None, tq, LANES), lambda i, p, t, qi, kj: (i, qi[t], gc0 + p)),
        ],
        out_specs=pl.BlockSpec((None, tq, LANES), lambda i, p, t, qi, kj: (i, qi[t], p)),
        scratch_shapes=[pltpu.VMEM((2, 1, tq), F32),
                        pltpu.VMEM((2, FOX_ACC_ROWS, tq), F32), pltpu.VMEM((2, tq, 2 * LANES), BF16),
                        pltpu.VMEM((tk, tq), F32), pltpu.VMEM((tk, tq), F32),
                        pltpu.VMEM((tk, tq), BF16), pltpu.VMEM((tk, tq), BF16)],
    )
    return pl.pallas_call(
        _fox_kernel,
        grid_spec=grid_spec,
        out_shape=jax.ShapeDtypeStruct((b, s, w), BF16),
        compiler_params=_params(("parallel", "parallel", "arbitrary")),
        name="fox_attention",
    )(qi, kj, u3, u3, u3, kc, u3)


def _merge_kernel(ya_ref, yb_ref, yc_ref, yd_ref, ma_ref, mb_ref, mc_ref, md_ref, bm_ref, wb_ref, o_ref):
    acc = None
    branches = ((ya_ref, ma_ref), (yb_ref, mb_ref), (yc_ref, mc_ref), (yd_ref, md_ref))
    for kbr, (y_ref, ml_ref) in enumerate(branches):
        proj = _dot(y_ref[...], wb_ref[kbr])
        gate = _sigmoid(ml_ref[...].astype(F32) + bm_ref[kbr:kbr + 1, :])
        acc = gate * proj if acc is None else acc + gate * proj
    o_ref[...] = acc.astype(o_ref.dtype)


def _merge(ys, u2, b_merge, w_branch_all, layer, ml_col0):
    t, w = ys[0].shape
    _, nb, _, d = w_branch_all.shape
    tm, tn = _tile(t, 1024), _tile(d, 512)
    nd = d // tn
    mc0 = ml_col0 // tn
    yspec = pl.BlockSpec((tm, w), lambda i, j: (i, 0))
    mspec = lambda k: pl.BlockSpec((tm, tn), lambda i, j, k=k: (i, mc0 + k * nd + j))
    return pl.pallas_call(
        _merge_kernel,
        grid=(t // tm, nd),
        in_specs=[yspec] * 4 + [mspec(k) for k in range(4)] + [
            pl.BlockSpec((nb, tn), lambda i, j: (0, j)),
            pl.BlockSpec((None, nb, w, tn), lambda i, j: (layer, 0, 0, j)),
        ],
        out_specs=pl.BlockSpec((tm, tn), lambda i, j: (i, j)),
        out_shape=jax.ShapeDtypeStruct((t, d), BF16),
        compiler_params=_params(("parallel", "arbitrary")),
        name="merge",
    )(*ys, u2, u2, u2, u2, b_merge, w_branch_all)


def _outproj_kernel(m_ref, w_ref, x_ref, o_ref):
    o_ref[...] = x_ref[...] + _dot(m_ref[...], w_ref[...])


def _outproj(merged, w_out_all, layer, x2):
    t, d = x2.shape
    tm, tn = _tile(t, 1024), _tile(d, 512)
    return pl.pallas_call(
        _outproj_kernel,
        grid=(t // tm, d // tn),
        in_specs=[
            pl.BlockSpec((tm, d), lambda i, j: (i, 0)),
            pl.BlockSpec((None, d, tn), lambda i, j: (layer, 0, j)),
            pl.BlockSpec((tm, tn), lambda i, j: (i, j)),
        ],
        out_specs=pl.BlockSpec((tm, tn), lambda i, j: (i, j)),
        out_shape=jax.ShapeDtypeStruct((t, d), F32),
        compiler_params=_params(("parallel", "arbitrary")),
        name="outproj",
    )(merged, w_out_all, x2)


def _final_norm_kernel(x_ref, g_ref, o_ref):
    o_ref[...] = _rms_rows(x_ref[...], g_ref[...])


def _final_norm(x2, g):
    t, d = x2.shape
    tm = _tile(t, 512)
    return pl.pallas_call(
        _final_norm_kernel,
        grid=(t // tm,),
        in_specs=[pl.BlockSpec((tm, d), lambda i: (i, 0)), pl.BlockSpec((1, d), lambda i: (0, 0))],
        out_specs=pl.BlockSpec((tm, d), lambda i: (i, 0)),
        out_shape=jax.ShapeDtypeStruct((t, d), F32),
        compiler_params=_params(("parallel",)),
        name="final_norm",
    )(x2, g)


def _split_in_weights(w_in, w, nh):
    w_t = jnp.swapaxes(w_in, 1, 2)
    lo0, f0 = 7 * w, 11 * w + 2 * LORA
    n_main = w_in.shape[2] - 2 * LORA - nh
    segments = ((0, 0), (lo0, lo0 + 2 * LORA), (f0 - 2 * LORA, f0 + nh))
    col_scale = np.ones((1, n_main), np.float32)
    col_scale[:, 8 * w:9 * w] = HEAD ** -0.5 * LOG2E
    pad = jnp.zeros((w_t.shape[0], LANES - nh, w_t.shape[2]), w_t.dtype)
    w_small = jnp.concatenate([w_t[:, lo0:lo0 + 2 * LORA], w_t[:, f0:f0 + nh], pad], axis=1).astype(BF16)
    return w_t.astype(BF16), segments, jnp.asarray(col_scale), w_small


def _layer(x2, bsz, seq, layer, w_in_all, segments, col_scale, w_small, w_branch_all, w_out_all, norm_g, b_merge,
           conv_w, rwkv_mu, rwkv_w0, rwkv_w2, rwkv_a0, rwkv_a2, rwkv_kk, rwkv_ka, rwkv_rk, rwkv_ln_g, rwkv_ln_b,
           fox_bf, pool_w, pool_scale):
    t, d = x2.shape
    w = conv_w.shape[1]
    nh = fox_bf.shape[0]
    b_f = jnp.concatenate([fox_bf, jnp.zeros((LANES - nh,), fox_bf.dtype)]).reshape(1, LANES)
    row = lambda a: a.reshape(1, -1)

    u2 = _inproj(x2, row(norm_g), w_in_all, layer, segments, col_scale)
    u3 = u2.reshape(bsz, seq, -1)
    lora, kc = _side(x2.reshape(bsz, seq, d), row(norm_g), w_small, b_f, w)
    y_a = _conv_branch(u3, conv_w, w, 0)
    prep = _rwkv_prep(u3, lora, rwkv_mu[:3 * w].reshape(3, w), row(rwkv_mu[3 * w:]), row(rwkv_w0),
                      rwkv_w2.astype(BF16), row(rwkv_a0), rwkv_a2.astype(BF16), row(rwkv_kk), row(rwkv_ka),
                      w, 4 * w)
    y_b = _rwkv_scan(prep, u3, row(rwkv_ln_g), row(rwkv_ln_b), row(rwkv_rk), w, 7 * w)
    y_c = _fox_branch(u3, kc, w, 8 * w, 11 * w)
    y_d = _pool_branch(u3, pool_w.astype(BF16), row(pool_scale), w, 12 * w)
    ys = [y.reshape(t, w) for y in (y_a, y_b, y_c, y_d)]
    merged = _merge(ys, u2, b_merge, w_branch_all, layer, 14 * w)
    return _outproj(merged, w_out_all, layer, x2)


def kernel(x, norm_g, w_in, b_merge, conv_w, rwkv_mu, rwkv_w0, rwkv_w2, rwkv_a0, rwkv_a2, rwkv_kk, rwkv_ka,
           rwkv_rk, rwkv_ln_g, rwkv_ln_b, fox_bf, pool_w, pool_scale, w_branch, w_out, final_g):
    bsz, seq, d = x.shape
    x2 = x.reshape(bsz * seq, d)
    w_in_all, segments, col_scale, w_small_all = _split_in_weights(w_in, conv_w.shape[2], fox_bf.shape[1])
    w_branch_all = w_branch.astype(BF16)
    w_out_all = w_out.astype(BF16)
    for l in range(norm_g.shape[0]):
        x2 = _layer(x2, bsz, seq, l, w_in_all, segments, col_scale, w_small_all[l], w_branch_all, w_out_all,
                    norm_g[l], b_merge[l], conv_w[l], rwkv_mu[l], rwkv_w0[l], rwkv_w2[l], rwkv_a0[l], rwkv_a2[l],
                    rwkv_kk[l], rwkv_ka[l], rwkv_rk[l], rwkv_ln_g[l], rwkv_ln_b[l], fox_bf[l], pool_w[l],
                    pool_scale[l])
    return _final_norm(x2, final_g.reshape(1, d)).reshape(bsz, seq, d)
```

```python
import functools

import jax
import jax.numpy as jnp
import numpy as np
from jax import lax
from jax.experimental import pallas as pl
from jax.experimental.pallas import tpu as pltpu

F32 = jnp.float32
BF16 = jnp.bfloat16
HIGHEST = lax.Precision.HIGHEST

NORM_EPS = 1e-6
RWKV_LN_EPS = 64e-5
HEAD = 64
LORA = 64
POOL_WINDOWS = (2, 4, 8, 16)
RWKV_CHUNK = 64
LANES = 128
MXU_DIM = 256
LOG2E = float(np.log2(np.e))
NEG_BIG = -1e30
VMEM_LIMIT = 48 * 1024 * 1024


def _params(sem):
    return pltpu.CompilerParams(dimension_semantics=sem, vmem_limit_bytes=VMEM_LIMIT)


def _tile(n, want):
    t = min(n, want)
    while n % t:
        t //= 2
    return t


def _sigmoid(x):
    return 1.0 / (1.0 + jnp.exp(-x))


def _silu(x):
    return x * _sigmoid(x)


def _rms_rows(xf, g):
    ms = jnp.mean(xf * xf, axis=-1, keepdims=True)
    return xf * lax.rsqrt(ms + NORM_EPS) * g


def _dot(a, b):
    return jnp.dot(a, b, preferred_element_type=F32)


def _dot_nt(a, b):
    return lax.dot_general(a, b, (((1,), (1,)), ((), ())), preferred_element_type=F32)


def _dot_f32(a, b):
    return jnp.dot(a, b, precision=HIGHEST, preferred_element_type=F32)


def _inproj_kernel(x_ref, g_ref, sc_ref, w_ref, o_ref, h_ref):
    @pl.when(pl.program_id(1) == 0)
    def _():
        h_ref[...] = _rms_rows(x_ref[...], g_ref[...]).astype(BF16)

    o_ref[...] = (_dot_nt(h_ref[...], w_ref[0].astype(BF16)) * sc_ref[...]).astype(o_ref.dtype)


def _inproj(x2, g, w_all, layer, segments, col_scale):
    t, d = x2.shape
    n = col_scale.shape[1]
    bounds = [c0 for c0, _ in segments] + [n]
    tn = _tile(functools.reduce(np.gcd, np.diff(bounds)), 1024)
    tm = _tile(t, 1024)

    align = int(functools.reduce(np.gcd, [tn] + [r0 for _, r0 in segments[1:]]))

    def weight_row(j):
        row = j * tn
        for (c0, r0), (c1, r1) in zip(segments[:-1], segments[1:]):
            row = row + jnp.where(j * tn >= c1, (r1 - c1) - (r0 - c0), 0)
        return pl.multiple_of(row + (segments[0][1] - segments[0][0]), align)

    return pl.pallas_call(
        _inproj_kernel,
        grid=(t // tm, n // tn),
        in_specs=[
            pl.BlockSpec((tm, d), lambda i, j: (i, 0)),
            pl.BlockSpec((1, d), lambda i, j: (0, 0)),
            pl.BlockSpec((1, tn), lambda i, j: (0, j)),
            pl.BlockSpec((pl.Element(1), pl.Element(tn), pl.Element(d)),
                         lambda i, j: (layer, weight_row(j), 0)),
        ],
        out_specs=pl.BlockSpec((tm, tn), lambda i, j: (i, j)),
        out_shape=jax.ShapeDtypeStruct((t, n), BF16),
        scratch_shapes=[pltpu.VMEM((tm, d), BF16)],
        compiler_params=_params(("parallel", "arbitrary")),
        name="inproj",
    )(x2, g, col_scale, w_all)


C_PIECES = 3


def _bf16_pieces(x, n=C_PIECES):
    pieces = []
    for _ in range(n):
        p = x.astype(BF16)
        pieces.append(p)
        x = x - p.astype(F32)
    return pieces


def _side_kernel(x_ref, g_ref, ws_ref, bf_ref, sel_ref, lora_ref, kc_ref, carry_ref):
    @pl.when(pl.program_id(1) == 0)
    def _():
        carry_ref[...] = jnp.zeros_like(carry_ref)

    h = _rms_rows(x_ref[...], g_ref[...]).astype(BF16)
    sf = _dot_nt(h, ws_ref[...])
    ns = lora_ref.shape[1]
    lora_ref[...] = sf[:, :ns]
    z = sf[:, ns:] + bf_ref[...]
    logf = jnp.minimum(z, 0.0) - jnp.log1p(jnp.exp(-jnp.abs(z)))
    ts = logf.shape[0]
    lower = (lax.broadcasted_iota(jnp.int32, (ts, ts), 0)
             >= lax.broadcasted_iota(jnp.int32, (ts, ts), 1)).astype(BF16)
    c = functools.reduce(jnp.add, [_dot(lower, p) for p in _bf16_pieces(logf)]) + carry_ref[...]
    carry_ref[...] = c[ts - 1:ts, :]
    kc = None
    for piece, sel in zip(_bf16_pieces(-LOG2E * c), (sel_ref[0], sel_ref[1], sel_ref[2])):
        term = _dot(piece, sel)
        kc = term if kc is None else kc + term
    kc_ref[...] = kc.astype(kc_ref.dtype)


def _forget_lane(head, piece):
    return C_PIECES * head + piece


def _side(x3, g, w_small, b_f, w):
    b, s, d = x3.shape
    ns = w_small.shape[0] - LANES
    ts = _tile(s, 512)
    nh = w // HEAD
    assert C_PIECES * nh <= LANES
    sel = np.zeros((C_PIECES, LANES, LANES), np.float32)
    for hd in range(nh):
        for piece in range(C_PIECES):
            sel[piece, hd, _forget_lane(hd, piece)] = 1.0
    return pl.pallas_call(
        _side_kernel,
        grid=(b, s // ts),
        in_specs=[
            pl.BlockSpec((None, ts, d), lambda i, j: (i, j, 0)),
            pl.BlockSpec((1, d), lambda i, j: (0, 0)),
            pl.BlockSpec((ns + LANES, d), lambda i, j: (0, 0)),
            pl.BlockSpec((1, LANES), lambda i, j: (0, 0)),
            pl.BlockSpec((C_PIECES, LANES, LANES), lambda i, j: (0, 0, 0)),
        ],
        out_specs=[
            pl.BlockSpec((None, ts, ns), lambda i, j: (i, j, 0)),
            pl.BlockSpec((None, ts, LANES), lambda i, j: (i, j, 0)),
        ],
        out_shape=[
            jax.ShapeDtypeStruct((b, s, ns), F32),
            jax.ShapeDtypeStruct((b, s, LANES), BF16),
        ],
        scratch_shapes=[pltpu.VMEM((1, LANES), F32)],
        compiler_params=_params(("parallel", "arbitrary")),
        name="side",
    )(x3, g, w_small, b_f, jnp.asarray(sel, BF16))


HALO = 8


def _conv_kernel(bg_ref, cg_ref, xv_ref, g_ref, cw_ref, o_ref, pbuf):
    ts = o_ref.shape[0]

    @pl.when(pl.program_id(2) == 0)
    def _():
        pbuf[0:HALO, :] = jnp.zeros((HALO, pbuf.shape[1]), F32)

    p = cg_ref[...].astype(F32) * xv_ref[...].astype(F32)
    pbuf[HALO:HALO + ts, :] = p
    cw = cw_ref[...]
    z = (cw[0:1] * pbuf[HALO - 2:HALO - 2 + ts, :] + cw[1:2] * pbuf[HALO - 1:HALO - 1 + ts, :]
         + cw[2:3] * p)
    o_ref[...] = (bg_ref[...].astype(F32) * z * _silu(g_ref[...].astype(F32))).astype(o_ref.dtype)
    pbuf[0:HALO, :] = pbuf[ts:ts + HALO, :]


def _conv_branch(u3, conv_w, w, col0):
    b, s, _ = u3.shape
    ts, tw = _tile(s, 512), _tile(w, 512)
    nw = w // tw
    c0 = col0 // tw

    def spec(k):
        return pl.BlockSpec((None, ts, tw), lambda i, j, t, k=k: (i, t, c0 + k * nw + j))

    return pl.pallas_call(
        _conv_kernel,
        grid=(b, nw, s // ts),
        in_specs=[spec(0), spec(1), spec(2), spec(3),
                  pl.BlockSpec((conv_w.shape[0], tw), lambda i, j, t: (0, j))],
        out_specs=pl.BlockSpec((None, ts, tw), lambda i, j, t: (i, t, j)),
        out_shape=jax.ShapeDtypeStruct((b, s, w), BF16),
        scratch_shapes=[pltpu.VMEM((HALO + ts, tw), F32)],
        compiler_params=_params(("parallel", "parallel", "arbitrary")),
        name="conv_branch",
    )(u3, u3, u3, u3, conv_w)


POOL_HALO = 16


def _pool_kernel(x_ref, g_ref, pw_ref, sc_ref, o_ref, xbuf):
    ts, w = o_ref.shape
    gw = w // len(POOL_WINDOWS)
    s = pl.program_id(1)

    @pl.when(s == 0)
    def _():
        xbuf[0:POOL_HALO, :] = jnp.zeros((POOL_HALO, w), F32)

    x = x_ref[...].astype(F32)
    xbuf[POOL_HALO:POOL_HALO + ts, :] = x
    pos = s * ts + lax.broadcasted_iota(jnp.int32, (ts, 1), 0)
    for gi, win in enumerate(POOL_WINDOWS):
        lo, hi = gi * gw, (gi + 1) * gw
        xi = x[:, lo:hi]
        acc = xi
        for k in range(1, win):
            acc = acc + xbuf[POOL_HALO - k:POOL_HALO - k + ts, lo:hi]
        count = jnp.minimum(pos + 1, win).astype(F32)
        pooled = acc / count - xi
        y = _dot(pooled.astype(BF16), pw_ref[gi])
        o_ref[:, lo:hi] = (y * sc_ref[:, lo:hi] * _silu(g_ref[:, lo:hi].astype(F32))).astype(o_ref.dtype)
    xbuf[0:POOL_HALO, :] = xbuf[ts:ts + POOL_HALO, :]


def _pool_branch(u3, pool_w, pool_scale, w, col0):
    b, s, _ = u3.shape
    ts = _tile(s, 512)
    c0 = col0 // w
    ng, gw, _ = pool_w.shape
    return pl.pallas_call(
        _pool_kernel,
        grid=(b, s // ts),
        in_specs=[
            pl.BlockSpec((None, ts, w), lambda i, t: (i, t, c0)),
            pl.BlockSpec((None, ts, w), lambda i, t: (i, t, c0 + 1)),
            pl.BlockSpec((ng, gw, gw), lambda i, t: (0, 0, 0)),
            pl.BlockSpec((1, w), lambda i, t: (0, 0)),
        ],
        out_specs=pl.BlockSpec((None, ts, w), lambda i, t: (i, t, 0)),
        out_shape=jax.ShapeDtypeStruct((b, s, w), BF16),
        scratch_shapes=[pltpu.VMEM((POOL_HALO + ts, w), F32)],
        compiler_params=_params(("parallel", "arbitrary")),
        name="pool_branch",
    )(u3, u3, pool_w, pool_scale)


def _rwkv_prep_kernel(r_ref, k_ref, v_ref, lo_ref, mu_ref, mul_ref, w0_ref, w2_ref, a0_ref, a2_ref,
                      kkw_ref, kaw_ref, hb_ref,
                      ro_ref, ko_ref, vo_ref, na_ref, bv_ref, ld_ref, buf, lbuf):
    ts, w = ro_ref.shape

    @pl.when(pl.program_id(1) == 0)
    def _():
        buf[0:HALO, :] = jnp.zeros((HALO, buf.shape[1]), F32)
        lbuf[0:HALO, :] = jnp.zeros((HALO, lbuf.shape[1]), F32)

    for idx, ref in enumerate((r_ref, k_ref, v_ref)):
        buf[HALO:HALO + ts, idx * w:(idx + 1) * w] = ref[...].astype(F32)
    lbuf[HALO:HALO + ts, :] = lo_ref[...]

    def mixed(cur, prev, mu):
        return cur + (prev - cur) * mu

    mu = mu_ref[...]
    r = mixed(buf[HALO:HALO + ts, 0:w], buf[HALO - 1:HALO - 1 + ts, 0:w], mu[0:1])
    k = mixed(buf[HALO:HALO + ts, w:2 * w], buf[HALO - 1:HALO - 1 + ts, w:2 * w], mu[1:2])
    v = mixed(buf[HALO:HALO + ts, 2 * w:3 * w], buf[HALO - 1:HALO - 1 + ts, 2 * w:3 * w], mu[2:3])
    lo = mixed(lbuf[HALO:HALO + ts, :], lbuf[HALO - 1:HALO - 1 + ts, :], mul_ref[...])
    wl, al = lo[:, 0:LORA], lo[:, LORA:2 * LORA]

    z = w0_ref[...] + _dot(jnp.tanh(wl).astype(BF16), w2_ref[...])
    ld_ref[...] = -float(np.exp(-0.5)) * _sigmoid(z)
    a = _sigmoid(a0_ref[...] + _dot(al.astype(BF16), a2_ref[...]))

    kk = k * kkw_ref[...]
    hb = hb_ref[...]
    gw = hb.shape[0]
    kk2 = kk * kk
    n2 = jnp.concatenate([_head_sums(kk2[:, l0:l0 + gw], hb) for l0 in range(0, w, gw)], axis=1)
    kk = kk * lax.rsqrt(jnp.maximum(n2, 1e-24))
    ro_ref[...] = r.astype(ro_ref.dtype)
    ko_ref[...] = (k * (1.0 + (a - 1.0) * kaw_ref[...])).astype(ko_ref.dtype)
    vo_ref[...] = v.astype(vo_ref.dtype)
    na_ref[...] = (-kk).astype(na_ref.dtype)
    bv_ref[...] = (kk * a).astype(bv_ref.dtype)

    buf[0:HALO, :] = buf[ts:ts + HALO, :]
    lbuf[0:HALO, :] = lbuf[ts:ts + HALO, :]


def _rwkv_prep(u3, lora, mu3, mul, w0, w2, a0, a2, kkw, kaw, w, col0):
    b, s, _ = u3.shape
    ts = _tile(s, 256)
    c0 = col0 // w
    gw = _tile(w, MXU_DIM)
    head_of = np.arange(gw) // HEAD
    hb = jnp.asarray(head_of[:, None] == head_of[None, :], BF16)
    row = lambda k: pl.BlockSpec((None, ts, w), lambda i, t, k=k: (i, t, c0 + k))
    full = lambda shape: pl.BlockSpec(shape, lambda i, t: (0,) * len(shape))
    outw = pl.BlockSpec((None, ts, w), lambda i, t: (i, t, 0))
    sds = lambda dt: jax.ShapeDtypeStruct((b, s, w), dt)
    return pl.pallas_call(
        _rwkv_prep_kernel,
        grid=(b, s // ts),
        in_specs=[row(0), row(1), row(2),
                  pl.BlockSpec((None, ts, 2 * LORA), lambda i, t: (i, t, 0)),
                  full((3, w)), full((1, 2 * LORA)), full((1, w)), full((LORA, w)),
                  full((1, w)), full((LORA, w)), full((1, w)), full((1, w)),
                  full((gw, gw))],
        out_specs=[outw] * 6,
        out_shape=[sds(BF16)] * 5 + [sds(F32)],
        scratch_shapes=[pltpu.VMEM((HALO + ts, 3 * w), F32), pltpu.VMEM((HALO + ts, 2 * LORA), F32)],
        compiler_params=_params(("parallel", "arbitrary")),
        name="rwkv_prep",
    )(u3, u3, u3, lora, mu3, mul, w0, w2, a0, a2, kkw, kaw, hb)


def _head_stack(x, lane_head):
    zero = jnp.zeros_like(x)
    return jnp.concatenate([jnp.where(lane_head == 0, x, zero), jnp.where(lane_head == 1, x, zero)], axis=0)


SUM_PIECES = 2


def _rwkv_scan_kernel(r_ref, k_ref, v_ref, na_ref, bv_ref, ld_ref, g_ref, lng_ref, lnb_ref, rk_ref,
                      cl_ref, cu_ref, hb_ref, o_ref, state, ybuf, dec_s, rh_s, ah_s, bt_s, kt_s, bb_s, kb_s):
    ts, width = o_ref.shape
    n_pairs = width // LANES
    c = RWKV_CHUNK
    n_chunks = ts // c

    @pl.when(pl.program_id(2) == 0)
    def _():
        state[...] = jnp.zeros_like(state)

    sb = cl_ref.shape[0]
    lcs, lrests = [], []
    for r0 in range(0, ts, sb):
        pieces = jnp.concatenate(_bf16_pieces(ld_ref[r0:r0 + sb, :], SUM_PIECES), axis=1)
        for m_ref, outs in ((cl_ref, lcs), (cu_ref, lrests)):
            both = _dot(m_ref[...], pieces)
            outs.append(both[:, 0:width] + both[:, width:2 * width])
    ld = ld_ref[...]
    lc = jnp.concatenate(lcs, axis=0)
    lrest = jnp.concatenate(lrests, axis=0)
    e_in = jnp.exp(lc)
    e_neg = jnp.exp(-lc)
    e_end = jnp.exp(lrest)
    r_all = r_ref[...].astype(F32)
    k_all = k_ref[...].astype(F32)
    na_all = na_ref[...].astype(F32)
    bv_all = bv_ref[...].astype(F32)
    dec_s[...] = e_in
    rh_s[...] = (r_all * e_in).astype(BF16)
    ah_s[...] = (na_all * jnp.exp(lc - ld)).astype(BF16)
    bt_s[...] = (bv_all * e_neg).astype(BF16)
    kt_s[...] = (k_all * e_neg).astype(BF16)
    bb_s[...] = (bv_all * e_end).astype(BF16)
    kb_s[...] = (k_all * e_end).astype(BF16)

    lane_head = lax.broadcasted_iota(jnp.int32, (1, LANES), 1) // HEAD
    t_idx = lax.broadcasted_iota(jnp.int32, (c, LANES), 0)
    s_idx = lax.broadcasted_iota(jnp.int32, (c, LANES), 1) % c
    strict = s_idx < t_idx
    incl = s_idx <= t_idx
    blk = (lax.broadcasted_iota(jnp.int32, (LANES, LANES), 0) // HEAD
           == lax.broadcasted_iota(jnp.int32, (LANES, LANES), 1) // HEAD)
    eye_cat = (s_idx == t_idx).astype(F32)

    def chunk(ci, states):
        rows = pl.ds(pl.multiple_of(ci * c, c), c)
        tail_rows = pl.ds(pl.multiple_of(ci * c + c - 8, 8), 8)
        pairs = range(n_pairs)
        lanes = [slice(pi * LANES, (pi + 1) * LANES) for pi in pairs]
        each = lambda f: [f(pi) for pi in pairs]
        v = each(lambda pi: v_ref[rows, lanes[pi]])
        rh = each(lambda pi: rh_s[rows, lanes[pi]])
        ah = each(lambda pi: ah_s[rows, lanes[pi]])
        sc = each(lambda pi: _dot_nt(
            jnp.concatenate([ah[pi], rh[pi]], axis=0),
            jnp.concatenate([_head_stack(bt_s[rows, lanes[pi]], lane_head),
                             _head_stack(kt_s[rows, lanes[pi]], lane_head)], axis=0)))
        a_ak = each(lambda pi: jnp.where(strict, sc[pi][0:c, LANES:2 * LANES], 0.0).astype(BF16))
        a_rb = each(lambda pi: jnp.where(incl, sc[pi][c:2 * c, 0:LANES], 0.0).astype(BF16))
        a_rk = each(lambda pi: jnp.where(incl, sc[pi][c:2 * c, LANES:2 * LANES], 0.0).astype(BF16))

        x = each(lambda pi: jnp.where(strict, sc[pi][0:c, 0:LANES], 0.0))
        tinv = each(lambda pi: eye_cat + x[pi])
        p2 = 2
        while p2 < c:
            xb = each(lambda pi: x[pi].astype(BF16))
            x = each(lambda pi: _dot(xb[pi], _head_stack(xb[pi], lane_head)))
            tinv = each(lambda pi: tinv[pi] + _dot(tinv[pi].astype(BF16),
                                                   _head_stack(x[pi].astype(BF16), lane_head)))
            p2 *= 2
        t_cat = each(lambda pi: tinv[pi].astype(BF16))

        g2b = each(lambda pi: states[pi].astype(BF16))
        vs = each(lambda pi: _head_stack(v[pi], lane_head))
        p = each(lambda pi: _dot_nt(ah[pi], g2b[pi]) + _dot(a_ak[pi], vs[pi]))
        ub = each(lambda pi: _dot(t_cat[pi], _head_stack(p[pi].astype(BF16), lane_head)).astype(BF16))
        new_states = []
        for pi in pairs:
            y = (_dot_nt(rh[pi], g2b[pi]) + _dot(a_rb[pi], _head_stack(ub[pi], lane_head))
                 + _dot(a_rk[pi], vs[pi]))
            ybuf[rows, lanes[pi]] = y
            uv_t = jnp.concatenate([ub[pi], v[pi]], axis=0).T
            upd = _dot(uv_t, jnp.concatenate([bb_s[rows, lanes[pi]], kb_s[rows, lanes[pi]]], axis=0))
            decay = dec_s[tail_rows, lanes[pi]][7:8, :]
            new_states.append(states[pi] * decay + jnp.where(blk, upd, 0.0))
        return tuple(new_states)

    final = lax.fori_loop(0, n_chunks, chunk, tuple(state[pi] for pi in range(n_pairs)))
    for pi in range(n_pairs):
        state[pi] = final[pi]

    hb = hb_ref[...]
    gw = hb.shape[0]
    for l0 in range(0, width, gw):
        lanes = slice(l0, l0 + gw)
        y = ybuf[:, lanes]
        mean = _head_sums(y, hb) * (1.0 / HEAD)
        d = y - mean
        var = _head_sums(d * d, hb) * (1.0 / HEAD)
        yn = d * lax.rsqrt(var + RWKV_LN_EPS) * lng_ref[:, lanes] + lnb_ref[:, lanes]
        rk = r_ref[:, lanes].astype(F32) * k_ref[:, lanes].astype(F32) * rk_ref[:, lanes]
        bonus = _head_sums(rk, hb) * v_ref[:, lanes].astype(F32)
        o_ref[:, lanes] = ((yn + bonus) * _silu(g_ref[:, lanes].astype(F32))).astype(o_ref.dtype)


def _head_sums(x, m):
    return functools.reduce(jnp.add, [_dot(p, m) for p in _bf16_pieces(x, SUM_PIECES)])


RWKV_PAIRS = 8


def _rwkv_scan(prep, u3, ln_g, ln_b, r_k, w, gate_col0):
    r, k, v, na, bv, ld = prep
    b, s, _ = r.shape
    ts = _tile(s, 512)
    width = LANES * _tile(w // LANES, RWKV_PAIRS)
    gc0 = gate_col0 // width
    sb = _tile(ts, MXU_DIM)
    chunk_of = np.arange(sb) // RWKV_CHUNK
    same = chunk_of[:, None] == chunk_of[None, :]
    tri = np.arange(sb)[:, None] >= np.arange(sb)[None, :]
    cl = jnp.asarray(same & tri, BF16)
    cu = jnp.asarray(same & ~tri, BF16)
    gw = _tile(width, MXU_DIM)
    head_of = np.arange(gw) // HEAD
    hb = jnp.asarray(head_of[:, None] == head_of[None, :], BF16)
    row = pl.BlockSpec((None, ts, width), lambda i, p, t: (i, t, p))
    vec = pl.BlockSpec((1, width), lambda i, p, t: (0, p))
    const = lambda n: pl.BlockSpec((n, n), lambda i, p, t: (0, 0))
    tile_bf16 = pltpu.VMEM((ts, width), BF16)
    return pl.pallas_call(
        _rwkv_scan_kernel,
        grid=(b, w // width, s // ts),
        in_specs=[row] * 6 + [pl.BlockSpec((None, ts, width), lambda i, p, t: (i, t, gc0 + p)),
                              vec, vec, vec, const(sb), const(sb), const(gw)],
        out_specs=row,
        out_shape=jax.ShapeDtypeStruct((b, s, w), BF16),
        scratch_shapes=[pltpu.VMEM((width // LANES, LANES, LANES), F32), pltpu.VMEM((ts, width), F32),
                        pltpu.VMEM((ts, width), F32)] + [tile_bf16] * 6,
        compiler_params=_params(("parallel", "parallel", "arbitrary")),
        name="rwkv_scan",
    )(r, k, v, na, bv, ld, u3, ln_g, ln_b, r_k, cl, cu, hb)


FOX_BLOCK = 1024
FOX_SUB = 16
FOX_ACC_ROWS = HEAD + 16


def _fold8(x, op):
    parts = [x[i:i + 8, :] for i in range(0, x.shape[0], 8)]
    return functools.reduce(op, parts)


def _fox_kernel(qi_ref, kj_ref, q_ref, k_ref, v_ref, kc_ref, g_ref, o_ref,
                m_ref, acc_ref, qx_ref, s0_ref, s1_ref, p0_ref, p1_ref):
    pair, step = pl.program_id(1), pl.program_id(2)
    qi, kj = qi_ref[step], kj_ref[step]
    tq, tk = q_ref.shape[0], k_ref.shape[0]
    lane = lax.broadcasted_iota(jnp.int32, (1, LANES), 1)

    @pl.when(kj == 0)
    def _():
        m_ref[...] = jnp.full_like(m_ref, NEG_BIG)
        acc_ref[...] = jnp.zeros_like(acc_ref)
        q2 = q_ref[...]
        for h in range(2):
            first = _forget_lane(2 * pair + h, 0)
            ones = jnp.where((lane >= first) & (lane < first + C_PIECES), 1.0, 0.0).astype(BF16)
            qx_ref[h, :, 0:LANES] = jnp.where(lane // HEAD == h, q2, jnp.zeros_like(q2))
            qx_ref[h, :, LANES:2 * LANES] = jnp.broadcast_to(ones, q2.shape)

    s_refs, p_refs = (s0_ref, s1_ref), (p0_ref, p1_ref)

    def step_body(masked):
        kx = jnp.concatenate([k_ref[...], kc_ref[...]], axis=1)
        half = tk // 2
        block_max = [None, None]
        for h in range(2):
            if masked:
                s_refs[h][0:half, :] = _dot_nt(kx[0:half, :], qx_ref[h])
                s_refs[h][half:, half:] = _dot_nt(kx[half:, :], qx_ref[h, half:, :])
            else:
                s_val = _dot_nt(kx, qx_ref[h])
                s_refs[h][...] = s_val
                block_max[h] = _fold8(s_val, jnp.maximum)
        n_sub, n_grp = tk // FOX_SUB, tq // LANES

        def first_group(kb):
            return (kb * FOX_SUB) // LANES if masked else 0

        def group_scores(h, kb, g):
            blk = s_refs[h][kb * FOX_SUB:(kb + 1) * FOX_SUB, g * LANES:(g + 1) * LANES]
            if masked and g == first_group(kb):
                key = kb * FOX_SUB + lax.broadcasted_iota(jnp.int32, (FOX_SUB, LANES), 0)
                query = g * LANES + lax.broadcasted_iota(jnp.int32, (FOX_SUB, LANES), 1)
                blk = jnp.where(key <= query, blk, NEG_BIG)
            return blk

        def scores(h, kb, g0):
            return jnp.concatenate([group_scores(h, kb, g) for g in range(g0, n_grp)], axis=1)

        def running_max(h):
            if masked:
                mx = [None] * n_grp
                for kb in range(n_sub):
                    g0 = first_group(kb)
                    part = _fold8(scores(h, kb, g0), jnp.maximum)
                    for g in range(g0, n_grp):
                        piece = part[:, (g - g0) * LANES:(g - g0 + 1) * LANES]
                        mx[g] = piece if mx[g] is None else jnp.maximum(mx[g], piece)
                mx8 = jnp.concatenate(mx, axis=1)
            else:
                mx8 = block_max[h]
            m_prev = m_ref[h]
            m_new = jnp.maximum(m_prev, jnp.max(mx8, axis=0, keepdims=True))
            m_ref[h] = m_new
            return m_new, jnp.exp2(m_prev - m_new)

        def probabilities(h, m_new):
            for kb in range(n_sub):
                g0 = first_group(kb)
                rows = slice(kb * FOX_SUB, (kb + 1) * FOX_SUB)
                p_refs[h][rows, g0 * LANES:] = jnp.exp2(scores(h, kb, g0) - m_new[:, g0 * LANES:]).astype(BF16)
                z0 = half if kb * FOX_SUB >= half else 0
                if g0 * LANES > z0:
                    p_refs[h][rows, z0:g0 * LANES] = jnp.zeros((FOX_SUB, g0 * LANES - z0), BF16)

        def accumulate(h, alpha):
            lhs = jnp.concatenate([v_t[h * HEAD:(h + 1) * HEAD, :], sum_rows], axis=0)
            if masked:
                acc_ref[h] = alpha * acc_ref[h] + _dot(lhs[:, 0:half], p_refs[h][0:half, :])
                acc_ref[h, :, half:] += _dot(lhs[:, half:], p_refs[h][half:, half:])
            else:
                acc_ref[h] = alpha * acc_ref[h] + _dot(lhs, p_refs[h][...])

        eye = (lax.broadcasted_iota(jnp.int32, (LANES, LANES), 0)
               == lax.broadcasted_iota(jnp.int32, (LANES, LANES), 1)).astype(BF16)
        v_t = _dot_nt(eye, v_ref[...]).astype(BF16)
        sum_rows = jnp.ones((FOX_ACC_ROWS - HEAD, tk), BF16)
        m0, alpha0 = running_max(0)
        probabilities(0, m0)
        m1, alpha1 = running_max(1)
        accumulate(0, alpha0)
        probabilities(1, m1)
        accumulate(1, alpha1)

    @pl.when(kj < qi)
    def _():
        step_body(False)

    @pl.when(kj == qi)
    def _():
        step_body(True)
        outs = [acc_ref[h, 0:HEAD, :] * (1.0 / acc_ref[h, HEAD:HEAD + 1, :]) for h in range(2)]
        out = jnp.concatenate(outs, axis=0).T
        o_ref[...] = (out * _silu(g_ref[...].astype(F32))).astype(o_ref.dtype)


def _fox_branch(u3, kc, w, q_col0, gate_col0):
    b, s, _ = u3.shape
    tq = tk = _tile(s, FOX_BLOCK)
    n_pairs = w // LANES
    pairs = [(i, j) for i in range(s // tq) for j in range(i + 1)]
    qi = jnp.asarray([p[0] for p in pairs], jnp.int32)
    kj = jnp.asarray([p[1] for p in pairs], jnp.int32)
    qc0, gc0 = q_col0 // LANES, gate_col0 // LANES
    grid_spec = pltpu.PrefetchScalarGridSpec(
        num_scalar_prefetch=2,
        grid=(b, n_pairs, len(pairs)),
        in_specs=[
            pl.BlockSpec((None, tq, LANES), lambda i, p, t, qi, kj: (i, qi[t], qc0 + p)),
            pl.BlockSpec((None, tk, LANES), lambda i, p, t, qi, kj: (i, kj[t], qc0 + n_pairs + p)),
            pl.BlockSpec((None, tk, LANES), lambda i, p, t, qi, kj: (i, kj[t], qc0 + 2 * n_pairs + p)),
            pl.BlockSpec((None, tk, LANES), lambda i, p, t, qi, kj: (i, kj[t], 0)),
            pl.BlockSpec((None, tq, LANES), lambda i, p, t, qi, kj: (i, qi[t], gc0 + p)),
        ],
        out_specs=pl.BlockSpec((None, tq, LANES), lambda i, p, t, qi, kj: (i, qi[t], p)),
        scratch_shapes=[pltpu.VMEM((2, 1, tq), F32),
                        pltpu.VMEM((2, FOX_ACC_ROWS, tq), F32), pltpu.VMEM((2, tq, 2 * LANES), BF16),
                        pltpu.VMEM((tk, tq), F32), pltpu.VMEM((tk, tq), F32),
                        pltpu.VMEM((tk, tq), BF16), pltpu.VMEM((tk, tq), BF16)],
    )
    return pl.pallas_call(
        _fox_kernel,
        grid_spec=grid_spec,
        out_shape=jax.ShapeDtypeStruct((b, s, w), BF16),
        compiler_params=_params(("parallel", "parallel", "arbitrary")),
        name="fox_attention",
    )(qi, kj, u3, u3, u3, kc, u3)


def _merge_kernel(ya_ref, yb_ref, yc_ref, yd_ref, ma_ref, mb_ref, mc_ref, md_ref, bm_ref, wb_ref, o_ref):
    acc = None
    branches = ((ya_ref, ma_ref), (yb_ref, mb_ref), (yc_ref, mc_ref), (yd_ref, md_ref))
    for kbr, (y_ref, ml_ref) in enumerate(branches):
        proj = _dot(y_ref[...], wb_ref[kbr])
        gate = _sigmoid(ml_ref[...].astype(F32) + bm_ref[kbr:kbr + 1, :])
        acc = gate * proj if acc is None else acc + gate * proj
    o_ref[...] = acc.astype(o_ref.dtype)


def _merge(ys, u2, b_merge, w_branch_all, layer, ml_col0):
    t, w = ys[0].shape
    _, nb, _, d = w_branch_all.shape
    tm, tn = _tile(t, 1024), _tile(d, 512)
    nd = d // tn
    mc0 = ml_col0 // tn
    yspec = pl.BlockSpec((tm, w), lambda i, j: (i, 0))
    mspec = lambda k: pl.BlockSpec((tm, tn), lambda i, j, k=k: (i, mc0 + k * nd + j))
    return pl.pallas_call(
        _merge_kernel,
        grid=(t // tm, nd),
        in_specs=[yspec] * 4 + [mspec(k) for k in range(4)] + [
            pl.BlockSpec((nb, tn), lambda i, j: (0, j)),
            pl.BlockSpec((None, nb, w, tn), lambda i, j: (layer, 0, 0, j)),
        ],
        out_specs=pl.BlockSpec((tm, tn), lambda i, j: (i, j)),
        out_shape=jax.ShapeDtypeStruct((t, d), BF16),
        compiler_params=_params(("parallel", "arbitrary")),
        name="merge",
    )(*ys, u2, u2, u2, u2, b_merge, w_branch_all)


def _outproj_kernel(m_ref, w_ref, x_ref, o_ref):
    o_ref[...] = x_ref[...] + _dot(m_ref[...], w_ref[...])


def _outproj(merged, w_out_all, layer, x2):
    t, d = x2.shape
    tm, tn = _tile(t, 1024), _tile(d, 512)
    return pl.pallas_call(
        _outproj_kernel,
        grid=(t // tm, d // tn),
        in_specs=[
            pl.BlockSpec((tm, d), lambda i, j: (i, 0)),
            pl.BlockSpec((None, d, tn), lambda i, j: (layer, 0, j)),
            pl.BlockSpec((tm, tn), lambda i, j: (i, j)),
        ],
        out_specs=pl.BlockSpec((tm, tn), lambda i, j: (i, j)),
        out_shape=jax.ShapeDtypeStruct((t, d), F32),
        compiler_params=_params(("parallel", "arbitrary")),
        name="outproj",
    )(merged, w_out_all, x2)


def _final_norm_kernel(x_ref, g_ref, o_ref):
    o_ref[...] = _rms_rows(x_ref[...], g_ref[...])


def _final_norm(x2, g):
    t, d = x2.shape
    tm = _tile(t, 512)
    return pl.pallas_call(
        _final_norm_kernel,
        grid=(t // tm,),
        in_specs=[pl.BlockSpec((tm, d), lambda i: (i, 0)), pl.BlockSpec((1, d), lambda i: (0, 0))],
        out_specs=pl.BlockSpec((tm, d), lambda i: (i, 0)),
        out_shape=jax.ShapeDtypeStruct((t, d), F32),
        compiler_params=_params(("parallel",)),
        name="final_norm",
    )(x2, g)


def _split_in_weights(w_in, w, nh):
    w_t = jnp.swapaxes(w_in, 1, 2)
    lo0, f0 = 7 * w, 11 * w + 2 * LORA
    n_main = w_in.shape[2] - 2 * LORA - nh
    segments = ((0, 0), (lo0, lo0 + 2 * LORA), (f0 - 2 * LORA, f0 + nh))
    col_scale = np.ones((1, n_main), np.float32)
    col_scale[:, 8 * w:9 * w] = HEAD ** -0.5 * LOG2E
    pad = jnp.zeros((w_t.shape[0], LANES - nh, w_t.shape[2]), w_t.dtype)
    w_small = jnp.concatenate([w_t[:, lo0:lo0 + 2 * LORA], w_t[:, f0:f0 + nh], pad], axis=1).astype(BF16)
    return w_t, segments, jnp.asarray(col_scale), w_small


def _layer(x2, bsz, seq, layer, w_in_all, segments, col_scale, w_small, w_branch_all, w_out_all, norm_g, b_merge,
           conv_w, rwkv_mu, rwkv_w0, rwkv_w2, rwkv_a0, rwkv_a2, rwkv_kk, rwkv_ka, rwkv_rk, rwkv_ln_g, rwkv_ln_b,
           fox_bf, pool_w, pool_scale):
    t, d = x2.shape
    w = conv_w.shape[1]
    nh = fox_bf.shape[0]
    b_f = jnp.concatenate([fox_bf, jnp.zeros((LANES - nh,), fox_bf.dtype)]).reshape(1, LANES)
    row = lambda a: a.reshape(1, -1)

    u2 = _inproj(x2, row(norm_g), w_in_all, layer, segments, col_scale)
    u3 = u2.reshape(bsz, seq, -1)
    lora, kc = _side(x2.reshape(bsz, seq, d), row(norm_g), w_small, b_f, w)
    y_a = _conv_branch(u3, conv_w, w, 0)
    prep = _rwkv_prep(u3, lora, rwkv_mu[:3 * w].reshape(3, w), row(rwkv_mu[3 * w:]), row(rwkv_w0),
                      rwkv_w2.astype(BF16), row(rwkv_a0), rwkv_a2.astype(BF16), row(rwkv_kk), row(rwkv_ka),
                      w, 4 * w)
    y_b = _rwkv_scan(prep, u3, row(rwkv_ln_g), row(rwkv_ln_b), row(rwkv_rk), w, 7 * w)
    y_c = _fox_branch(u3, kc, w, 8 * w, 11 * w)
    y_d = _pool_branch(u3, pool_w.astype(BF16), row(pool_scale), w, 12 * w)
    ys = [y.reshape(t, w) for y in (y_a, y_b, y_c, y_d)]
    merged = _merge(ys, u2, b_merge, w_branch_all, layer, 14 * w)
    return _outproj(merged, w_out_all, layer, x2)


def kernel(x, norm_g, w_in, b_merge, conv_w, rwkv_mu, rwkv_w0, rwkv_w2, rwkv_a0, rwkv_a2, rwkv_kk, rwkv_ka,
           rwkv_rk, rwkv_ln_g, rwkv_ln_b, fox_bf, pool_w, pool_scale, w_branch, w_out, final_g):
    bsz, seq, d = x.shape
    x2 = x.reshape(bsz * seq, d)
    w_in_all, segments, col_scale, w_small_all = _split_in_weights(w_in, conv_w.shape[2], fox_bf.shape[1])
    w_branch_all = w_branch.astype(BF16)
    w_out_all = w_out.astype(BF16)
    for l in range(norm_g.shape[0]):
        x2 = _layer(x2, bsz, seq, l, w_in_all, segments, col_scale, w_small_all[l], w_branch_all, w_out_all,
                    norm_g[l], b_merge[l], conv_w[l], rwkv_mu[l], rwkv_w0[l], rwkv_w2[l], rwkv_a0[l], rwkv_a2[l],
                    rwkv_kk[l], rwkv_ka[l], rwkv_rk[l], rwkv_ln_g[l], rwkv_ln_b[l], fox_bf[l], pool_w[l],
                    pool_scale[l])
    return _final_norm(x2, final_g.reshape(1, d)).reshape(bsz, seq, d)
```

```python
import functools

import jax
import jax.numpy as jnp
import numpy as np
from jax import lax
from jax.experimental import pallas as pl
from jax.experimental.pallas import tpu as pltpu

F32 = jnp.float32
BF16 = jnp.bfloat16
HIGHEST = lax.Precision.HIGHEST

NORM_EPS = 1e-6
RWKV_LN_EPS = 64e-5
HEAD = 64
LORA = 64
POOL_WINDOWS = (2, 4, 8, 16)
RWKV_CHUNK = 64
LANES = 128
MXU_DIM = 256
LOG2E = float(np.log2(np.e))
NEG_BIG = -1e30
VMEM_LIMIT = 48 * 1024 * 1024


def _params(sem):
    return pltpu.CompilerParams(dimension_semantics=sem, vmem_limit_bytes=VMEM_LIMIT)


def _tile(n, want):
    t = min(n, want)
    while n % t:
        t //= 2
    return t


def _sigmoid(x):
    return 1.0 / (1.0 + jnp.exp(-x))


def _silu(x):
    return x * _sigmoid(x)


def _rms_rows(xf, g):
    ms = jnp.mean(xf * xf, axis=-1, keepdims=True)
    return xf * lax.rsqrt(ms + NORM_EPS) * g


def _dot(a, b):
    return jnp.dot(a, b, preferred_element_type=F32)


def _dot_nt(a, b):
    return lax.dot_general(a, b, (((1,), (1,)), ((), ())), preferred_element_type=F32)


def _dot_f32(a, b):
    return jnp.dot(a, b, precision=HIGHEST, preferred_element_type=F32)


def _inproj_kernel(x_ref, g_ref, sc_ref, w_ref, o_ref, h_ref):
    @pl.when(pl.program_id(1) == 0)
    def _():
        h_ref[...] = _rms_rows(x_ref[...], g_ref[...]).astype(BF16)

    o_ref[...] = (_dot_nt(h_ref[...], w_ref[0]) * sc_ref[...]).astype(o_ref.dtype)


def _inproj(x2, g, w_all, layer, segments, col_scale):
    t, d = x2.shape
    n = col_scale.shape[1]
    bounds = [c0 for c0, _ in segments] + [n]
    tn = _tile(functools.reduce(np.gcd, np.diff(bounds)), 1024)
    tm = _tile(t, 1024)

    align = int(functools.reduce(np.gcd, [tn] + [r0 for _, r0 in segments[1:]]))

    def weight_row(j):
        row = j * tn
        for (c0, r0), (c1, r1) in zip(segments[:-1], segments[1:]):
            row = row + jnp.where(j * tn >= c1, (r1 - c1) - (r0 - c0), 0)
        return pl.multiple_of(row + (segments[0][1] - segments[0][0]), align)

    return pl.pallas_call(
        _inproj_kernel,
        grid=(t // tm, n // tn),
        in_specs=[
            pl.BlockSpec((tm, d), lambda i, j: (i, 0)),
            pl.BlockSpec((1, d), lambda i, j: (0, 0)),
            pl.BlockSpec((1, tn), lambda i, j: (0, j)),
            pl.BlockSpec((pl.Element(1), pl.Element(tn), pl.Element(d)),
                         lambda i, j: (layer, weight_row(j), 0)),
        ],
        out_specs=pl.BlockSpec((tm, tn), lambda i, j: (i, j)),
        out_shape=jax.ShapeDtypeStruct((t, n), BF16),
        scratch_shapes=[pltpu.VMEM((tm, d), BF16)],
        compiler_params=_params(("parallel", "arbitrary")),
        name="inproj",
    )(x2, g, col_scale, w_all)


C_PIECES = 3


def _bf16_pieces(x, n=C_PIECES):
    pieces = []
    for _ in range(n):
        p = x.astype(BF16)
        pieces.append(p)
        x = x - p.astype(F32)
    return pieces


def _side_kernel(x_ref, g_ref, ws_ref, bf_ref, sel_ref, lora_ref, kc_ref, carry_ref):
    @pl.when(pl.program_id(1) == 0)
    def _():
        carry_ref[...] = jnp.zeros_like(carry_ref)

    h = _rms_rows(x_ref[...], g_ref[...]).astype(BF16)
    sf = _dot_nt(h, ws_ref[...])
    ns = lora_ref.shape[1]
    lora_ref[...] = sf[:, :ns]
    z = sf[:, ns:] + bf_ref[...]
    logf = jnp.minimum(z, 0.0) - jnp.log1p(jnp.exp(-jnp.abs(z)))
    ts = logf.shape[0]
    lower = (lax.broadcasted_iota(jnp.int32, (ts, ts), 0)
             >= lax.broadcasted_iota(jnp.int32, (ts, ts), 1)).astype(BF16)
    c = functools.reduce(jnp.add, [_dot(lower, p) for p in _bf16_pieces(logf)]) + carry_ref[...]
    carry_ref[...] = c[ts - 1:ts, :]
    kc = None
    for piece, sel in zip(_bf16_pieces(-LOG2E * c), (sel_ref[0], sel_ref[1], sel_ref[2])):
        term = _dot(piece, sel)
        kc = term if kc is None else kc + term
    kc_ref[...] = kc.astype(kc_ref.dtype)


def _forget_lane(head, piece):
    return C_PIECES * head + piece


def _side(x3, g, w_small, b_f, w):
    b, s, d = x3.shape
    ns = w_small.shape[0] - LANES
    ts = _tile(s, 512)
    nh = w // HEAD
    assert C_PIECES * nh <= LANES
    sel = np.zeros((C_PIECES, LANES, LANES), np.float32)
    for hd in range(nh):
        for piece in range(C_PIECES):
            sel[piece, hd, _forget_lane(hd, piece)] = 1.0
    return pl.pallas_call(
        _side_kernel,
        grid=(b, s // ts),
        in_specs=[
            pl.BlockSpec((None, ts, d), lambda i, j: (i, j, 0)),
            pl.BlockSpec((1, d), lambda i, j: (0, 0)),
            pl.BlockSpec((ns + LANES, d), lambda i, j: (0, 0)),
            pl.BlockSpec((1, LANES), lambda i, j: (0, 0)),
            pl.BlockSpec((C_PIECES, LANES, LANES), lambda i, j: (0, 0, 0)),
        ],
        out_specs=[
            pl.BlockSpec((None, ts, ns), lambda i, j: (i, j, 0)),
            pl.BlockSpec((None, ts, LANES), lambda i, j: (i, j, 0)),
        ],
        out_shape=[
            jax.ShapeDtypeStruct((b, s, ns), F32),
            jax.ShapeDtypeStruct((b, s, LANES), BF16),
        ],
        scratch_shapes=[pltpu.VMEM((1, LANES), F32)],
        compiler_params=_params(("parallel", "arbitrary")),
        name="side",
    )(x3, g, w_small, b_f, jnp.asarray(sel, BF16))


HALO = 8


def _conv_kernel(bg_ref, cg_ref, xv_ref, g_ref, cw_ref, o_ref, pbuf):
    ts = o_ref.shape[0]

    @pl.when(pl.program_id(2) == 0)
    def _():
        pbuf[0:HALO, :] = jnp.zeros((HALO, pbuf.shape[1]), F32)

    p = cg_ref[...].astype(F32) * xv_ref[...].astype(F32)
    pbuf[HALO:HALO + ts, :] = p
    cw = cw_ref[...]
    z = (cw[0:1] * pbuf[HALO - 2:HALO - 2 + ts, :] + cw[1:2] * pbuf[HALO - 1:HALO - 1 + ts, :]
         + cw[2:3] * p)
    o_ref[...] = (bg_ref[...].astype(F32) * z * _silu(g_ref[...].astype(F32))).astype(o_ref.dtype)
    pbuf[0:HALO, :] = pbuf[ts:ts + HALO, :]


def _conv_branch(u3, conv_w, w, col0):
    b, s, _ = u3.shape
    ts, tw = _tile(s, 512), _tile(w, 512)
    nw = w // tw
    c0 = col0 // tw

    def spec(k):
        return pl.BlockSpec((None, ts, tw), lambda i, j, t, k=k: (i, t, c0 + k * nw + j))

    return pl.pallas_call(
        _conv_kernel,
        grid=(b, nw, s // ts),
        in_specs=[spec(0), spec(1), spec(2), spec(3),
                  pl.BlockSpec((conv_w.shape[0], tw), lambda i, j, t: (0, j))],
        out_specs=pl.BlockSpec((None, ts, tw), lambda i, j, t: (i, t, j)),
        out_shape=jax.ShapeDtypeStruct((b, s, w), BF16),
        scratch_shapes=[pltpu.VMEM((HALO + ts, tw), F32)],
        compiler_params=_params(("parallel", "parallel", "arbitrary")),
        name="conv_branch",
    )(u3, u3, u3, u3, conv_w)


POOL_HALO = 16


def _pool_kernel(x_ref, g_ref, pw_ref, sc_ref, o_ref, xbuf):
    ts, w = o_ref.shape
    gw = w // len(POOL_WINDOWS)
    s = pl.program_id(1)

    @pl.when(s == 0)
    def _():
        xbuf[0:POOL_HALO, :] = jnp.zeros((POOL_HALO, w), F32)

    x = x_ref[...].astype(F32)
    xbuf[POOL_HALO:POOL_HALO + ts, :] = x
    pos = s * ts + lax.broadcasted_iota(jnp.int32, (ts, 1), 0)
    for gi, win in enumerate(POOL_WINDOWS):
        lo, hi = gi * gw, (gi + 1) * gw
        xi = x[:, lo:hi]
        acc = xi
        for k in range(1, win):
            acc = acc + xbuf[POOL_HALO - k:POOL_HALO - k + ts, lo:hi]
        count = jnp.minimum(pos + 1, win).astype(F32)
        pooled = acc / count - xi
        y = _dot(pooled.astype(BF16), pw_ref[gi])
        o_ref[:, lo:hi] = (y * sc_ref[:, lo:hi] * _silu(g_ref[:, lo:hi].astype(F32))).astype(o_ref.dtype)
    xbuf[0:POOL_HALO, :] = xbuf[ts:ts + POOL_HALO, :]


def _pool_branch(u3, pool_w, pool_scale, w, col0):
    b, s, _ = u3.shape
    ts = _tile(s, 512)
    c0 = col0 // w
    ng, gw, _ = pool_w.shape
    return pl.pallas_call(
        _pool_kernel,
        grid=(b, s // ts),
        in_specs=[
            pl.BlockSpec((None, ts, w), lambda i, t: (i, t, c0)),
            pl.BlockSpec((None, ts, w), lambda i, t: (i, t, c0 + 1)),
            pl.BlockSpec((ng, gw, gw), lambda i, t: (0, 0, 0)),
            pl.BlockSpec((1, w), lambda i, t: (0, 0)),
        ],
        out_specs=pl.BlockSpec((None, ts, w), lambda i, t: (i, t, 0)),
        out_shape=jax.ShapeDtypeStruct((b, s, w), BF16),
        scratch_shapes=[pltpu.VMEM((POOL_HALO + ts, w), F32)],
        compiler_params=_params(("parallel", "arbitrary")),
        name="pool_branch",
    )(u3, u3, pool_w, pool_scale)


def _rwkv_prep_kernel(r_ref, k_ref, v_ref, lo_ref, mu_ref, mul_ref, w0_ref, w2_ref, a0_ref, a2_ref,
                      kkw_ref, kaw_ref, hb_ref,
                      ro_ref, ko_ref, vo_ref, na_ref, bv_ref, ld_ref, buf, lbuf):
    ts, w = ro_ref.shape

    @pl.when(pl.program_id(1) == 0)
    def _():
        buf[0:HALO, :] = jnp.zeros((HALO, buf.shape[1]), F32)
        lbuf[0:HALO, :] = jnp.zeros((HALO, lbuf.shape[1]), F32)

    for idx, ref in enumerate((r_ref, k_ref, v_ref)):
        buf[HALO:HALO + ts, idx * w:(idx + 1) * w] = ref[...].astype(F32)
    lbuf[HALO:HALO + ts, :] = lo_ref[...]

    def mixed(cur, prev, mu):
        return cur + (prev - cur) * mu

    mu = mu_ref[...]
    r = mixed(buf[HALO:HALO + ts, 0:w], buf[HALO - 1:HALO - 1 + ts, 0:w], mu[0:1])
    k = mixed(buf[HALO:HALO + ts, w:2 * w], buf[HALO - 1:HALO - 1 + ts, w:2 * w], mu[1:2])
    v = mixed(buf[HALO:HALO + ts, 2 * w:3 * w], buf[HALO - 1:HALO - 1 + ts, 2 * w:3 * w], mu[2:3])
    lo = mixed(lbuf[HALO:HALO + ts, :], lbuf[HALO - 1:HALO - 1 + ts, :], mul_ref[...])
    wl, al = lo[:, 0:LORA], lo[:, LORA:2 * LORA]

    z = w0_ref[...] + _dot(jnp.tanh(wl).astype(BF16), w2_ref[...])
    ld_ref[...] = -float(np.exp(-0.5)) * _sigmoid(z)
    a = _sigmoid(a0_ref[...] + _dot(al.astype(BF16), a2_ref[...]))

    kk = k * kkw_ref[...]
    hb = hb_ref[...]
    gw = hb.shape[0]
    kk2 = kk * kk
    n2 = jnp.concatenate([_head_sums(kk2[:, l0:l0 + gw], hb) for l0 in range(0, w, gw)], axis=1)
    kk = kk * lax.rsqrt(jnp.maximum(n2, 1e-24))
    ro_ref[...] = r.astype(ro_ref.dtype)
    ko_ref[...] = (k * (1.0 + (a - 1.0) * kaw_ref[...])).astype(ko_ref.dtype)
    vo_ref[...] = v.astype(vo_ref.dtype)
    na_ref[...] = (-kk).astype(na_ref.dtype)
    bv_ref[...] = (kk * a).astype(bv_ref.dtype)

    buf[0:HALO, :] = buf[ts:ts + HALO, :]
    lbuf[0:HALO, :] = lbuf[ts:ts + HALO, :]


def _rwkv_prep(u3, lora, mu3, mul, w0, w2, a0, a2, kkw, kaw, w, col0):
    b, s, _ = u3.shape
    ts = _tile(s, 256)
    c0 = col0 // w
    gw = _tile(w, MXU_DIM)
    head_of = np.arange(gw) // HEAD
    hb = jnp.asarray(head_of[:, None] == head_of[None, :], BF16)
    row = lambda k: pl.BlockSpec((None, ts, w), lambda i, t, k=k: (i, t, c0 + k))
    full = lambda shape: pl.BlockSpec(shape, lambda i, t: (0,) * len(shape))
    outw = pl.BlockSpec((None, ts, w), lambda i, t: (i, t, 0))
    sds = lambda dt: jax.ShapeDtypeStruct((b, s, w), dt)
    return pl.pallas_call(
        _rwkv_prep_kernel,
        grid=(b, s // ts),
        in_specs=[row(0), row(1), row(2),
                  pl.BlockSpec((None, ts, 2 * LORA), lambda i, t: (i, t, 0)),
                  full((3, w)), full((1, 2 * LORA)), full((1, w)), full((LORA, w)),
                  full((1, w)), full((LORA, w)), full((1, w)), full((1, w)),
                  full((gw, gw))],
        out_specs=[outw] * 6,
        out_shape=[sds(BF16)] * 5 + [sds(F32)],
        scratch_shapes=[pltpu.VMEM((HALO + ts, 3 * w), F32), pltpu.VMEM((HALO + ts, 2 * LORA), F32)],
        compiler_params=_params(("parallel", "arbitrary")),
        name="rwkv_prep",
    )(u3, u3, u3, lora, mu3, mul, w0, w2, a0, a2, kkw, kaw, hb)


def _head_stack(x, lane_head):
    zero = jnp.zeros_like(x)
    return jnp.concatenate([jnp.where(lane_head == 0, x, zero), jnp.where(lane_head == 1, x, zero)], axis=0)


SUM_PIECES = 2


def _rwkv_scan_kernel(r_ref, k_ref, v_ref, na_ref, bv_ref, ld_ref, g_ref, lng_ref, lnb_ref, rk_ref,
                      cl_ref, cu_ref, hb_ref, o_ref, state, ybuf, dec_s, rh_s, ah_s, bt_s, kt_s, bb_s, kb_s):
    ts, width = o_ref.shape
    n_pairs = width // LANES
    c = RWKV_CHUNK
    n_chunks = ts // c

    @pl.when(pl.program_id(2) == 0)
    def _():
        state[...] = jnp.zeros_like(state)

    sb = cl_ref.shape[0]
    lcs, lrests = [], []
    for r0 in range(0, ts, sb):
        pieces = jnp.concatenate(_bf16_pieces(ld_ref[r0:r0 + sb, :], SUM_PIECES), axis=1)
        for m_ref, outs in ((cl_ref, lcs), (cu_ref, lrests)):
            both = _dot(m_ref[...], pieces)
            outs.append(both[:, 0:width] + both[:, width:2 * width])
    ld = ld_ref[...]
    lc = jnp.concatenate(lcs, axis=0)
    lrest = jnp.concatenate(lrests, axis=0)
    e_in = jnp.exp(lc)
    e_neg = jnp.exp(-lc)
    e_end = jnp.exp(lrest)
    r_all = r_ref[...].astype(F32)
    k_all = k_ref[...].astype(F32)
    na_all = na_ref[...].astype(F32)
    bv_all = bv_ref[...].astype(F32)
    dec_s[...] = e_in
    rh_s[...] = (r_all * e_in).astype(BF16)
    ah_s[...] = (na_all * jnp.exp(lc - ld)).astype(BF16)
    bt_s[...] = (bv_all * e_neg).astype(BF16)
    kt_s[...] = (k_all * e_neg).astype(BF16)
    bb_s[...] = (bv_all * e_end).astype(BF16)
    kb_s[...] = (k_all * e_end).astype(BF16)

    lane_head = lax.broadcasted_iota(jnp.int32, (1, LANES), 1) // HEAD
    t_idx = lax.broadcasted_iota(jnp.int32, (c, LANES), 0)
    s_idx = lax.broadcasted_iota(jnp.int32, (c, LANES), 1) % c
    strict = s_idx < t_idx
    incl = s_idx <= t_idx
    blk = (lax.broadcasted_iota(jnp.int32, (LANES, LANES), 0) // HEAD
           == lax.broadcasted_iota(jnp.int32, (LANES, LANES), 1) // HEAD)
    eye_cat = (s_idx == t_idx).astype(F32)

    def chunk(ci, states):
        rows = pl.ds(pl.multiple_of(ci * c, c), c)
        tail_rows = pl.ds(pl.multiple_of(ci * c + c - 8, 8), 8)
        pairs = range(n_pairs)
        lanes = [slice(pi * LANES, (pi + 1) * LANES) for pi in pairs]
        each = lambda f: [f(pi) for pi in pairs]
        v = each(lambda pi: v_ref[rows, lanes[pi]])
        rh = each(lambda pi: rh_s[rows, lanes[pi]])
        ah = each(lambda pi: ah_s[rows, lanes[pi]])
        sc = each(lambda pi: _dot_nt(
            jnp.concatenate([ah[pi], rh[pi]], axis=0),
            jnp.concatenate([_head_stack(bt_s[rows, lanes[pi]], lane_head),
                             _head_stack(kt_s[rows, lanes[pi]], lane_head)], axis=0)))
        a_ak = each(lambda pi: jnp.where(strict, sc[pi][0:c, LANES:2 * LANES], 0.0).astype(BF16))
        a_rb = each(lambda pi: jnp.where(incl, sc[pi][c:2 * c, 0:LANES], 0.0).astype(BF16))
        a_rk = each(lambda pi: jnp.where(incl, sc[pi][c:2 * c, LANES:2 * LANES], 0.0).astype(BF16))

        x = each(lambda pi: jnp.where(strict, sc[pi][0:c, 0:LANES], 0.0))
        tinv = each(lambda pi: eye_cat + x[pi])
        p2 = 2
        while p2 < c:
            xb = each(lambda pi: x[pi].astype(BF16))
            x = each(lambda pi: _dot(xb[pi], _head_stack(xb[pi], lane_head)))
            tinv = each(lambda pi: tinv[pi] + _dot(tinv[pi].astype(BF16),
                                                   _head_stack(x[pi].astype(BF16), lane_head)))
            p2 *= 2
        t_cat = each(lambda pi: tinv[pi].astype(BF16))

        g2b = each(lambda pi: states[pi].astype(BF16))
        vs = each(lambda pi: _head_stack(v[pi], lane_head))
        p = each(lambda pi: _dot_nt(ah[pi], g2b[pi]) + _dot(a_ak[pi], vs[pi]))
        ub = each(lambda pi: _dot(t_cat[pi], _head_stack(p[pi].astype(BF16), lane_head)).astype(BF16))
        new_states = []
        for pi in pairs:
            y = (_dot_nt(rh[pi], g2b[pi]) + _dot(a_rb[pi], _head_stack(ub[pi], lane_head))
                 + _dot(a_rk[pi], vs[pi]))
            ybuf[rows, lanes[pi]] = y
            uv_t = jnp.concatenate([ub[pi], v[pi]], axis=0).T
            upd = _dot(uv_t, jnp.concatenate([bb_s[rows, lanes[pi]], kb_s[rows, lanes[pi]]], axis=0))
            decay = dec_s[tail_rows, lanes[pi]][7:8, :]
            new_states.append(states[pi] * decay + jnp.where(blk, upd, 0.0))
        return tuple(new_states)

    final = lax.fori_loop(0, n_chunks, chunk, tuple(state[pi] for pi in range(n_pairs)), unroll=4)
    for pi in range(n_pairs):
        state[pi] = final[pi]

    hb = hb_ref[...]
    gw = hb.shape[0]
    for l0 in range(0, width, gw):
        lanes = slice(l0, l0 + gw)
        y = ybuf[:, lanes]
        mean = _head_sums(y, hb) * (1.0 / HEAD)
        d = y - mean
        var = _head_sums(d * d, hb) * (1.0 / HEAD)
        yn = d * lax.rsqrt(var + RWKV_LN_EPS) * lng_ref[:, lanes] + lnb_ref[:, lanes]
        rk = r_ref[:, lanes].astype(F32) * k_ref[:, lanes].astype(F32) * rk_ref[:, lanes]
        bonus = _head_sums(rk, hb) * v_ref[:, lanes].astype(F32)
        o_ref[:, lanes] = ((yn + bonus) * _silu(g_ref[:, lanes].astype(F32))).astype(o_ref.dtype)


def _head_sums(x, m):
    return functools.reduce(jnp.add, [_dot(p, m) for p in _bf16_pieces(x, SUM_PIECES)])


RWKV_PAIRS = 8


def _rwkv_scan(prep, u3, ln_g, ln_b, r_k, w, gate_col0):
    r, k, v, na, bv, ld = prep
    b, s, _ = r.shape
    ts = _tile(s, 512)
    width = LANES * _tile(w // LANES, RWKV_PAIRS)
    gc0 = gate_col0 // width
    sb = _tile(ts, MXU_DIM)
    chunk_of = np.arange(sb) // RWKV_CHUNK
    same = chunk_of[:, None] == chunk_of[None, :]
    tri = np.arange(sb)[:, None] >= np.arange(sb)[None, :]
    cl = jnp.asarray(same & tri, BF16)
    cu = jnp.asarray(same & ~tri, BF16)
    gw = _tile(width, MXU_DIM)
    head_of = np.arange(gw) // HEAD
    hb = jnp.asarray(head_of[:, None] == head_of[None, :], BF16)
    row = pl.BlockSpec((None, ts, width), lambda i, p, t: (i, t, p))
    vec = pl.BlockSpec((1, width), lambda i, p, t: (0, p))
    const = lambda n: pl.BlockSpec((n, n), lambda i, p, t: (0, 0))
    tile_bf16 = pltpu.VMEM((ts, width), BF16)
    return pl.pallas_call(
        _rwkv_scan_kernel,
        grid=(b, w // width, s // ts),
        in_specs=[row] * 6 + [pl.BlockSpec((None, ts, width), lambda i, p, t: (i, t, gc0 + p)),
                              vec, vec, vec, const(sb), const(sb), const(gw)],
        out_specs=row,
        out_shape=jax.ShapeDtypeStruct((b, s, w), BF16),
        scratch_shapes=[pltpu.VMEM((width // LANES, LANES, LANES), F32), pltpu.VMEM((ts, width), F32),
                        pltpu.VMEM((ts, width), F32)] + [tile_bf16] * 6,
        compiler_params=_params(("parallel", "parallel", "arbitrary")),
        name="rwkv_scan",
    )(r, k, v, na, bv, ld, u3, ln_g, ln_b, r_k, cl, cu, hb)


FOX_BLOCK = 1024
FOX_SUB = 16
FOX_ACC_ROWS = HEAD + 16


def _fold8(x, op):
    parts = [x[i:i + 8, :] for i in range(0, x.shape[0], 8)]
    return functools.reduce(op, parts)


def _fox_kernel(qi_ref, kj_ref, q_ref, k_ref, v_ref, kc_ref, g_ref, o_ref,
                m_ref, acc_ref, qx_ref, s0_ref, s1_ref, p0_ref, p1_ref):
    pair, step = pl.program_id(1), pl.program_id(2)
    qi, kj = qi_ref[step], kj_ref[step]
    tq, tk = q_ref.shape[0], k_ref.shape[0]
    lane = lax.broadcasted_iota(jnp.int32, (1, LANES), 1)

    @pl.when(kj == 0)
    def _():
        m_ref[...] = jnp.full_like(m_ref, NEG_BIG)
        acc_ref[...] = jnp.zeros_like(acc_ref)
        q2 = q_ref[...]
        for h in range(2):
            first = _forget_lane(2 * pair + h, 0)
            ones = jnp.where((lane >= first) & (lane < first + C_PIECES), 1.0, 0.0).astype(BF16)
            qx_ref[h, :, 0:LANES] = jnp.where(lane // HEAD == h, q2, jnp.zeros_like(q2))
            qx_ref[h, :, LANES:2 * LANES] = jnp.broadcast_to(ones, q2.shape)

    s_refs, p_refs = (s0_ref, s1_ref), (p0_ref, p1_ref)

    def step_body(masked):
        kx = jnp.concatenate([k_ref[...], kc_ref[...]], axis=1)
        half = tk // 2
        block_max = [None, None]
        for h in range(2):
            if masked:
                s_refs[h][0:half, :] = _dot_nt(kx[0:half, :], qx_ref[h])
                s_refs[h][half:, half:] = _dot_nt(kx[half:, :], qx_ref[h, half:, :])
            else:
                s_val = _dot_nt(kx, qx_ref[h])
                s_refs[h][...] = s_val
                block_max[h] = _fold8(s_val, jnp.maximum)
        n_sub, n_grp = tk // FOX_SUB, tq // LANES

        def first_group(kb):
            return (kb * FOX_SUB) // LANES if masked else 0

        def group_scores(h, kb, g):
            blk = s_refs[h][kb * FOX_SUB:(kb + 1) * FOX_SUB, g * LANES:(g + 1) * LANES]
            if masked and g == first_group(kb):
                key = kb * FOX_SUB + lax.broadcasted_iota(jnp.int32, (FOX_SUB, LANES), 0)
                query = g * LANES + lax.broadcasted_iota(jnp.int32, (FOX_SUB, LANES), 1)
                blk = jnp.where(key <= query, blk, NEG_BIG)
            return blk

        def scores(h, kb, g0):
            return jnp.concatenate([group_scores(h, kb, g) for g in range(g0, n_grp)], axis=1)

        def running_max(h):
            if masked:
                mx = [None] * n_grp
                for kb in range(n_sub):
                    g0 = first_group(kb)
                    part = _fold8(scores(h, kb, g0), jnp.maximum)
                    for g in range(g0, n_grp):
                        piece = part[:, (g - g0) * LANES:(g - g0 + 1) * LANES]
                        mx[g] = piece if mx[g] is None else jnp.maximum(mx[g], piece)
                mx8 = jnp.concatenate(mx, axis=1)
            else:
                mx8 = block_max[h]
            m_prev = m_ref[h]
            m_new = jnp.maximum(m_prev, jnp.max(mx8, axis=0, keepdims=True))
            m_ref[h] = m_new
            return m_new, jnp.exp2(m_prev - m_new)

        def probabilities(h, m_new):
            for kb in range(n_sub):
                g0 = first_group(kb)
                rows = slice(kb * FOX_SUB, (kb + 1) * FOX_SUB)
                p_refs[h][rows, g0 * LANES:] = jnp.exp2(scores(h, kb, g0) - m_new[:, g0 * LANES:]).astype(BF16)
                z0 = half if kb * FOX_SUB >= half else 0
                if g0 * LANES > z0:
                    p_refs[h][rows, z0:g0 * LANES] = jnp.zeros((FOX_SUB, g0 * LANES - z0), BF16)

        def accumulate(h, alpha):
            lhs = jnp.concatenate([v_t[h * HEAD:(h + 1) * HEAD, :], sum_rows], axis=0)
            if masked:
                acc_ref[h] = alpha * acc_ref[h] + _dot(lhs[:, 0:half], p_refs[h][0:half, :])
                acc_ref[h, :, half:] += _dot(lhs[:, half:], p_refs[h][half:, half:])
            else:
                acc_ref[h] = alpha * acc_ref[h] + _dot(lhs, p_refs[h][...])

        eye = (lax.broadcasted_iota(jnp.int32, (LANES, LANES), 0)
               == lax.broadcasted_iota(jnp.int32, (LANES, LANES), 1)).astype(BF16)
        v_t = _dot_nt(eye, v_ref[...]).astype(BF16)
        sum_rows = jnp.ones((FOX_ACC_ROWS - HEAD, tk), BF16)
        m0, alpha0 = running_max(0)
        probabilities(0, m0)
        m1, alpha1 = running_max(1)
        accumulate(0, alpha0)
        probabilities(1, m1)
        accumulate(1, alpha1)

    @pl.when(kj < qi)
    def _():
        step_body(False)

    @pl.when(kj == qi)
    def _():
        step_body(True)
        outs = [acc_ref[h, 0:HEAD, :] * (1.0 / acc_ref[h, HEAD:HEAD + 1, :]) for h in range(2)]
        out = jnp.concatenate(outs, axis=0).T
        o_ref[...] = (out * _silu(g_ref[...].astype(F32))).astype(o_ref.dtype)


def _fox_branch(u3, kc, w, q_col0, gate_col0):
    b, s, _ = u3.shape
    tq = tk = _tile(s, FOX_BLOCK)
    n_pairs = w // LANES
    pairs = [(i, j) for i in range(s // tq) for j in range(i + 1)]
    qi = jnp.asarray([p[0] for p in pairs], jnp.int32)
    kj = jnp.asarray([p[1] for p in pairs], jnp.int32)
    qc0, gc0 = q_col0 // LANES, gate_col0 // LANES
    grid_spec = pltpu.PrefetchScalarGridSpec(
        num_scalar_prefetch=2,
        grid=(b, n_pairs, len(pairs)),
        in_specs=[
            pl.BlockSpec((None, tq, LANES), lambda i, p, t, qi, kj: (i, qi[t], qc0 + p)),
            pl.BlockSpec((None, tk, LANES), lambda i, p, t, qi, kj: (i, kj[t], qc0 + n_pairs + p)),
            pl.BlockSpec((None, tk, LANES), lambda i, p, t, qi, kj: (i, kj[t], qc0 + 2 * n_pairs + p)),
            pl.BlockSpec((None, tk, LANES), lambda i, p, t, qi, kj: (i, kj[t], 0)),
            pl.BlockSpec((None, tq, LANES), lambda i, p, t, qi, kj: (i, qi[t], gc0 + p)),
        ],
        out_specs=pl.BlockSpec((None, tq, LANES), lambda i, p, t, qi, kj: (i, qi[t], p)),
        scratch_shapes=[pltpu.VMEM((2, 1, tq), F32),
                        pltpu.VMEM((2, FOX_ACC_ROWS, tq), F32), pltpu.VMEM((2, tq, 2 * LANES), BF16),
                        pltpu.VMEM((tk, tq), F32), pltpu.VMEM((tk, tq), F32),
                        pltpu.VMEM((tk, tq), BF16), pltpu.VMEM((tk, tq), BF16)],
    )
    return pl.pallas_call(
        _fox_kernel,
        grid_spec=grid_spec,
        out_shape=jax.ShapeDtypeStruct((b, s, w), BF16),
        compiler_params=_params(("parallel", "parallel", "arbitrary")),
        name="fox_attention",
    )(qi, kj, u3, u3, u3, kc, u3)


def _merge_kernel(ya_ref, yb_ref, yc_ref, yd_ref, ma_ref, mb_ref, mc_ref, md_ref, bm_ref, wb_ref, o_ref):
    acc = None
    branches = ((ya_ref, ma_ref), (yb_ref, mb_ref), (yc_ref, mc_ref), (yd_ref, md_ref))
    for kbr, (y_ref, ml_ref) in enumerate(branches):
        proj = _dot(y_ref[...], wb_ref[kbr])
        gate = _sigmoid(ml_ref[...].astype(F32) + bm_ref[kbr:kbr + 1, :])
        acc = gate * proj if acc is None else acc + gate * proj
    o_ref[...] = acc.astype(o_ref.dtype)


def _merge(ys, u2, b_merge, w_branch_all, layer, ml_col0):
    t, w = ys[0].shape
    _, nb, _, d = w_branch_all.shape
    tm, tn = _tile(t, 1024), _tile(d, 512)
    nd = d // tn
    mc0 = ml_col0 // tn
    yspec = pl.BlockSpec((tm, w), lambda i, j: (i, 0))
    mspec = lambda k: pl.BlockSpec((tm, tn), lambda i, j, k=k: (i, mc0 + k * nd + j))
    return pl.pallas_call(
        _merge_kernel,
        grid=(t // tm, nd),
        in_specs=[yspec] * 4 + [mspec(k) for k in range(4)] + [
            pl.BlockSpec((nb, tn), lambda i, j: (0, j)),
            pl.BlockSpec((None, nb, w, tn), lambda i, j: (layer, 0, 0, j)),
        ],
        out_specs=pl.BlockSpec((tm, tn), lambda i, j: (i, j)),
        out_shape=jax.ShapeDtypeStruct((t, d), BF16),
        compiler_params=_params(("parallel", "arbitrary")),
        name="merge",
    )(*ys, u2, u2, u2, u2, b_merge, w_branch_all)


def _outproj_kernel(m_ref, w_ref, x_ref, o_ref):
    o_ref[...] = x_ref[...] + _dot(m_ref[...], w_ref[...])


def _outproj(merged, w_out_all, layer, x2):
    t, d = x2.shape
    tm, tn = _tile(t, 1024), _tile(d, 1024)
    return pl.pallas_call(
        _outproj_kernel,
        grid=(t // tm, d // tn),
        in_specs=[
            pl.BlockSpec((tm, d), lambda i, j: (i, 0)),
            pl.BlockSpec((None, d, tn), lambda i, j: (layer, 0, j)),
            pl.BlockSpec((tm, tn), lambda i, j: (i, j)),
        ],
        out_specs=pl.BlockSpec((tm, tn), lambda i, j: (i, j)),
        out_shape=jax.ShapeDtypeStruct((t, d), F32),
        compiler_params=_params(("parallel", "arbitrary")),
        name="outproj",
    )(merged, w_out_all, x2)


def _final_norm_kernel(x_ref, g_ref, o_ref):
    o_ref[...] = _rms_rows(x_ref[...], g_ref[...])


def _final_norm(x2, g):
    t, d = x2.shape
    tm = _tile(t, 512)
    return pl.pallas_call(
        _final_norm_kernel,
        grid=(t // tm,),
        in_specs=[pl.BlockSpec((tm, d), lambda i: (i, 0)), pl.BlockSpec((1, d), lambda i: (0, 0))],
        out_specs=pl.BlockSpec((tm, d), lambda i: (i, 0)),
        out_shape=jax.ShapeDtypeStruct((t, d), F32),
        compiler_params=_params(("parallel",)),
        name="final_norm",
    )(x2, g)


def _split_in_weights(w_in, w, nh):
    w_t = jnp.swapaxes(w_in, 1, 2)
    lo0, f0 = 7 * w, 11 * w + 2 * LORA
    n_main = w_in.shape[2] - 2 * LORA - nh
    segments = ((0, 0), (lo0, lo0 + 2 * LORA), (f0 - 2 * LORA, f0 + nh))
    col_scale = np.ones((1, n_main), np.float32)
    col_scale[:, 8 * w:9 * w] = HEAD ** -0.5 * LOG2E
    pad = jnp.zeros((w_t.shape[0], LANES - nh, w_t.shape[2]), w_t.dtype)
    w_small = jnp.concatenate([w_t[:, lo0:lo0 + 2 * LORA], w_t[:, f0:f0 + nh], pad], axis=1).astype(BF16)
    return w_t.astype(BF16), segments, jnp.asarray(col_scale), w_small


def _layer(x2, bsz, seq, layer, w_in_all, segments, col_scale, w_small, w_branch_all, w_out_all, norm_g, b_merge,
           conv_w, rwkv_mu, rwkv_w0, rwkv_w2, rwkv_a0, rwkv_a2, rwkv_kk, rwkv_ka, rwkv_rk, rwkv_ln_g, rwkv_ln_b,
           fox_bf, pool_w, pool_scale):
    t, d = x2.shape
    w = conv_w.shape[1]
    nh = fox_bf.shape[0]
    b_f = jnp.concatenate([fox_bf, jnp.zeros((LANES - nh,), fox_bf.dtype)]).reshape(1, LANES)
    row = lambda a: a.reshape(1, -1)

    u2 = _inproj(x2, row(norm_g), w_in_all, layer, segments, col_scale)
    u3 = u2.reshape(bsz, seq, -1)
    lora, kc = _side(x2.reshape(bsz, seq, d), row(norm_g), w_small, b_f, w)
    y_a = _conv_branch(u3, conv_w, w, 0)
    prep = _rwkv_prep(u3, lora, rwkv_mu[:3 * w].reshape(3, w), row(rwkv_mu[3 * w:]), row(rwkv_w0),
                      rwkv_w2.astype(BF16), row(rwkv_a0), rwkv_a2.astype(BF16), row(rwkv_kk), row(rwkv_ka),
                      w, 4 * w)
    y_b = _rwkv_scan(prep, u3, row(rwkv_ln_g), row(rwkv_ln_b), row(rwkv_rk), w, 7 * w)
    y_c = _fox_branch(u3, kc, w, 8 * w, 11 * w)
    y_d = _pool_branch(u3, pool_w.astype(BF16), row(pool_scale), w, 12 * w)
    ys = [y.reshape(t, w) for y in (y_a, y_b, y_c, y_d)]
    merged = _merge(ys, u2, b_merge, w_branch_all, layer, 14 * w)
    return _outproj(merged, w_out_all, layer, x2)


def kernel(x, norm_g, w_in, b_merge, conv_w, rwkv_mu, rwkv_w0, rwkv_w2, rwkv_a0, rwkv_a2, rwkv_kk, rwkv_ka,
           rwkv_rk, rwkv_ln_g, rwkv_ln_b, fox_bf, pool_w, pool_scale, w_branch, w_out, final_g):
    bsz, seq, d = x.shape
    x2 = x.reshape(bsz * seq, d)
    w_in_all, segments, col_scale, w_small_all = _split_in_weights(w_in, conv_w.shape[2], fox_bf.shape[1])
    w_branch_all = w_branch.astype(BF16)
    w_out_all = w_out.astype(BF16)
    for l in range(norm_g.shape[0]):
        x2 = _layer(x2, bsz, seq, l, w_in_all, segments, col_scale, w_small_all[l], w_branch_all, w_out_all,
                    norm_g[l], b_merge[l], conv_w[l], rwkv_mu[l], rwkv_w0[l], rwkv_w2[l], rwkv_a0[l], rwkv_a2[l],
                    rwkv_kk[l], rwkv_ka[l], rwkv_rk[l], rwkv_ln_g[l], rwkv_ln_b[l], fox_bf[l], pool_w[l],
                    pool_scale[l])
    return _final_norm(x2, final_g.reshape(1, d)).reshape(bsz, seq, d)
```

```python
import functools

import jax
import jax.numpy as jnp
import numpy as np
from jax import lax
from jax.experimental import pallas as pl
from jax.experimental.pallas import tpu as pltpu

F32 = jnp.float32
BF16 = jnp.bfloat16
HIGHEST = lax.Precision.HIGHEST

NORM_EPS = 1e-6
RWKV_LN_EPS = 64e-5
HEAD = 64
LORA = 64
POOL_WINDOWS = (2, 4, 8, 16)
RWKV_CHUNK = 64
LANES = 128
MXU_DIM = 256
LOG2E = float(np.log2(np.e))
NEG_BIG = -1e30
VMEM_LIMIT = 48 * 1024 * 1024


def _params(sem):
    return pltpu.CompilerParams(dimension_semantics=sem, vmem_limit_bytes=VMEM_LIMIT)


def _tile(n, want):
    t = min(n, want)
    while n % t:
        t //= 2
    return t


def _sigmoid(x):
    return 1.0 / (1.0 + jnp.exp(-x))


def _silu(x):
    return x * _sigmoid(x)


def _rms_rows(xf, g):
    ms = jnp.mean(xf * xf, axis=-1, keepdims=True)
    return xf * lax.rsqrt(ms + NORM_EPS) * g


def _dot(a, b):
    return jnp.dot(a, b, preferred_element_type=F32)


def _dot_nt(a, b):
    return lax.dot_general(a, b, (((1,), (1,)), ((), ())), preferred_element_type=F32)


def _dot_f32(a, b):
    return jnp.dot(a, b, precision=HIGHEST, preferred_element_type=F32)


def _inproj_kernel(x_ref, g_ref, sc_ref, w_ref, o_ref, h_ref):
    @pl.when(pl.program_id(1) == 0)
    def _():
        h_ref[...] = _rms_rows(x_ref[...], g_ref[...]).astype(BF16)

    o_ref[...] = (_dot_nt(h_ref[...], w_ref[0]) * sc_ref[...]).astype(o_ref.dtype)


def _inproj(x2, g, w_all, layer, segments, col_scale):
    t, d = x2.shape
    n = col_scale.shape[1]
    bounds = [c0 for c0, _ in segments] + [n]
    tn = _tile(functools.reduce(np.gcd, np.diff(bounds)), 1024)
    tm = _tile(t, 1024)

    align = int(functools.reduce(np.gcd, [tn] + [r0 for _, r0 in segments[1:]]))

    def weight_row(j):
        row = j * tn
        for (c0, r0), (c1, r1) in zip(segments[:-1], segments[1:]):
            row = row + jnp.where(j * tn >= c1, (r1 - c1) - (r0 - c0), 0)
        return pl.multiple_of(row + (segments[0][1] - segments[0][0]), align)

    return pl.pallas_call(
        _inproj_kernel,
        grid=(t // tm, n // tn),
        in_specs=[
            pl.BlockSpec((tm, d), lambda i, j: (i, 0)),
            pl.BlockSpec((1, d), lambda i, j: (0, 0)),
            pl.BlockSpec((1, tn), lambda i, j: (0, j)),
            pl.BlockSpec((pl.Element(1), pl.Element(tn), pl.Element(d)),
                         lambda i, j: (layer, weight_row(j), 0)),
        ],
        out_specs=pl.BlockSpec((tm, tn), lambda i, j: (i, j)),
        out_shape=jax.ShapeDtypeStruct((t, n), BF16),
        scratch_shapes=[pltpu.VMEM((tm, d), BF16)],
        compiler_params=_params(("parallel", "arbitrary")),
        name="inproj",
    )(x2, g, col_scale, w_all)


C_PIECES = 3


def _bf16_pieces(x, n=C_PIECES):
    pieces = []
    for _ in range(n):
        p = x.astype(BF16)
        pieces.append(p)
        x = x - p.astype(F32)
    return pieces


def _side_kernel(x_ref, g_ref, ws_ref, bf_ref, sel_ref, lora_ref, kc_ref, carry_ref):
    @pl.when(pl.program_id(1) == 0)
    def _():
        carry_ref[...] = jnp.zeros_like(carry_ref)

    h = _rms_rows(x_ref[...], g_ref[...]).astype(BF16)
    sf = _dot_nt(h, ws_ref[...])
    ns = lora_ref.shape[1]
    lora_ref[...] = sf[:, :ns]
    z = sf[:, ns:] + bf_ref[...]
    logf = jnp.minimum(z, 0.0) - jnp.log1p(jnp.exp(-jnp.abs(z)))
    ts = logf.shape[0]
    lower = (lax.broadcasted_iota(jnp.int32, (ts, ts), 0)
             >= lax.broadcasted_iota(jnp.int32, (ts, ts), 1)).astype(BF16)
    c = functools.reduce(jnp.add, [_dot(lower, p) for p in _bf16_pieces(logf)]) + carry_ref[...]
    carry_ref[...] = c[ts - 1:ts, :]
    kc = None
    for piece, sel in zip(_bf16_pieces(-LOG2E * c), (sel_ref[0], sel_ref[1], sel_ref[2])):
        term = _dot(piece, sel)
        kc = term if kc is None else kc + term
    kc_ref[...] = kc.astype(kc_ref.dtype)


def _forget_lane(head, piece):
    return C_PIECES * head + piece


def _side(x3, g, w_small, b_f, w):
    b, s, d = x3.shape
    ns = w_small.shape[0] - LANES
    ts = _tile(s, 512)
    nh = w // HEAD
    assert C_PIECES * nh <= LANES
    sel = np.zeros((C_PIECES, LANES, LANES), np.float32)
    for hd in range(nh):
        for piece in range(C_PIECES):
            sel[piece, hd, _forget_lane(hd, piece)] = 1.0
    return pl.pallas_call(
        _side_kernel,
        grid=(b, s // ts),
        in_specs=[
            pl.BlockSpec((None, ts, d), lambda i, j: (i, j, 0)),
            pl.BlockSpec((1, d), lambda i, j: (0, 0)),
            pl.BlockSpec((ns + LANES, d), lambda i, j: (0, 0)),
            pl.BlockSpec((1, LANES), lambda i, j: (0, 0)),
            pl.BlockSpec((C_PIECES, LANES, LANES), lambda i, j: (0, 0, 0)),
        ],
        out_specs=[
            pl.BlockSpec((None, ts, ns), lambda i, j: (i, j, 0)),
            pl.BlockSpec((None, ts, LANES), lambda i, j: (i, j, 0)),
        ],
        out_shape=[
            jax.ShapeDtypeStruct((b, s, ns), F32),
            jax.ShapeDtypeStruct((b, s, LANES), BF16),
        ],
        scratch_shapes=[pltpu.VMEM((1, LANES), F32)],
        compiler_params=_params(("parallel", "arbitrary")),
        name="side",
    )(x3, g, w_small, b_f, jnp.asarray(sel, BF16))


HALO = 8


def _conv_kernel(bg_ref, cg_ref, xv_ref, g_ref, cw_ref, o_ref, pbuf):
    ts = o_ref.shape[0]

    @pl.when(pl.program_id(2) == 0)
    def _():
        pbuf[0:HALO, :] = jnp.zeros((HALO, pbuf.shape[1]), F32)

    p = cg_ref[...].astype(F32) * xv_ref[...].astype(F32)
    pbuf[HALO:HALO + ts, :] = p
    cw = cw_ref[...]
    z = (cw[0:1] * pbuf[HALO - 2:HALO - 2 + ts, :] + cw[1:2] * pbuf[HALO - 1:HALO - 1 + ts, :]
         + cw[2:3] * p)
    o_ref[...] = (bg_ref[...].astype(F32) * z * _silu(g_ref[...].astype(F32))).astype(o_ref.dtype)
    pbuf[0:HALO, :] = pbuf[ts:ts + HALO, :]


def _conv_branch(u3, conv_w, w, col0):
    b, s, _ = u3.shape
    ts, tw = _tile(s, 512), _tile(w, 512)
    nw = w // tw
    c0 = col0 // tw

    def spec(k):
        return pl.BlockSpec((None, ts, tw), lambda i, j, t, k=k: (i, t, c0 + k * nw + j))

    return pl.pallas_call(
        _conv_kernel,
        grid=(b, nw, s // ts),
        in_specs=[spec(0), spec(1), spec(2), spec(3),
                  pl.BlockSpec((conv_w.shape[0], tw), lambda i, j, t: (0, j))],
        out_specs=pl.BlockSpec((None, ts, tw), lambda i, j, t: (i, t, j)),
        out_shape=jax.ShapeDtypeStruct((b, s, w), BF16),
        scratch_shapes=[pltpu.VMEM((HALO + ts, tw), F32)],
        compiler_params=_params(("parallel", "parallel", "arbitrary")),
        name="conv_branch",
    )(u3, u3, u3, u3, conv_w)


POOL_HALO = 16


def _pool_kernel(x_ref, g_ref, pw_ref, sc_ref, o_ref, xbuf):
    ts, w = o_ref.shape
    gw = w // len(POOL_WINDOWS)
    s = pl.program_id(1)

    @pl.when(s == 0)
    def _():
        xbuf[0:POOL_HALO, :] = jnp.zeros((POOL_HALO, w), F32)

    x = x_ref[...].astype(F32)
    xbuf[POOL_HALO:POOL_HALO + ts, :] = x
    pos = s * ts + lax.broadcasted_iota(jnp.int32, (ts, 1), 0)
    for gi, win in enumerate(POOL_WINDOWS):
        lo, hi = gi * gw, (gi + 1) * gw
        xi = x[:, lo:hi]
        acc = xi
        for k in range(1, win):
            acc = acc + xbuf[POOL_HALO - k:POOL_HALO - k + ts, lo:hi]
        count = jnp.minimum(pos + 1, win).astype(F32)
        pooled = acc / count - xi
        y = _dot(pooled.astype(BF16), pw_ref[gi])
        o_ref[:, lo:hi] = (y * sc_ref[:, lo:hi] * _silu(g_ref[:, lo:hi].astype(F32))).astype(o_ref.dtype)
    xbuf[0:POOL_HALO, :] = xbuf[ts:ts + POOL_HALO, :]


def _pool_branch(u3, pool_w, pool_scale, w, col0):
    b, s, _ = u3.shape
    ts = _tile(s, 512)
    c0 = col0 // w
    ng, gw, _ = pool_w.shape
    return pl.pallas_call(
        _pool_kernel,
        grid=(b, s // ts),
        in_specs=[
            pl.BlockSpec((None, ts, w), lambda i, t: (i, t, c0)),
            pl.BlockSpec((None, ts, w), lambda i, t: (i, t, c0 + 1)),
            pl.BlockSpec((ng, gw, gw), lambda i, t: (0, 0, 0)),
            pl.BlockSpec((1, w), lambda i, t: (0, 0)),
        ],
        out_specs=pl.BlockSpec((None, ts, w), lambda i, t: (i, t, 0)),
        out_shape=jax.ShapeDtypeStruct((b, s, w), BF16),
        scratch_shapes=[pltpu.VMEM((POOL_HALO + ts, w), F32)],
        compiler_params=_params(("parallel", "arbitrary")),
        name="pool_branch",
    )(u3, u3, pool_w, pool_scale)


def _rwkv_prep_kernel(r_ref, k_ref, v_ref, lo_ref, mu_ref, mul_ref, w0_ref, w2_ref, a0_ref, a2_ref,
                      kkw_ref, kaw_ref, hb_ref,
                      ro_ref, ko_ref, vo_ref, na_ref, bv_ref, ld_ref, buf, lbuf):
    ts, w = ro_ref.shape

    @pl.when(pl.program_id(1) == 0)
    def _():
        buf[0:HALO, :] = jnp.zeros((HALO, buf.shape[1]), F32)
        lbuf[0:HALO, :] = jnp.zeros((HALO, lbuf.shape[1]), F32)

    for idx, ref in enumerate((r_ref, k_ref, v_ref)):
        buf[HALO:HALO + ts, idx * w:(idx + 1) * w] = ref[...].astype(F32)
    lbuf[HALO:HALO + ts, :] = lo_ref[...]

    def mixed(cur, prev, mu):
        return cur + (prev - cur) * mu

    mu = mu_ref[...]
    r = mixed(buf[HALO:HALO + ts, 0:w], buf[HALO - 1:HALO - 1 + ts, 0:w], mu[0:1])
    k = mixed(buf[HALO:HALO + ts, w:2 * w], buf[HALO - 1:HALO - 1 + ts, w:2 * w], mu[1:2])
    v = mixed(buf[HALO:HALO + ts, 2 * w:3 * w], buf[HALO - 1:HALO - 1 + ts, 2 * w:3 * w], mu[2:3])
    lo = mixed(lbuf[HALO:HALO + ts, :], lbuf[HALO - 1:HALO - 1 + ts, :], mul_ref[...])
    wl, al = lo[:, 0:LORA], lo[:, LORA:2 * LORA]

    z = w0_ref[...] + _dot(jnp.tanh(wl).astype(BF16), w2_ref[...])
    ld_ref[...] = -float(np.exp(-0.5)) * _sigmoid(z)
    a = _sigmoid(a0_ref[...] + _dot(al.astype(BF16), a2_ref[...]))

    kk = k * kkw_ref[...]
    hb = hb_ref[...]
    gw = hb.shape[0]
    kk2 = kk * kk
    n2 = jnp.concatenate([_head_sums(kk2[:, l0:l0 + gw], hb) for l0 in range(0, w, gw)], axis=1)
    kk = kk * lax.rsqrt(jnp.maximum(n2, 1e-24))
    ro_ref[...] = r.astype(ro_ref.dtype)
    ko_ref[...] = (k * (1.0 + (a - 1.0) * kaw_ref[...])).astype(ko_ref.dtype)
    vo_ref[...] = v.astype(vo_ref.dtype)
    na_ref[...] = (-kk).astype(na_ref.dtype)
    bv_ref[...] = (kk * a).astype(bv_ref.dtype)

    buf[0:HALO, :] = buf[ts:ts + HALO, :]
    lbuf[0:HALO, :] = lbuf[ts:ts + HALO, :]


def _rwkv_prep(u3, lora, mu3, mul, w0, w2, a0, a2, kkw, kaw, w, col0):
    b, s, _ = u3.shape
    ts = _tile(s, 256)
    c0 = col0 // w
    gw = _tile(w, MXU_DIM)
    head_of = np.arange(gw) // HEAD
    hb = jnp.asarray(head_of[:, None] == head_of[None, :], BF16)
    row = lambda k: pl.BlockSpec((None, ts, w), lambda i, t, k=k: (i, t, c0 + k))
    full = lambda shape: pl.BlockSpec(shape, lambda i, t: (0,) * len(shape))
    outw = pl.BlockSpec((None, ts, w), lambda i, t: (i, t, 0))
    sds = lambda dt: jax.ShapeDtypeStruct((b, s, w), dt)
    return pl.pallas_call(
        _rwkv_prep_kernel,
        grid=(b, s // ts),
        in_specs=[row(0), row(1), row(2),
                  pl.BlockSpec((None, ts, 2 * LORA), lambda i, t: (i, t, 0)),
                  full((3, w)), full((1, 2 * LORA)), full((1, w)), full((LORA, w)),
                  full((1, w)), full((LORA, w)), full((1, w)), full((1, w)),
                  full((gw, gw))],
        out_specs=[outw] * 6,
        out_shape=[sds(BF16)] * 5 + [sds(F32)],
        scratch_shapes=[pltpu.VMEM((HALO + ts, 3 * w), F32), pltpu.VMEM((HALO + ts, 2 * LORA), F32)],
        compiler_params=_params(("parallel", "arbitrary")),
        name="rwkv_prep",
    )(u3, u3, u3, lora, mu3, mul, w0, w2, a0, a2, kkw, kaw, hb)


def _head_stack(x, lane_head):
    zero = jnp.zeros_like(x)
    return jnp.concatenate([jnp.where(lane_head == 0, x, zero), jnp.where(lane_head == 1, x, zero)], axis=0)


SUM_PIECES = 2
RWKV_UNROLL = 8


def _rwkv_scan_kernel(r_ref, k_ref, v_ref, na_ref, bv_ref, ld_ref, g_ref, lng_ref, lnb_ref, rk_ref,
                      cl_ref, cu_ref, hb_ref, o_ref, state, ybuf, dec_s, rh_s, ah_s, bt_s, kt_s, bb_s, kb_s):
    ts, width = o_ref.shape
    n_pairs = width // LANES
    c = RWKV_CHUNK
    n_chunks = ts // c

    @pl.when(pl.program_id(2) == 0)
    def _():
        state[...] = jnp.zeros_like(state)

    sb = cl_ref.shape[0]
    lcs, lrests = [], []
    for r0 in range(0, ts, sb):
        pieces = jnp.concatenate(_bf16_pieces(ld_ref[r0:r0 + sb, :], SUM_PIECES), axis=1)
        for m_ref, outs in ((cl_ref, lcs), (cu_ref, lrests)):
            both = _dot(m_ref[...], pieces)
            outs.append(both[:, 0:width] + both[:, width:2 * width])
    ld = ld_ref[...]
    lc = jnp.concatenate(lcs, axis=0)
    lrest = jnp.concatenate(lrests, axis=0)
    e_in = jnp.exp(lc)
    e_neg = jnp.exp(-lc)
    e_end = jnp.exp(lrest)
    r_all = r_ref[...].astype(F32)
    k_all = k_ref[...].astype(F32)
    na_all = na_ref[...].astype(F32)
    bv_all = bv_ref[...].astype(F32)
    dec_s[...] = e_in
    rh_s[...] = (r_all * e_in).astype(BF16)
    ah_s[...] = (na_all * jnp.exp(lc - ld)).astype(BF16)
    bt_s[...] = (bv_all * e_neg).astype(BF16)
    kt_s[...] = (k_all * e_neg).astype(BF16)
    bb_s[...] = (bv_all * e_end).astype(BF16)
    kb_s[...] = (k_all * e_end).astype(BF16)

    lane_head = lax.broadcasted_iota(jnp.int32, (1, LANES), 1) // HEAD
    t_idx = lax.broadcasted_iota(jnp.int32, (c, LANES), 0)
    s_idx = lax.broadcasted_iota(jnp.int32, (c, LANES), 1) % c
    strict = s_idx < t_idx
    incl = s_idx <= t_idx
    blk = (lax.broadcasted_iota(jnp.int32, (LANES, LANES), 0) // HEAD
           == lax.broadcasted_iota(jnp.int32, (LANES, LANES), 1) // HEAD)
    eye_cat = (s_idx == t_idx).astype(F32)

    def chunk(ci, states):
        rows = pl.ds(pl.multiple_of(ci * c, c), c)
        tail_rows = pl.ds(pl.multiple_of(ci * c + c - 8, 8), 8)
        pairs = range(n_pairs)
        lanes = [slice(pi * LANES, (pi + 1) * LANES) for pi in pairs]
        each = lambda f: [f(pi) for pi in pairs]
        v = each(lambda pi: v_ref[rows, lanes[pi]])
        rh = each(lambda pi: rh_s[rows, lanes[pi]])
        ah = each(lambda pi: ah_s[rows, lanes[pi]])
        sc = each(lambda pi: _dot_nt(
            jnp.concatenate([ah[pi], rh[pi]], axis=0),
            jnp.concatenate([_head_stack(bt_s[rows, lanes[pi]], lane_head),
                             _head_stack(kt_s[rows, lanes[pi]], lane_head)], axis=0)))
        a_ak = each(lambda pi: jnp.where(strict, sc[pi][0:c, LANES:2 * LANES], 0.0).astype(BF16))
        a_rb = each(lambda pi: jnp.where(incl, sc[pi][c:2 * c, 0:LANES], 0.0).astype(BF16))
        a_rk = each(lambda pi: jnp.where(incl, sc[pi][c:2 * c, LANES:2 * LANES], 0.0).astype(BF16))

        x = each(lambda pi: jnp.where(strict, sc[pi][0:c, 0:LANES], 0.0))
        tinv = each(lambda pi: eye_cat + x[pi])
        p2 = 2
        while p2 < c:
            xb = each(lambda pi: x[pi].astype(BF16))
            x = each(lambda pi: _dot(xb[pi], _head_stack(xb[pi], lane_head)))
            tinv = each(lambda pi: tinv[pi] + _dot(tinv[pi].astype(BF16),
                                                   _head_stack(x[pi].astype(BF16), lane_head)))
            p2 *= 2
        t_cat = each(lambda pi: tinv[pi].astype(BF16))

        g2b = each(lambda pi: states[pi].astype(BF16))
        vs = each(lambda pi: _head_stack(v[pi], lane_head))
        p = each(lambda pi: _dot_nt(ah[pi], g2b[pi]) + _dot(a_ak[pi], vs[pi]))
        ub = each(lambda pi: _dot(t_cat[pi], _head_stack(p[pi].astype(BF16), lane_head)).astype(BF16))
        new_states = []
        for pi in pairs:
            y = (_dot_nt(rh[pi], g2b[pi]) + _dot(a_rb[pi], _head_stack(ub[pi], lane_head))
                 + _dot(a_rk[pi], vs[pi]))
            ybuf[rows, lanes[pi]] = y
            uv_t = jnp.concatenate([ub[pi], v[pi]], axis=0).T
            upd = _dot(uv_t, jnp.concatenate([bb_s[rows, lanes[pi]], kb_s[rows, lanes[pi]]], axis=0))
            decay = dec_s[tail_rows, lanes[pi]][7:8, :]
            new_states.append(states[pi] * decay + jnp.where(blk, upd, 0.0))
        return tuple(new_states)

    final = lax.fori_loop(0, n_chunks, chunk, tuple(state[pi] for pi in range(n_pairs)),
                          unroll=min(n_chunks, RWKV_UNROLL))
    for pi in range(n_pairs):
        state[pi] = final[pi]

    hb = hb_ref[...]
    gw = hb.shape[0]
    for l0 in range(0, width, gw):
        lanes = slice(l0, l0 + gw)
        y = ybuf[:, lanes]
        mean = _head_sums(y, hb) * (1.0 / HEAD)
        d = y - mean
        var = _head_sums(d * d, hb) * (1.0 / HEAD)
        yn = d * lax.rsqrt(var + RWKV_LN_EPS) * lng_ref[:, lanes] + lnb_ref[:, lanes]
        rk = r_ref[:, lanes].astype(F32) * k_ref[:, lanes].astype(F32) * rk_ref[:, lanes]
        bonus = _head_sums(rk, hb) * v_ref[:, lanes].astype(F32)
        o_ref[:, lanes] = ((yn + bonus) * _silu(g_ref[:, lanes].astype(F32))).astype(o_ref.dtype)


def _head_sums(x, m):
    return functools.reduce(jnp.add, [_dot(p, m) for p in _bf16_pieces(x, SUM_PIECES)])


RWKV_PAIRS = 8


def _rwkv_scan(prep, u3, ln_g, ln_b, r_k, w, gate_col0):
    r, k, v, na, bv, ld = prep
    b, s, _ = r.shape
    ts = _tile(s, 512)
    width = LANES * _tile(w // LANES, RWKV_PAIRS)
    gc0 = gate_col0 // width
    sb = _tile(ts, MXU_DIM)
    chunk_of = np.arange(sb) // RWKV_CHUNK
    same = chunk_of[:, None] == chunk_of[None, :]
    tri = np.arange(sb)[:, None] >= np.arange(sb)[None, :]
    cl = jnp.asarray(same & tri, BF16)
    cu = jnp.asarray(same & ~tri, BF16)
    gw = _tile(width, MXU_DIM)
    head_of = np.arange(gw) // HEAD
    hb = jnp.asarray(head_of[:, None] == head_of[None, :], BF16)
    row = pl.BlockSpec((None, ts, width), lambda i, p, t: (i, t, p))
    vec = pl.BlockSpec((1, width), lambda i, p, t: (0, p))
    const = lambda n: pl.BlockSpec((n, n), lambda i, p, t: (0, 0))
    tile_bf16 = pltpu.VMEM((ts, width), BF16)
    return pl.pallas_call(
        _rwkv_scan_kernel,
        grid=(b, w // width, s // ts),
        in_specs=[row] * 6 + [pl.BlockSpec((None, ts, width), lambda i, p, t: (i, t, gc0 + p)),
                              vec, vec, vec, const(sb), const(sb), const(gw)],
        out_specs=row,
        out_shape=jax.ShapeDtypeStruct((b, s, w), BF16),
        scratch_shapes=[pltpu.VMEM((width // LANES, LANES, LANES), F32), pltpu.VMEM((ts, width), F32),
                        pltpu.VMEM((ts, width), F32)] + [tile_bf16] * 6,
        compiler_params=_params(("parallel", "parallel", "arbitrary")),
        name="rwkv_scan",
    )(r, k, v, na, bv, ld, u3, ln_g, ln_b, r_k, cl, cu, hb)


FOX_BLOCK = 1024
FOX_SUB = 16
FOX_ACC_ROWS = HEAD + 16


def _fold8(x, op):
    parts = [x[i:i + 8, :] for i in range(0, x.shape[0], 8)]
    return functools.reduce(op, parts)


def _fox_kernel(qi_ref, kj_ref, q_ref, k_ref, v_ref, kc_ref, g_ref, o_ref,
                m_ref, acc_ref, qx_ref, s0_ref, s1_ref, p0_ref, p1_ref):
    pair, step = pl.program_id(1), pl.program_id(2)
    qi, kj = qi_ref[step], kj_ref[step]
    tq, tk = q_ref.shape[0], k_ref.shape[0]
    lane = lax.broadcasted_iota(jnp.int32, (1, LANES), 1)

    @pl.when(kj == 0)
    def _():
        m_ref[...] = jnp.full_like(m_ref, NEG_BIG)
        acc_ref[...] = jnp.zeros_like(acc_ref)
        q2 = q_ref[...]
        for h in range(2):
            first = _forget_lane(2 * pair + h, 0)
            ones = jnp.where((lane >= first) & (lane < first + C_PIECES), 1.0, 0.0).astype(BF16)
            qx_ref[h, :, 0:LANES] = jnp.where(lane // HEAD == h, q2, jnp.zeros_like(q2))
            qx_ref[h, :, LANES:2 * LANES] = jnp.broadcast_to(ones, q2.shape)

    s_refs, p_refs = (s0_ref, s1_ref), (p0_ref, p1_ref)

    def step_body(masked):
        kx = jnp.concatenate([k_ref[...], kc_ref[...]], axis=1)
        half = tk // 2
        block_max = [None, None]
        for h in range(2):
            if masked:
                s_refs[h][0:half, :] = _dot_nt(kx[0:half, :], qx_ref[h])
                s_refs[h][half:, half:] = _dot_nt(kx[half:, :], qx_ref[h, half:, :])
            else:
                s_val = _dot_nt(kx, qx_ref[h])
                s_refs[h][...] = s_val
                block_max[h] = _fold8(s_val, jnp.maximum)
        n_sub, n_grp = tk // FOX_SUB, tq // LANES

        def first_group(kb):
            return (kb * FOX_SUB) // LANES if masked else 0

        def group_scores(h, kb, g):
            blk = s_refs[h][kb * FOX_SUB:(kb + 1) * FOX_SUB, g * LANES:(g + 1) * LANES]
            if masked and g == first_group(kb):
                key = kb * FOX_SUB + lax.broadcasted_iota(jnp.int32, (FOX_SUB, LANES), 0)
                query = g * LANES + lax.broadcasted_iota(jnp.int32, (FOX_SUB, LANES), 1)
                blk = jnp.where(key <= query, blk, NEG_BIG)
            return blk

        def scores(h, kb, g0):
            return jnp.concatenate([group_scores(h, kb, g) for g in range(g0, n_grp)], axis=1)

        def running_max(h):
            if masked:
                mx = [None] * n_grp
                for kb in range(n_sub):
                    g0 = first_group(kb)
                    part = _fold8(scores(h, kb, g0), jnp.maximum)
                    for g in range(g0, n_grp):
                        piece = part[:, (g - g0) * LANES:(g - g0 + 1) * LANES]
                        mx[g] = piece if mx[g] is None else jnp.maximum(mx[g], piece)
                mx8 = jnp.concatenate(mx, axis=1)
            else:
                mx8 = block_max[h]
            m_prev = m_ref[h]
            m_new = jnp.maximum(m_prev, jnp.max(mx8, axis=0, keepdims=True))
            m_ref[h] = m_new
            return m_new, jnp.exp2(m_prev - m_new)

        def probabilities(h, m_new):
            for kb in range(n_sub):
                g0 = first_group(kb)
                rows = slice(kb * FOX_SUB, (kb + 1) * FOX_SUB)
                p_refs[h][rows, g0 * LANES:] = jnp.exp2(scores(h, kb, g0) - m_new[:, g0 * LANES:]).astype(BF16)
                z0 = half if kb * FOX_SUB >= half else 0
                if g0 * LANES > z0:
                    p_refs[h][rows, z0:g0 * LANES] = jnp.zeros((FOX_SUB, g0 * LANES - z0), BF16)

        def accumulate(h, alpha):
            lhs = jnp.concatenate([v_t[h * HEAD:(h + 1) * HEAD, :], sum_rows], axis=0)
            if masked:
                acc_ref[h] = alpha * acc_ref[h] + _dot(lhs[:, 0:half], p_refs[h][0:half, :])
                acc_ref[h, :, half:] += _dot(lhs[:, half:], p_refs[h][half:, half:])
            else:
                acc_ref[h] = alpha * acc_ref[h] + _dot(lhs, p_refs[h][...])

        eye = (lax.broadcasted_iota(jnp.int32, (LANES, LANES), 0)
               == lax.broadcasted_iota(jnp.int32, (LANES, LANES), 1)).astype(BF16)
        v_t = _dot_nt(eye, v_ref[...]).astype(BF16)
        sum_rows = jnp.ones((FOX_ACC_ROWS - HEAD, tk), BF16)
        m0, alpha0 = running_max(0)
        probabilities(0, m0)
        m1, alpha1 = running_max(1)
        accumulate(0, alpha0)
        probabilities(1, m1)
        accumulate(1, alpha1)

    @pl.when(kj < qi)
    def _():
        step_body(False)

    @pl.when(kj == qi)
    def _():
        step_body(True)
        outs = [acc_ref[h, 0:HEAD, :] * (1.0 / acc_ref[h, HEAD:HEAD + 1, :]) for h in range(2)]
        out = jnp.concatenate(outs, axis=0).T
        o_ref[...] = (out * _silu(g_ref[...].astype(F32))).astype(o_ref.dtype)


def _fox_branch(u3, kc, w, q_col0, gate_col0):
    b, s, _ = u3.shape
    tq = tk = _tile(s, FOX_BLOCK)
    n_pairs = w // LANES
    pairs = [(i, j) for i in range(s // tq) for j in range(i + 1)]
    qi = jnp.asarray([p[0] for p in pairs], jnp.int32)
    kj = jnp.asarray([p[1] for p in pairs], jnp.int32)
    qc0, gc0 = q_col0 // LANES, gate_col0 // LANES
    grid_spec = pltpu.PrefetchScalarGridSpec(
        num_scalar_prefetch=2,
        grid=(b, n_pairs, len(pairs)),
        in_specs=[
            pl.BlockSpec((None, tq, LANES), lambda i, p, t, qi, kj: (i, qi[t], qc0 + p)),
            pl.BlockSpec((None, tk, LANES), lambda i, p, t, qi, kj: (i, kj[t], qc0 + n_pairs + p)),
            pl.BlockSpec((None, tk, LANES), lambda i, p, t, qi, kj: (i, kj[t], qc0 + 2 * n_pairs + p)),
            pl.BlockSpec((None, tk, LANES), lambda i, p, t, qi, kj: (i, kj[t], 0)),
            pl.BlockSpec((None, tq, LANES), lambda i, p, t, qi, kj: (i, qi[t], gc0 + p)),
        ],
        out_specs=pl.BlockSpec((None, tq, LANES), lambda i, p, t, qi, kj: (i, qi[t], p)),
        scratch_shapes=[pltpu.VMEM((2, 1, tq), F32),
                        pltpu.VMEM((2, FOX_ACC_ROWS, tq), F32), pltpu.VMEM((2, tq, 2 * LANES), BF16),
                        pltpu.VMEM((tk, tq), F32), pltpu.VMEM((tk, tq), F32),
                        pltpu.VMEM((tk, tq), BF16), pltpu.VMEM((tk, tq), BF16)],
    )
    return pl.pallas_call(
        _fox_kernel,
        grid_spec=grid_spec,
        out_shape=jax.ShapeDtypeStruct((b, s, w), BF16),
        compiler_params=_params(("parallel", "parallel", "arbitrary")),
        name="fox_attention",
    )(qi, kj, u3, u3, u3, kc, u3)


def _merge_kernel(ya_ref, yb_ref, yc_ref, yd_ref, ma_ref, mb_ref, mc_ref, md_ref, bm_ref, wb_ref, o_ref):
    acc = None
    branches = ((ya_ref, ma_ref), (yb_ref, mb_ref), (yc_ref, mc_ref), (yd_ref, md_ref))
    for kbr, (y_ref, ml_ref) in enumerate(branches):
        proj = _dot(y_ref[...], wb_ref[kbr])
        gate = _sigmoid(ml_ref[...].astype(F32) + bm_ref[kbr:kbr + 1, :])
        acc = gate * proj if acc is None else acc + gate * proj
    o_ref[...] = acc.astype(o_ref.dtype)


def _merge(ys, u2, b_merge, w_branch_all, layer, ml_col0):
    t, w = ys[0].shape
    _, nb, _, d = w_branch_all.shape
    tm, tn = _tile(t, 1024), _tile(d, 512)
    nd = d // tn
    mc0 = ml_col0 // tn
    yspec = pl.BlockSpec((tm, w), lambda i, j: (i, 0))
    mspec = lambda k: pl.BlockSpec((tm, tn), lambda i, j, k=k: (i, mc0 + k * nd + j))
    return pl.pallas_call(
        _merge_kernel,
        grid=(t // tm, nd),
        in_specs=[yspec] * 4 + [mspec(k) for k in range(4)] + [
            pl.BlockSpec((nb, tn), lambda i, j: (0, j)),
            pl.BlockSpec((None, nb, w, tn), lambda i, j: (layer, 0, 0, j)),
        ],
        out_specs=pl.BlockSpec((tm, tn), lambda i, j: (i, j)),
        out_shape=jax.ShapeDtypeStruct((t, d), BF16),
        compiler_params=_params(("parallel", "arbitrary")),
        name="merge",
    )(*ys, u2, u2, u2, u2, b_merge, w_branch_all)


def _outproj_kernel(m_ref, w_ref, x_ref, o_ref):
    o_ref[...] = x_ref[...] + _dot(m_ref[...], w_ref[...])


def _outproj(merged, w_out_all, layer, x2):
    t, d = x2.shape
    tm, tn = _tile(t, 1024), _tile(d, 1024)
    return pl.pallas_call(
        _outproj_kernel,
        grid=(t // tm, d // tn),
        in_specs=[
            pl.BlockSpec((tm, d), lambda i, j: (i, 0)),
            pl.BlockSpec((None, d, tn), lambda i, j: (layer, 0, j)),
            pl.BlockSpec((tm, tn), lambda i, j: (i, j)),
        ],
        out_specs=pl.BlockSpec((tm, tn), lambda i, j: (i, j)),
        out_shape=jax.ShapeDtypeStruct((t, d), F32),
        compiler_params=_params(("parallel", "arbitrary")),
        name="outproj",
    )(merged, w_out_all, x2)


def _final_norm_kernel(x_ref, g_ref, o_ref):
    o_ref[...] = _rms_rows(x_ref[...], g_ref[...])


def _final_norm(x2, g):
    t, d = x2.shape
    tm = _tile(t, 512)
    return pl.pallas_call(
        _final_norm_kernel,
        grid=(t // tm,),
        in_specs=[pl.BlockSpec((tm, d), lambda i: (i, 0)), pl.BlockSpec((1, d), lambda i: (0, 0))],
        out_specs=pl.BlockSpec((tm, d), lambda i: (i, 0)),
        out_shape=jax.ShapeDtypeStruct((t, d), F32),
        compiler_params=_params(("parallel",)),
        name="final_norm",
    )(x2, g)


def _split_in_weights(w_in, w, nh):
    w_t = jnp.swapaxes(w_in, 1, 2)
    lo0, f0 = 7 * w, 11 * w + 2 * LORA
    n_main = w_in.shape[2] - 2 * LORA - nh
    segments = ((0, 0), (lo0, lo0 + 2 * LORA), (f0 - 2 * LORA, f0 + nh))
    col_scale = np.ones((1, n_main), np.float32)
    col_scale[:, 8 * w:9 * w] = HEAD ** -0.5 * LOG2E
    pad = jnp.zeros((w_t.shape[0], LANES - nh, w_t.shape[2]), w_t.dtype)
    w_small = jnp.concatenate([w_t[:, lo0:lo0 + 2 * LORA], w_t[:, f0:f0 + nh], pad], axis=1).astype(BF16)
    return w_t.astype(BF16), segments, jnp.asarray(col_scale), w_small


def _layer(x2, bsz, seq, layer, w_in_all, segments, col_scale, w_small, w_branch_all, w_out_all, norm_g, b_merge,
           conv_w, rwkv_mu, rwkv_w0, rwkv_w2, rwkv_a0, rwkv_a2, rwkv_kk, rwkv_ka, rwkv_rk, rwkv_ln_g, rwkv_ln_b,
           fox_bf, pool_w, pool_scale):
    t, d = x2.shape
    w = conv_w.shape[1]
    nh = fox_bf.shape[0]
    b_f = jnp.concatenate([fox_bf, jnp.zeros((LANES - nh,), fox_bf.dtype)]).reshape(1, LANES)
    row = lambda a: a.reshape(1, -1)

    u2 = _inproj(x2, row(norm_g), w_in_all, layer, segments, col_scale)
    u3 = u2.reshape(bsz, seq, -1)
    lora, kc = _side(x2.reshape(bsz, seq, d), row(norm_g), w_small, b_f, w)
    y_a = _conv_branch(u3, conv_w, w, 0)
    prep = _rwkv_prep(u3, lora, rwkv_mu[:3 * w].reshape(3, w), row(rwkv_mu[3 * w:]), row(rwkv_w0),
                      rwkv_w2.astype(BF16), row(rwkv_a0), rwkv_a2.astype(BF16), row(rwkv_kk), row(rwkv_ka),
                      w, 4 * w)
    y_b = _rwkv_scan(prep, u3, row(rwkv_ln_g), row(rwkv_ln_b), row(rwkv_rk), w, 7 * w)
    y_c = _fox_branch(u3, kc, w, 8 * w, 11 * w)
    y_d = _pool_branch(u3, pool_w.astype(BF16), row(pool_scale), w, 12 * w)
    ys = [y.reshape(t, w) for y in (y_a, y_b, y_c, y_d)]
    merged = _merge(ys, u2, b_merge, w_branch_all, layer, 14 * w)
    return _outproj(merged, w_out_all, layer, x2)


def kernel(x, norm_g, w_in, b_merge, conv_w, rwkv_mu, rwkv_w0, rwkv_w2, rwkv_a0, rwkv_a2, rwkv_kk, rwkv_ka,
           rwkv_rk, rwkv_ln_g, rwkv_ln_b, fox_bf, pool_w, pool_scale, w_branch, w_out, final_g):
    bsz, seq, d = x.shape
    x2 = x.reshape(bsz * seq, d)
    w_in_all, segments, col_scale, w_small_all = _split_in_weights(w_in, conv_w.shape[2], fox_bf.shape[1])
    w_branch_all = w_branch.astype(BF16)
    w_out_all = w_out.astype(BF16)
    for l in range(norm_g.shape[0]):
        x2 = _layer(x2, bsz, seq, l, w_in_all, segments, col_scale, w_small_all[l], w_branch_all, w_out_all,
                    norm_g[l], b_merge[l], conv_w[l], rwkv_mu[l], rwkv_w0[l], rwkv_w2[l], rwkv_a0[l], rwkv_a2[l],
                    rwkv_kk[l], rwkv_ka[l], rwkv_rk[l], rwkv_ln_g[l], rwkv_ln_b[l], fox_bf[l], pool_w[l],
                    pool_scale[l])
    return _final_norm(x2, final_g.reshape(1, d)).reshape(bsz, seq, d)
```

```python
import functools

import jax
import jax.numpy as jnp
import numpy as np
from jax import lax
from jax.experimental import pallas as pl
from jax.experimental.pallas import tpu as pltpu

F32 = jnp.float32
BF16 = jnp.bfloat16
HIGHEST = lax.Precision.HIGHEST

NORM_EPS = 1e-6
RWKV_LN_EPS = 64e-5
HEAD = 64
LORA = 64
POOL_WINDOWS = (2, 4, 8, 16)
RWKV_CHUNK = 64
LANES = 128
MXU_DIM = 256
LOG2E = float(np.log2(np.e))
NEG_BIG = -1e30
VMEM_LIMIT = 48 * 1024 * 1024


def _params(sem):
    return pltpu.CompilerParams(dimension_semantics=sem, vmem_limit_bytes=VMEM_LIMIT)


def _tile(n, want):
    t = min(n, want)
    while n % t:
        t //= 2
    return t


def _sigmoid(x):
    return 1.0 / (1.0 + jnp.exp(-x))


def _silu(x):
    return x * _sigmoid(x)


def _rms_rows(xf, g):
    ms = jnp.mean(xf * xf, axis=-1, keepdims=True)
    return xf * lax.rsqrt(ms + NORM_EPS) * g


def _dot(a, b):
    return jnp.dot(a, b, preferred_element_type=F32)


def _dot_nt(a, b):
    return lax.dot_general(a, b, (((1,), (1,)), ((), ())), preferred_element_type=F32)


def _dot_f32(a, b):
    return jnp.dot(a, b, precision=HIGHEST, preferred_element_type=F32)


def _inproj_kernel(x_ref, g_ref, sc_ref, w_ref, o_ref, h_ref):
    @pl.when(pl.program_id(1) == 0)
    def _():
        h_ref[...] = _rms_rows(x_ref[...], g_ref[...]).astype(BF16)

    o_ref[...] = (_dot_nt(h_ref[...], w_ref[0]) * sc_ref[...]).astype(o_ref.dtype)


def _inproj(x2, g, w_all, layer, segments, col_scale):
    t, d = x2.shape
    n = col_scale.shape[1]
    bounds = [c0 for c0, _ in segments] + [n]
    tn = _tile(functools.reduce(np.gcd, np.diff(bounds)), 1024)
    tm = _tile(t, 1024)

    align = int(functools.reduce(np.gcd, [tn] + [r0 for _, r0 in segments[1:]]))

    def weight_row(j):
        row = j * tn
        for (c0, r0), (c1, r1) in zip(segments[:-1], segments[1:]):
            row = row + jnp.where(j * tn >= c1, (r1 - c1) - (r0 - c0), 0)
        return pl.multiple_of(row + (segments[0][1] - segments[0][0]), align)

    return pl.pallas_call(
        _inproj_kernel,
        grid=(t // tm, n // tn),
        in_specs=[
            pl.BlockSpec((tm, d), lambda i, j: (i, 0)),
            pl.BlockSpec((1, d), lambda i, j: (0, 0)),
            pl.BlockSpec((1, tn), lambda i, j: (0, j)),
            pl.BlockSpec((pl.Element(1), pl.Element(tn), pl.Element(d)),
                         lambda i, j: (layer, weight_row(j), 0)),
        ],
        out_specs=pl.BlockSpec((tm, tn), lambda i, j: (i, j)),
        out_shape=jax.ShapeDtypeStruct((t, n), BF16),
        scratch_shapes=[pltpu.VMEM((tm, d), BF16)],
        compiler_params=_params(("parallel", "arbitrary")),
        name="inproj",
    )(x2, g, col_scale, w_all)


C_PIECES = 3


def _bf16_pieces(x, n=C_PIECES):
    pieces = []
    for _ in range(n):
        p = x.astype(BF16)
        pieces.append(p)
        x = x - p.astype(F32)
    return pieces


def _side_kernel(x_ref, g_ref, ws_ref, bf_ref, sel_ref, lora_ref, kc_ref, carry_ref):
    @pl.when(pl.program_id(1) == 0)
    def _():
        carry_ref[...] = jnp.zeros_like(carry_ref)

    h = _rms_rows(x_ref[...], g_ref[...]).astype(BF16)
    sf = _dot_nt(h, ws_ref[...])
    ns = lora_ref.shape[1]
    lora_ref[...] = sf[:, :ns]
    z = sf[:, ns:] + bf_ref[...]
    logf = jnp.minimum(z, 0.0) - jnp.log1p(jnp.exp(-jnp.abs(z)))
    ts = logf.shape[0]
    lower = (lax.broadcasted_iota(jnp.int32, (ts, ts), 0)
             >= lax.broadcasted_iota(jnp.int32, (ts, ts), 1)).astype(BF16)
    c = functools.reduce(jnp.add, [_dot(lower, p) for p in _bf16_pieces(logf)]) + carry_ref[...]
    carry_ref[...] = c[ts - 1:ts, :]
    kc = None
    for piece, sel in zip(_bf16_pieces(-LOG2E * c), (sel_ref[0], sel_ref[1], sel_ref[2])):
        term = _dot(piece, sel)
        kc = term if kc is None else kc + term
    kc_ref[...] = kc.astype(kc_ref.dtype)


def _forget_lane(head, piece):
    return C_PIECES * head + piece


def _side(x3, g, w_small, b_f, w):
    b, s, d = x3.shape
    ns = w_small.shape[0] - LANES
    ts = _tile(s, 512)
    nh = w // HEAD
    assert C_PIECES * nh <= LANES
    sel = np.zeros((C_PIECES, LANES, LANES), np.float32)
    for hd in range(nh):
        for piece in range(C_PIECES):
            sel[piece, hd, _forget_lane(hd, piece)] = 1.0
    return pl.pallas_call(
        _side_kernel,
        grid=(b, s // ts),
        in_specs=[
            pl.BlockSpec((None, ts, d), lambda i, j: (i, j, 0)),
            pl.BlockSpec((1, d), lambda i, j: (0, 0)),
            pl.BlockSpec((ns + LANES, d), lambda i, j: (0, 0)),
            pl.BlockSpec((1, LANES), lambda i, j: (0, 0)),
            pl.BlockSpec((C_PIECES, LANES, LANES), lambda i, j: (0, 0, 0)),
        ],
        out_specs=[
            pl.BlockSpec((None, ts, ns), lambda i, j: (i, j, 0)),
            pl.BlockSpec((None, ts, LANES), lambda i, j: (i, j, 0)),
        ],
        out_shape=[
            jax.ShapeDtypeStruct((b, s, ns), F32),
            jax.ShapeDtypeStruct((b, s, LANES), BF16),
        ],
        scratch_shapes=[pltpu.VMEM((1, LANES), F32)],
        compiler_params=_params(("parallel", "arbitrary")),
        name="side",
    )(x3, g, w_small, b_f, jnp.asarray(sel, BF16))


HALO = 8


def _conv_kernel(bg_ref, cg_ref, xv_ref, g_ref, cw_ref, o_ref, pbuf, seq_axis=2):
    ts = o_ref.shape[0]

    @pl.when(pl.program_id(seq_axis) == 0)
    def _():
        pbuf[0:HALO, :] = jnp.zeros((HALO, pbuf.shape[1]), F32)

    p = cg_ref[...].astype(F32) * xv_ref[...].astype(F32)
    pbuf[HALO:HALO + ts, :] = p
    cw = cw_ref[...]
    z = (cw[0:1] * pbuf[HALO - 2:HALO - 2 + ts, :] + cw[1:2] * pbuf[HALO - 1:HALO - 1 + ts, :]
         + cw[2:3] * p)
    o_ref[...] = (bg_ref[...].astype(F32) * z * _silu(g_ref[...].astype(F32))).astype(o_ref.dtype)
    pbuf[0:HALO, :] = pbuf[ts:ts + HALO, :]


def _conv_branch(u3, conv_w, w, col0):
    b, s, _ = u3.shape
    ts, tw = _tile(s, 512), _tile(w, 512)
    nw = w // tw
    c0 = col0 // tw

    def spec(k):
        return pl.BlockSpec((None, ts, tw), lambda i, j, t, k=k: (i, t, c0 + k * nw + j))

    return pl.pallas_call(
        _conv_kernel,
        grid=(b, nw, s // ts),
        in_specs=[spec(0), spec(1), spec(2), spec(3),
                  pl.BlockSpec((conv_w.shape[0], tw), lambda i, j, t: (0, j))],
        out_specs=pl.BlockSpec((None, ts, tw), lambda i, j, t: (i, t, j)),
        out_shape=jax.ShapeDtypeStruct((b, s, w), BF16),
        scratch_shapes=[pltpu.VMEM((HALO + ts, tw), F32)],
        compiler_params=_params(("parallel", "parallel", "arbitrary")),
        name="conv_branch",
    )(u3, u3, u3, u3, conv_w)


POOL_HALO = 16


def _pool_kernel(x_ref, g_ref, pw_ref, sc_ref, o_ref, xbuf):
    ts, w = o_ref.shape
    gw = w // len(POOL_WINDOWS)
    s = pl.program_id(1)

    @pl.when(s == 0)
    def _():
        xbuf[0:POOL_HALO, :] = jnp.zeros((POOL_HALO, w), F32)

    x = x_ref[...].astype(F32)
    xbuf[POOL_HALO:POOL_HALO + ts, :] = x
    pos = s * ts + lax.broadcasted_iota(jnp.int32, (ts, 1), 0)
    for gi, win in enumerate(POOL_WINDOWS):
        lo, hi = gi * gw, (gi + 1) * gw
        xi = x[:, lo:hi]
        acc = xi
        for k in range(1, win):
            acc = acc + xbuf[POOL_HALO - k:POOL_HALO - k + ts, lo:hi]
        count = jnp.minimum(pos + 1, win).astype(F32)
        pooled = acc / count - xi
        y = _dot(pooled.astype(BF16), pw_ref[gi])
        o_ref[:, lo:hi] = (y * sc_ref[:, lo:hi] * _silu(g_ref[:, lo:hi].astype(F32))).astype(o_ref.dtype)
    xbuf[0:POOL_HALO, :] = xbuf[ts:ts + POOL_HALO, :]


def _pool_branch(u3, pool_w, pool_scale, w, col0):
    b, s, _ = u3.shape
    ts = _tile(s, 512)
    c0 = col0 // w
    ng, gw, _ = pool_w.shape
    return pl.pallas_call(
        _pool_kernel,
        grid=(b, s // ts),
        in_specs=[
            pl.BlockSpec((None, ts, w), lambda i, t: (i, t, c0)),
            pl.BlockSpec((None, ts, w), lambda i, t: (i, t, c0 + 1)),
            pl.BlockSpec((ng, gw, gw), lambda i, t: (0, 0, 0)),
            pl.BlockSpec((1, w), lambda i, t: (0, 0)),
        ],
        out_specs=pl.BlockSpec((None, ts, w), lambda i, t: (i, t, 0)),
        out_shape=jax.ShapeDtypeStruct((b, s, w), BF16),
        scratch_shapes=[pltpu.VMEM((POOL_HALO + ts, w), F32)],
        compiler_params=_params(("parallel", "arbitrary")),
        name="pool_branch",
    )(u3, u3, pool_w, pool_scale)


def _conv_pool_kernel(bg_ref, cg_ref, xv_ref, g_ref, cw_ref, x_ref, dg_ref, pw_ref, sc_ref,
                      oa_ref, od_ref, pbuf, xbuf):
    _conv_kernel(bg_ref, cg_ref, xv_ref, g_ref, cw_ref, oa_ref, pbuf, seq_axis=1)
    _pool_kernel(x_ref, dg_ref, pw_ref, sc_ref, od_ref, xbuf)


def _conv_pool_branches(u3, conv_w, pool_w, pool_scale, w, conv_col0, pool_col0):
    b, s, _ = u3.shape
    ts = _tile(s, 512)
    ca, cd = conv_col0 // w, pool_col0 // w
    ng, gw, _ = pool_w.shape
    col = lambda c: pl.BlockSpec((None, ts, w), lambda i, t, c=c: (i, t, c))
    out = pl.BlockSpec((None, ts, w), lambda i, t: (i, t, 0))
    return pl.pallas_call(
        _conv_pool_kernel,
        grid=(b, s // ts),
        in_specs=[col(ca), col(ca + 1), col(ca + 2), col(ca + 3),
                  pl.BlockSpec((conv_w.shape[0], w), lambda i, t: (0, 0)),
                  col(cd), col(cd + 1),
                  pl.BlockSpec((ng, gw, gw), lambda i, t: (0, 0, 0)),
                  pl.BlockSpec((1, w), lambda i, t: (0, 0))],
        out_specs=[out, out],
        out_shape=[jax.ShapeDtypeStruct((b, s, w), BF16)] * 2,
        scratch_shapes=[pltpu.VMEM((HALO + ts, w), F32), pltpu.VMEM((POOL_HALO + ts, w), F32)],
        compiler_params=_params(("parallel", "arbitrary")),
        name="conv_pool_branches",
    )(u3, u3, u3, u3, conv_w, u3, u3, pool_w, pool_scale)


def _rwkv_prep_kernel(r_ref, k_ref, v_ref, lo_ref, mu_ref, mul_ref, w0_ref, w2_ref, a0_ref, a2_ref,
                      kkw_ref, kaw_ref, hb_ref,
                      ro_ref, ko_ref, vo_ref, na_ref, bv_ref, ld_ref, buf, lbuf):
    ts, w = ro_ref.shape

    @pl.when(pl.program_id(1) == 0)
    def _():
        buf[0:HALO, :] = jnp.zeros((HALO, buf.shape[1]), F32)
        lbuf[0:HALO, :] = jnp.zeros((HALO, lbuf.shape[1]), F32)

    for idx, ref in enumerate((r_ref, k_ref, v_ref)):
        buf[HALO:HALO + ts, idx * w:(idx + 1) * w] = ref[...].astype(F32)
    lbuf[HALO:HALO + ts, :] = lo_ref[...]

    def mixed(cur, prev, mu):
        return cur + (prev - cur) * mu

    mu = mu_ref[...]
    r = mixed(buf[HALO:HALO + ts, 0:w], buf[HALO - 1:HALO - 1 + ts, 0:w], mu[0:1])
    k = mixed(buf[HALO:HALO + ts, w:2 * w], buf[HALO - 1:HALO - 1 + ts, w:2 * w], mu[1:2])
    v = mixed(buf[HALO:HALO + ts, 2 * w:3 * w], buf[HALO - 1:HALO - 1 + ts, 2 * w:3 * w], mu[2:3])
    lo = mixed(lbuf[HALO:HALO + ts, :], lbuf[HALO - 1:HALO - 1 + ts, :], mul_ref[...])
    wl, al = lo[:, 0:LORA], lo[:, LORA:2 * LORA]

    z = w0_ref[...] + _dot(jnp.tanh(wl).astype(BF16), w2_ref[...])
    ld_ref[...] = -float(np.exp(-0.5)) * _sigmoid(z)
    a = _sigmoid(a0_ref[...] + _dot(al.astype(BF16), a2_ref[...]))

    kk = k * kkw_ref[...]
    hb = hb_ref[...]
    gw = hb.shape[0]
    kk2 = kk * kk
    n2 = jnp.concatenate([_head_sums(kk2[:, l0:l0 + gw], hb) for l0 in range(0, w, gw)], axis=1)
    kk = kk * lax.rsqrt(jnp.maximum(n2, 1e-24))
    ro_ref[...] = r.astype(ro_ref.dtype)
    ko_ref[...] = (k * (1.0 + (a - 1.0) * kaw_ref[...])).astype(ko_ref.dtype)
    vo_ref[...] = v.astype(vo_ref.dtype)
    na_ref[...] = (-kk).astype(na_ref.dtype)
    bv_ref[...] = (kk * a).astype(bv_ref.dtype)

    buf[0:HALO, :] = buf[ts:ts + HALO, :]
    lbuf[0:HALO, :] = lbuf[ts:ts + HALO, :]


def _rwkv_prep(u3, lora, mu3, mul, w0, w2, a0, a2, kkw, kaw, w, col0):
    b, s, _ = u3.shape
    ts = _tile(s, 256)
    c0 = col0 // w
    gw = _tile(w, MXU_DIM)
    head_of = np.arange(gw) // HEAD
    hb = jnp.asarray(head_of[:, None] == head_of[None, :], BF16)
    row = lambda k: pl.BlockSpec((None, ts, w), lambda i, t, k=k: (i, t, c0 + k))
    full = lambda shape: pl.BlockSpec(shape, lambda i, t: (0,) * len(shape))
    outw = pl.BlockSpec((None, ts, w), lambda i, t: (i, t, 0))
    sds = lambda dt: jax.ShapeDtypeStruct((b, s, w), dt)
    return pl.pallas_call(
        _rwkv_prep_kernel,
        grid=(b, s // ts),
        in_specs=[row(0), row(1), row(2),
                  pl.BlockSpec((None, ts, 2 * LORA), lambda i, t: (i, t, 0)),
                  full((3, w)), full((1, 2 * LORA)), full((1, w)), full((LORA, w)),
                  full((1, w)), full((LORA, w)), full((1, w)), full((1, w)),
                  full((gw, gw))],
        out_specs=[outw] * 6,
        out_shape=[sds(BF16)] * 5 + [sds(F32)],
        scratch_shapes=[pltpu.VMEM((HALO + ts, 3 * w), F32), pltpu.VMEM((HALO + ts, 2 * LORA), F32)],
        compiler_params=_params(("parallel", "arbitrary")),
        name="rwkv_prep",
    )(u3, u3, u3, lora, mu3, mul, w0, w2, a0, a2, kkw, kaw, hb)


def _head_stack(x, lane_head):
    zero = jnp.zeros_like(x)
    return jnp.concatenate([jnp.where(lane_head == 0, x, zero), jnp.where(lane_head == 1, x, zero)], axis=0)


SUM_PIECES = 2
RWKV_UNROLL = 8


def _rwkv_scan_kernel(r_ref, k_ref, v_ref, na_ref, bv_ref, ld_ref, g_ref, lng_ref, lnb_ref, rk_ref,
                      cl_ref, cu_ref, hb_ref, o_ref, state, ybuf, dec_s, rh_s, ah_s, bt_s, kt_s, bb_s, kb_s):
    ts, width = o_ref.shape
    n_pairs = width // LANES
    c = RWKV_CHUNK
    n_chunks = ts // c

    @pl.when(pl.program_id(2) == 0)
    def _():
        state[...] = jnp.zeros_like(state)

    sb = cl_ref.shape[0]
    lcs, lrests = [], []
    for r0 in range(0, ts, sb):
        pieces = jnp.concatenate(_bf16_pieces(ld_ref[r0:r0 + sb, :], SUM_PIECES), axis=1)
        for m_ref, outs in ((cl_ref, lcs), (cu_ref, lrests)):
            both = _dot(m_ref[...], pieces)
            outs.append(both[:, 0:width] + both[:, width:2 * width])
    ld = ld_ref[...]
    lc = jnp.concatenate(lcs, axis=0)
    lrest = jnp.concatenate(lrests, axis=0)
    e_in = jnp.exp(lc)
    e_neg = jnp.exp(-lc)
    e_end = jnp.exp(lrest)
    r_all = r_ref[...].astype(F32)
    k_all = k_ref[...].astype(F32)
    na_all = na_ref[...].astype(F32)
    bv_all = bv_ref[...].astype(F32)
    dec_s[...] = e_in
    rh_s[...] = (r_all * e_in).astype(BF16)
    ah_s[...] = (na_all * jnp.exp(lc - ld)).astype(BF16)
    bt_s[...] = (bv_all * e_neg).astype(BF16)
    kt_s[...] = (k_all * e_neg).astype(BF16)
    bb_s[...] = (bv_all * e_end).astype(BF16)
    kb_s[...] = (k_all * e_end).astype(BF16)

    lane_head = lax.broadcasted_iota(jnp.int32, (1, LANES), 1) // HEAD
    t_idx = lax.broadcasted_iota(jnp.int32, (c, LANES), 0)
    s_idx = lax.broadcasted_iota(jnp.int32, (c, LANES), 1) % c
    strict = s_idx < t_idx
    incl = s_idx <= t_idx
    blk = (lax.broadcasted_iota(jnp.int32, (LANES, LANES), 0) // HEAD
           == lax.broadcasted_iota(jnp.int32, (LANES, LANES), 1) // HEAD)
    eye_cat = (s_idx == t_idx).astype(F32)

    def chunk(ci, states):
        rows = pl.ds(pl.multiple_of(ci * c, c), c)
        tail_rows = pl.ds(pl.multiple_of(ci * c + c - 8, 8), 8)
        pairs = range(n_pairs)
        lanes = [slice(pi * LANES, (pi + 1) * LANES) for pi in pairs]
        each = lambda f: [f(pi) for pi in pairs]
        v = each(lambda pi: v_ref[rows, lanes[pi]])
        rh = each(lambda pi: rh_s[rows, lanes[pi]])
        ah = each(lambda pi: ah_s[rows, lanes[pi]])
        sc = each(lambda pi: _dot_nt(
            jnp.concatenate([ah[pi], rh[pi]], axis=0),
            jnp.concatenate([_head_stack(bt_s[rows, lanes[pi]], lane_head),
                             _head_stack(kt_s[rows, lanes[pi]], lane_head)], axis=0)))
        a_ak = each(lambda pi: jnp.where(strict, sc[pi][0:c, LANES:2 * LANES], 0.0).astype(BF16))
        a_rb = each(lambda pi: jnp.where(incl, sc[pi][c:2 * c, 0:LANES], 0.0).astype(BF16))
        a_rk = each(lambda pi: jnp.where(incl, sc[pi][c:2 * c, LANES:2 * LANES], 0.0).astype(BF16))

        x = each(lambda pi: jnp.where(strict, sc[pi][0:c, 0:LANES], 0.0))
        tinv = each(lambda pi: eye_cat + x[pi])
        p2 = 2
        while p2 < c:
            xb = each(lambda pi: x[pi].astype(BF16))
            x = each(lambda pi: _dot(xb[pi], _head_stack(xb[pi], lane_head)))
            tinv = each(lambda pi: tinv[pi] + _dot(tinv[pi].astype(BF16),
                                                   _head_stack(x[pi].astype(BF16), lane_head)))
            p2 *= 2
        t_cat = each(lambda pi: tinv[pi].astype(BF16))

        g2b = each(lambda pi: states[pi].astype(BF16))
        vs = each(lambda pi: _head_stack(v[pi], lane_head))
        p = each(lambda pi: _dot_nt(ah[pi], g2b[pi]) + _dot(a_ak[pi], vs[pi]))
        ub = each(lambda pi: _dot(t_cat[pi], _head_stack(p[pi].astype(BF16), lane_head)).astype(BF16))
        new_states = []
        for pi in pairs:
            y = (_dot_nt(rh[pi], g2b[pi]) + _dot(a_rb[pi], _head_stack(ub[pi], lane_head))
                 + _dot(a_rk[pi], vs[pi]))
            ybuf[rows, lanes[pi]] = y
            uv_t = jnp.concatenate([ub[pi], v[pi]], axis=0).T
            upd = _dot(uv_t, jnp.concatenate([bb_s[rows, lanes[pi]], kb_s[rows, lanes[pi]]], axis=0))
            decay = dec_s[tail_rows, lanes[pi]][7:8, :]
            new_states.append(states[pi] * decay + jnp.where(blk, upd, 0.0))
        return tuple(new_states)

    final = lax.fori_loop(0, n_chunks, chunk, tuple(state[pi] for pi in range(n_pairs)),
                          unroll=min(n_chunks, RWKV_UNROLL))
    for pi in range(n_pairs):
        state[pi] = final[pi]

    hb = hb_ref[...]
    gw = hb.shape[0]
    for l0 in range(0, width, gw):
        lanes = slice(l0, l0 + gw)
        y = ybuf[:, lanes]
        mean = _head_sums(y, hb) * (1.0 / HEAD)
        d = y - mean
        var = _head_sums(d * d, hb) * (1.0 / HEAD)
        yn = d * lax.rsqrt(var + RWKV_LN_EPS) * lng_ref[:, lanes] + lnb_ref[:, lanes]
        rk = r_ref[:, lanes].astype(F32) * k_ref[:, lanes].astype(F32) * rk_ref[:, lanes]
        bonus = _head_sums(rk, hb) * v_ref[:, lanes].astype(F32)
        o_ref[:, lanes] = ((yn + bonus) * _silu(g_ref[:, lanes].astype(F32))).astype(o_ref.dtype)


def _head_sums(x, m):
    return functools.reduce(jnp.add, [_dot(p, m) for p in _bf16_pieces(x, SUM_PIECES)])


RWKV_PAIRS = 8


def _rwkv_scan(prep, u3, ln_g, ln_b, r_k, w, gate_col0):
    r, k, v, na, bv, ld = prep
    b, s, _ = r.shape
    ts = _tile(s, 512)
    width = LANES * _tile(w // LANES, RWKV_PAIRS)
    gc0 = gate_col0 // width
    sb = _tile(ts, MXU_DIM)
    chunk_of = np.arange(sb) // RWKV_CHUNK
    same = chunk_of[:, None] == chunk_of[None, :]
    tri = np.arange(sb)[:, None] >= np.arange(sb)[None, :]
    cl = jnp.asarray(same & tri, BF16)
    cu = jnp.asarray(same & ~tri, BF16)
    gw = _tile(width, MXU_DIM)
    head_of = np.arange(gw) // HEAD
    hb = jnp.asarray(head_of[:, None] == head_of[None, :], BF16)
    row = pl.BlockSpec((None, ts, width), lambda i, p, t: (i, t, p))
    vec = pl.BlockSpec((1, width), lambda i, p, t: (0, p))
    const = lambda n: pl.BlockSpec((n, n), lambda i, p, t: (0, 0))
    tile_bf16 = pltpu.VMEM((ts, width), BF16)
    return pl.pallas_call(
        _rwkv_scan_kernel,
        grid=(b, w // width, s // ts),
        in_specs=[row] * 6 + [pl.BlockSpec((None, ts, width), lambda i, p, t: (i, t, gc0 + p)),
                              vec, vec, vec, const(sb), const(sb), const(gw)],
        out_specs=row,
        out_shape=jax.ShapeDtypeStruct((b, s, w), BF16),
        scratch_shapes=[pltpu.VMEM((width // LANES, LANES, LANES), F32), pltpu.VMEM((ts, width), F32),
                        pltpu.VMEM((ts, width), F32)] + [tile_bf16] * 6,
        compiler_params=_params(("parallel", "parallel", "arbitrary")),
        name="rwkv_scan",
    )(r, k, v, na, bv, ld, u3, ln_g, ln_b, r_k, cl, cu, hb)


FOX_BLOCK = 1024
FOX_SUB = 16
FOX_ACC_ROWS = HEAD + 16


def _fold8(x, op):
    parts = [x[i:i + 8, :] for i in range(0, x.shape[0], 8)]
    return functools.reduce(op, parts)


def _fox_kernel(qi_ref, kj_ref, q_ref, k_ref, v_ref, kc_ref, g_ref, o_ref,
                m_ref, acc_ref, qx_ref, s0_ref, s1_ref, p0_ref, p1_ref):
    pair, step = pl.program_id(1), pl.program_id(2)
    qi, kj = qi_ref[step], kj_ref[step]
    tq, tk = q_ref.shape[0], k_ref.shape[0]
    lane = lax.broadcasted_iota(jnp.int32, (1, LANES), 1)

    @pl.when(kj == 0)
    def _():
        m_ref[...] = jnp.full_like(m_ref, NEG_BIG)
        acc_ref[...] = jnp.zeros_like(acc_ref)
        q2 = q_ref[...]
        for h in range(2):
            first = _forget_lane(2 * pair + h, 0)
            ones = jnp.where((lane >= first) & (lane < first + C_PIECES), 1.0, 0.0).astype(BF16)
            qx_ref[h, :, 0:LANES] = jnp.where(lane // HEAD == h, q2, jnp.zeros_like(q2))
            qx_ref[h, :, LANES:2 * LANES] = jnp.broadcast_to(ones, q2.shape)

    s_refs, p_refs = (s0_ref, s1_ref), (p0_ref, p1_ref)

    def step_body(masked):
        kx = jnp.concatenate([k_ref[...], kc_ref[...]], axis=1)
        half = tk // 2
        block_max = [None, None]
        for h in range(2):
            if masked:
                s_refs[h][0:half, :] = _dot_nt(kx[0:half, :], qx_ref[h])
                s_refs[h][half:, half:] = _dot_nt(kx[half:, :], qx_ref[h, half:, :])
            else:
                s_val = _dot_nt(kx, qx_ref[h])
                s_refs[h][...] = s_val
                block_max[h] = _fold8(s_val, jnp.maximum)
        n_sub, n_grp = tk // FOX_SUB, tq // LANES

        def first_group(kb):
            return (kb * FOX_SUB) // LANES if masked else 0

        def group_scores(h, kb, g):
            blk = s_refs[h][kb * FOX_SUB:(kb + 1) * FOX_SUB, g * LANES:(g + 1) * LANES]
            if masked and g == first_group(kb):
                key = kb * FOX_SUB + lax.broadcasted_iota(jnp.int32, (FOX_SUB, LANES), 0)
                query = g * LANES + lax.broadcasted_iota(jnp.int32, (FOX_SUB, LANES), 1)
                blk = jnp.where(key <= query, blk, NEG_BIG)
            return blk

        def scores(h, kb, g0):
            return jnp.concatenate([group_scores(h, kb, g) for g in range(g0, n_grp)], axis=1)

        def running_max(h):
            if masked:
                mx = [None] * n_grp
                for kb in range(n_sub):
                    g0 = first_group(kb)
                    part = _fold8(scores(h, kb, g0), jnp.maximum)
                    for g in range(g0, n_grp):
                        piece = part[:, (g - g0) * LANES:(g - g0 + 1) * LANES]
                        mx[g] = piece if mx[g] is None else jnp.maximum(mx[g], piece)
                mx8 = jnp.concatenate(mx, axis=1)
            else:
                mx8 = block_max[h]
            m_prev = m_ref[h]
            m_new = jnp.maximum(m_prev, jnp.max(mx8, axis=0, keepdims=True))
            m_ref[h] = m_new
            return m_new, jnp.exp2(m_prev - m_new)

        def probabilities(h, m_new):
            for kb in range(n_sub):
                g0 = first_group(kb)
                rows = slice(kb * FOX_SUB, (kb + 1) * FOX_SUB)
                p_refs[h][rows, g0 * LANES:] = jnp.exp2(scores(h, kb, g0) - m_new[:, g0 * LANES:]).astype(BF16)
                z0 = half if kb * FOX_SUB >= half else 0
                if g0 * LANES > z0:
                    p_refs[h][rows, z0:g0 * LANES] = jnp.zeros((FOX_SUB, g0 * LANES - z0), BF16)

        def accumulate(h, alpha):
            lhs = jnp.concatenate([v_t[h * HEAD:(h + 1) * HEAD, :], sum_rows], axis=0)
            if masked:
                acc_ref[h] = alpha * acc_ref[h] + _dot(lhs[:, 0:half], p_refs[h][0:half, :])
                acc_ref[h, :, half:] += _dot(lhs[:, half:], p_refs[h][half:, half:])
            else:
                acc_ref[h] = alpha * acc_ref[h] + _dot(lhs, p_refs[h][...])

        eye = (lax.broadcasted_iota(jnp.int32, (LANES, LANES), 0)
               == lax.broadcasted_iota(jnp.int32, (LANES, LANES), 1)).astype(BF16)
        v_t = _dot_nt(eye, v_ref[...]).astype(BF16)
        sum_rows = jnp.ones((FOX_ACC_ROWS - HEAD, tk), BF16)
        m0, alpha0 = running_max(0)
        probabilities(0, m0)
        m1, alpha1 = running_max(1)
        accumulate(0, alpha0)
        probabilities(1, m1)
        accumulate(1, alpha1)

    @pl.when(kj < qi)
    def _():
        step_body(False)

    @pl.when(kj == qi)
    def _():
        step_body(True)
        outs = [acc_ref[h, 0:HEAD, :] * (1.0 / acc_ref[h, HEAD:HEAD + 1, :]) for h in range(2)]
        out = jnp.concatenate(outs, axis=0).T
        o_ref[...] = (out * _silu(g_ref[...].astype(F32))).astype(o_ref.dtype)


def _fox_branch(u3, kc, w, q_col0, gate_col0):
    b, s, _ = u3.shape
    tq = tk = _tile(s, FOX_BLOCK)
    n_pairs = w // LANES
    pairs = [(i, j) for i in range(s // tq) for j in range(i + 1)]
    qi = jnp.asarray([p[0] for p in pairs], jnp.int32)
    kj = jnp.asarray([p[1] for p in pairs], jnp.int32)
    qc0, gc0 = q_col0 // LANES, gate_col0 // LANES
    grid_spec = pltpu.PrefetchScalarGridSpec(
        num_scalar_prefetch=2,
        grid=(b, n_pairs, len(pairs)),
        in_specs=[
            pl.BlockSpec((None, tq, LANES), lambda i, p, t, qi, kj: (i, qi[t], qc0 + p)),
            pl.BlockSpec((None, tk, LANES), lambda i, p, t, qi, kj: (i, kj[t], qc0 + n_pairs + p)),
            pl.BlockSpec((None, tk, LANES), lambda i, p, t, qi, kj: (i, kj[t], qc0 + 2 * n_pairs + p)),
            pl.BlockSpec((None, tk, LANES), lambda i, p, t, qi, kj: (i, kj[t], 0)),
            pl.BlockSpec((None, tq, LANES), lambda i, p, t, qi, kj: (i, qi[t], gc0 + p)),
        ],
        out_specs=pl.BlockSpec((None, tq, LANES), lambda i, p, t, qi, kj: (i, qi[t], p)),
        scratch_shapes=[pltpu.VMEM((2, 1, tq), F32),
                        pltpu.VMEM((2, FOX_ACC_ROWS, tq), F32), pltpu.VMEM((2, tq, 2 * LANES), BF16),
                        pltpu.VMEM((tk, tq), F32), pltpu.VMEM((tk, tq), F32),
                        pltpu.VMEM((tk, tq), BF16), pltpu.VMEM((tk, tq), BF16)],
    )
    return pl.pallas_call(
        _fox_kernel,
        grid_spec=grid_spec,
        out_shape=jax.ShapeDtypeStruct((b, s, w), BF16),
        compiler_params=_params(("parallel", "parallel", "arbitrary")),
        name="fox_attention",
    )(qi, kj, u3, u3, u3, kc, u3)


def _merge_kernel(ya_ref, yb_ref, yc_ref, yd_ref, ma_ref, mb_ref, mc_ref, md_ref, bm_ref, wb_ref, o_ref):
    acc = None
    branches = ((ya_ref, ma_ref), (yb_ref, mb_ref), (yc_ref, mc_ref), (yd_ref, md_ref))
    for kbr, (y_ref, ml_ref) in enumerate(branches):
        proj = _dot(y_ref[...], wb_ref[kbr])
        gate = _sigmoid(ml_ref[...].astype(F32) + bm_ref[kbr:kbr + 1, :])
        acc = gate * proj if acc is None else acc + gate * proj
    o_ref[...] = acc.astype(o_ref.dtype)


def _merge(ys, u2, b_merge, w_branch_all, layer, ml_col0):
    t, w = ys[0].shape
    _, nb, _, d = w_branch_all.shape
    tm, tn = _tile(t, 1024), _tile(d, 512)
    nd = d // tn
    mc0 = ml_col0 // tn
    yspec = pl.BlockSpec((tm, w), lambda i, j: (i, 0))
    mspec = lambda k: pl.BlockSpec((tm, tn), lambda i, j, k=k: (i, mc0 + k * nd + j))
    return pl.pallas_call(
        _merge_kernel,
        grid=(t // tm, nd),
        in_specs=[yspec] * 4 + [mspec(k) for k in range(4)] + [
            pl.BlockSpec((nb, tn), lambda i, j: (0, j)),
            pl.BlockSpec((None, nb, w, tn), lambda i, j: (layer, 0, 0, j)),
        ],
        out_specs=pl.BlockSpec((tm, tn), lambda i, j: (i, j)),
        out_shape=jax.ShapeDtypeStruct((t, d), BF16),
        compiler_params=_params(("parallel", "arbitrary")),
        name="merge",
    )(*ys, u2, u2, u2, u2, b_merge, w_branch_all)


def _outproj_kernel(m_ref, w_ref, x_ref, o_ref):
    o_ref[...] = x_ref[...] + _dot(m_ref[...], w_ref[...])


def _outproj(merged, w_out_all, layer, x2):
    t, d = x2.shape
    tm, tn = _tile(t, 1024), _tile(d, 1024)
    return pl.pallas_call(
        _outproj_kernel,
        grid=(t // tm, d // tn),
        in_specs=[
            pl.BlockSpec((tm, d), lambda i, j: (i, 0)),
            pl.BlockSpec((None, d, tn), lambda i, j: (layer, 0, j)),
            pl.BlockSpec((tm, tn), lambda i, j: (i, j)),
        ],
        out_specs=pl.BlockSpec((tm, tn), lambda i, j: (i, j)),
        out_shape=jax.ShapeDtypeStruct((t, d), F32),
        compiler_params=_params(("parallel", "arbitrary")),
        name="outproj",
    )(merged, w_out_all, x2)


def _final_norm_kernel(x_ref, g_ref, o_ref):
    o_ref[...] = _rms_rows(x_ref[...], g_ref[...])


def _final_norm(x2, g):
    t, d = x2.shape
    tm = _tile(t, 512)
    return pl.pallas_call(
        _final_norm_kernel,
        grid=(t // tm,),
        in_specs=[pl.BlockSpec((tm, d), lambda i: (i, 0)), pl.BlockSpec((1, d), lambda i: (0, 0))],
        out_specs=pl.BlockSpec((tm, d), lambda i: (i, 0)),
        out_shape=jax.ShapeDtypeStruct((t, d), F32),
        compiler_params=_params(("parallel",)),
        name="final_norm",
    )(x2, g)


def _split_in_weights(w_in, w, nh):
    w_t = jnp.swapaxes(w_in, 1, 2)
    lo0, f0 = 7 * w, 11 * w + 2 * LORA
    n_main = w_in.shape[2] - 2 * LORA - nh
    segments = ((0, 0), (lo0, lo0 + 2 * LORA), (f0 - 2 * LORA, f0 + nh))
    col_scale = np.ones((1, n_main), np.float32)
    col_scale[:, 8 * w:9 * w] = HEAD ** -0.5 * LOG2E
    pad = jnp.zeros((w_t.shape[0], LANES - nh, w_t.shape[2]), w_t.dtype)
    w_small = jnp.concatenate([w_t[:, lo0:lo0 + 2 * LORA], w_t[:, f0:f0 + nh], pad], axis=1).astype(BF16)
    return w_t.astype(BF16), segments, jnp.asarray(col_scale), w_small


def _layer(x2, bsz, seq, layer, w_in_all, segments, col_scale, w_small, w_branch_all, w_out_all, norm_g, b_merge,
           conv_w, rwkv_mu, rwkv_w0, rwkv_w2, rwkv_a0, rwkv_a2, rwkv_kk, rwkv_ka, rwkv_rk, rwkv_ln_g, rwkv_ln_b,
           fox_bf, pool_w, pool_scale):
    t, d = x2.shape
    w = conv_w.shape[1]
    nh = fox_bf.shape[0]
    b_f = jnp.concatenate([fox_bf, jnp.zeros((LANES - nh,), fox_bf.dtype)]).reshape(1, LANES)
    row = lambda a: a.reshape(1, -1)

    u2 = _inproj(x2, row(norm_g), w_in_all, layer, segments, col_scale)
    u3 = u2.reshape(bsz, seq, -1)
    lora, kc = _side(x2.reshape(bsz, seq, d), row(norm_g), w_small, b_f, w)
    y_a, y_d = _conv_pool_branches(u3, conv_w, pool_w.astype(BF16), row(pool_scale), w, 0, 12 * w)
    prep = _rwkv_prep(u3, lora, rwkv_mu[:3 * w].reshape(3, w), row(rwkv_mu[3 * w:]), row(rwkv_w0),
                      rwkv_w2.astype(BF16), row(rwkv_a0), rwkv_a2.astype(BF16), row(rwkv_kk), row(rwkv_ka),
                      w, 4 * w)
    y_b = _rwkv_scan(prep, u3, row(rwkv_ln_g), row(rwkv_ln_b), row(rwkv_rk), w, 7 * w)
    y_c = _fox_branch(u3, kc, w, 8 * w, 11 * w)
    ys = [y.reshape(t, w) for y in (y_a, y_b, y_c, y_d)]
    merged = _merge(ys, u2, b_merge, w_branch_all, layer, 14 * w)
    return _outproj(merged, w_out_all, layer, x2)


def kernel(x, norm_g, w_in, b_merge, conv_w, rwkv_mu, rwkv_w0, rwkv_w2, rwkv_a0, rwkv_a2, rwkv_kk, rwkv_ka,
           rwkv_rk, rwkv_ln_g, rwkv_ln_b, fox_bf, pool_w, pool_scale, w_branch, w_out, final_g):
    bsz, seq, d = x.shape
    x2 = x.reshape(bsz * seq, d)
    w_in_all, segments, col_scale, w_small_all = _split_in_weights(w_in, conv_w.shape[2], fox_bf.shape[1])
    w_branch_all = w_branch.astype(BF16)
    w_out_all = w_out.astype(BF16)
    for l in range(norm_g.shape[0]):
        x2 = _layer(x2, bsz, seq, l, w_in_all, segments, col_scale, w_small_all[l], w_branch_all, w_out_all,
                    norm_g[l], b_merge[l], conv_w[l], rwkv_mu[l], rwkv_w0[l], rwkv_w2[l], rwkv_a0[l], rwkv_a2[l],
                    rwkv_kk[l], rwkv_ka[l], rwkv_rk[l], rwkv_ln_g[l], rwkv_ln_b[l], fox_bf[l], pool_w[l],
                    pool_scale[l])
    return _final_norm(x2, final_g.reshape(1, d)).reshape(bsz, seq, d)
```
